```python
import jax, jax.numpy as jnp
from jax import lax
import numpy as np

D_MODEL = 1024
BATCH = 32
SEQ = 2048
DEPTH = 4

CHUNK = 64
N_MIXERS = 2
N_HGRN_LAYERS = (DEPTH + N_MIXERS - 1) // N_MIXERS
N_RET_LAYERS = DEPTH // N_MIXERS

HG_DK = 128
HG_HEADS = D_MODEL // HG_DK
HG_DV = D_MODEL // HG_HEADS
HG_BLOCK = 16

RET_DK = 256
RET_HEADS = D_MODEL // RET_DK
RET_DV = 2 * RET_DK
ROPE_BASE = 10000.0
MAX_POS_OFFSET = 4096

N_EXPERTS = 32
TOP_K = 4
D_EXPERT = D_MODEL
SWIGLU_ALPHA = 1.702
SWIGLU_LIMIT = 7.0
MOE_BLOCK = 512

EPS = 1e-6

kernel_name = 'hybrid_hgrn2_retention_moe_adaln'


def rms_norm(x, g=None):
    x32 = x.astype(jnp.float32)
    y = x32 * lax.rsqrt(jnp.mean(x32 * x32, axis=-1, keepdims=True) + EPS)
    if g is not None:
        y = y * g.astype(jnp.float32)
    return y.astype(x.dtype)


def hgrn_lower_bounds(lb_logits):
    p = jax.nn.softmax(lb_logits.astype(jnp.float32), axis=0)
    cum = jnp.cumsum(p, axis=0)
    return cum - cum[0:1]


def hgrn2_mixer(h, w_in, w_out, o_gain, lb):
    B, S, _ = h.shape
    hk = HG_HEADS * HG_DK
    hv = HG_HEADS * HG_DV
    proj = h @ w_in
    q, f, i, g = jnp.split(proj, [hk, 2 * hk, 2 * hk + hv], axis=-1)
    f = f.astype(jnp.float32)
    log_f = jnp.logaddexp(jnp.log(lb), jnp.log1p(-lb) + jax.nn.log_sigmoid(f))
    k = (1.0 - lb) * jax.nn.sigmoid(-f)
    q = jax.nn.silu(q.astype(jnp.float32))
    v = i.astype(jnp.float32)
    L = HG_BLOCK
    nb = S // L

    def to_blocks(t, d):
        return t.reshape(B, nb, L, HG_HEADS, d).transpose(1, 0, 3, 2, 4)

    qb, kb, gb = to_blocks(q, HG_DK), to_blocks(k, HG_DK), to_blocks(log_f, HG_DK)
    vb = to_blocks(v, HG_DV)
    tri = jnp.tril(jnp.ones((L, L), dtype=bool))[:, :, None]

    def step(state, blk):
        qc, kc, vc, gc = blk
        b = jnp.cumsum(gc, axis=2)
        o = jnp.einsum('bhtd,bhdv->bhtv', qc * jnp.exp(b), state)
        diff = b[:, :, :, None, :] - b[:, :, None, :, :]
        decay = jnp.where(tri, jnp.exp(jnp.minimum(diff, 0.0)), 0.0)
        att = jnp.einsum('bhtd,bhsd,bhtsd->bhts', qc, kc, decay)
        o = o + jnp.einsum('bhts,bhsv->bhtv', att, vc)
        b_last = b[:, :, -1:, :]
        state = jnp.exp(b_last[:, :, 0, :, None]) * state + jnp.einsum('bhsd,bhsv->bhdv', kc * jnp.exp(b_last - b), vc)
        return state, o

    s0 = jnp.zeros((B, HG_HEADS, HG_DK, HG_DV), jnp.float32)
    _, o = lax.scan(step, s0, (qb, kb, vb, gb))
    o = o.transpose(1, 0, 3, 2, 4).reshape(B, S, HG_HEADS, HG_DV)
    o = rms_norm(o, o_gain) * jax.nn.silu(g.astype(jnp.float32).reshape(B, S, HG_HEADS, HG_DV))
    return o.reshape(B, S, hv).astype(h.dtype) @ w_out


def rotate(x, cos, sin):
    half = x.shape[-1] // 2
    x1, x2 = x[..., :half], x[..., half:]
    return jnp.concatenate([x1 * cos - x2 * sin, x2 * cos + x1 * sin], axis=-1)


def retention_mixer(h, positions, w_in, w_out):
    B, S, _ = h.shape
    hq = RET_HEADS * RET_DK
    hv = RET_HEADS * RET_DV
    proj = h @ w_in
    q, k, v, g = jnp.split(proj, [hq, 2 * hq, 2 * hq + hv], axis=-1)
    q = q.astype(jnp.float32).reshape(B, S, RET_HEADS, RET_DK)
    k = k.astype(jnp.float32).reshape(B, S, RET_HEADS, RET_DK)
    v = v.astype(jnp.float32).reshape(B, S, RET_HEADS, RET_DV)
    inv_freq = 1.0 / (ROPE_BASE ** jnp.linspace(0.0, 1.0, RET_DK // 2, dtype=jnp.float32))
    ang = positions.astype(jnp.float32)[..., None] * inv_freq
    cos = jnp.cos(ang)[:, :, None, :]
    sin = jnp.sin(ang)[:, :, None, :]
    q = rotate(q, cos, sin)
    k = rotate(k, cos, sin) * (RET_DK ** -0.5)
    log_gamma = jnp.log(1.0 - 2.0 ** (-5.0 - jnp.arange(RET_HEADS, dtype=jnp.float32)))
    idx = jnp.arange(CHUNK, dtype=jnp.float32)
    dmat = jnp.exp(jnp.abs(idx[:, None] - idx[None, :])[None] * log_gamma[:, None, None])
    q_decay = jnp.exp((idx + 1.0)[None] * log_gamma[:, None])[None, :, :, None]
    k_decay = jnp.exp((CHUNK - 1.0 - idx)[None] * log_gamma[:, None])[None, :, :, None]
    chunk_decay = jnp.exp(CHUNK * log_gamma)[None, :, None, None]
    nc = S // CHUNK

    def to_chunks(t, d):
        return t.reshape(B, nc, CHUNK, RET_HEADS, d).transpose(1, 0, 3, 2, 4)

    qc_all, kc_all, vc_all = to_chunks(q, RET_DK), to_chunks(k, RET_DK), to_chunks(v, RET_DV)

    def step(R, blk):
        qc, kc, vc = blk
        sc = jnp.einsum('bhnd,bhmd->bhnm', qc, kc) * dmat
        o = jnp.einsum('bhnm,bhmv->bhnv', sc, vc) + jnp.einsum('bhnd,bhdv->bhnv', qc, R) * q_decay
        R = chunk_decay * R + jnp.einsum('bhmd,bhmv->bhdv', kc * k_decay, vc)
        return R, o

    R0 = jnp.zeros((B, RET_HEADS, RET_DK, RET_DV), jnp.float32)
    _, o = lax.scan(step, R0, (qc_all, kc_all, vc_all))
    o = o.transpose(1, 0, 3, 2, 4).reshape(B, S, RET_HEADS, RET_DV)
    o = rms_norm(o) * jax.nn.silu(g.astype(jnp.float32).reshape(B, S, RET_HEADS, RET_DV))
    return o.reshape(B, S, hv).astype(h.dtype) @ w_out


def moe_ffn(h, router_w, router_b, w1, b1, w2, b2):
    N, D = h.shape
    logits = (h @ router_w + router_b).astype(jnp.float32)
    top_logit, top_idx = lax.top_k(logits, TOP_K)
    gate = jax.nn.softmax(top_logit, axis=-1)
    n_assign = N * TOP_K
    flat_e = top_idx.reshape(-1)
    order = jnp.argsort(flat_e)
    e_sorted = flat_e[order]
    tok_sorted = (order // TOP_K).astype(jnp.int32)
    w_sorted = gate.reshape(-1)[order]
    counts = jnp.bincount(flat_e, length=N_EXPERTS)
    padded = (counts + MOE_BLOCK - 1) // MOE_BLOCK * MOE_BLOCK
    pad_end = jnp.cumsum(padded)
    pad_start = pad_end - padded
    start = jnp.cumsum(counts) - counts
    dest = pad_start[e_sorted] + jnp.arange(n_assign, dtype=jnp.int32) - start[e_sorted]
    n_blocks = -(-n_assign // MOE_BLOCK) + N_EXPERTS
    n_rows = n_blocks * MOE_BLOCK
    row_tok = jnp.zeros((n_rows,), jnp.int32).at[dest].set(tok_sorted)
    row_w = jnp.zeros((n_rows,), jnp.float32).at[dest].set(w_sorted)
    block_e = jnp.minimum(jnp.searchsorted(pad_end, jnp.arange(n_blocks, dtype=jnp.int32) * MOE_BLOCK, side='right'), N_EXPERTS - 1)
    xb = h[row_tok].reshape(n_blocks, MOE_BLOCK, D)
    wb = row_w.reshape(n_blocks, MOE_BLOCK).astype(h.dtype)

    def expert_block(args):
        xe, we, e = args
        u = xe @ w1[e] + b1[e]
        glu, lin = jnp.split(u, 2, axis=-1)
        glu = jnp.minimum(glu, SWIGLU_LIMIT)
        lin = jnp.clip(lin, -SWIGLU_LIMIT, SWIGLU_LIMIT)
        a = glu * jax.nn.sigmoid(SWIGLU_ALPHA * glu) * (lin + 1.0)
        return (a @ w2[e] + b2[e]) * we[:, None]

    y = lax.map(expert_block, (xb, wb, block_e)).reshape(n_rows, D)
    return jax.ops.segment_sum(y, row_tok, num_segments=N)


def setup_inputs(seed: int = 0) -> dict:
    key = jax.random.key(seed)
    ks = jax.random.split(key, 24)
    f32 = jnp.float32

    def nrm(k, shape, scale):
        return jax.random.normal(k, shape, f32) * scale

    hk = HG_HEADS * HG_DK
    hv = HG_HEADS * HG_DV
    rq = RET_HEADS * RET_DK
    rv = RET_HEADS * RET_DV
    x = nrm(ks[0], (BATCH, SEQ, D_MODEL), 1.0)
    c = nrm(ks[1], (BATCH, D_MODEL), 1.0)
    offsets = jax.random.randint(ks[2], (BATCH, 1), 0, MAX_POS_OFFSET, dtype=jnp.int32)
    positions = offsets + jnp.arange(SEQ, dtype=jnp.int32)[None, :]
    return {
        'x': x,
        'c': c,
        'positions': positions,
        'ada_w': nrm(ks[3], (DEPTH, D_MODEL, 6 * D_MODEL), 0.5 * D_MODEL ** -0.5),
        'ada_b': nrm(ks[4], (DEPTH, 6 * D_MODEL), 0.02),
        'norm1_g': 1.0 + nrm(ks[5], (DEPTH, D_MODEL), 0.02),
        'norm2_g': 1.0 + nrm(ks[6], (DEPTH, D_MODEL), 0.02),
        'hgrn_w_in': nrm(ks[7], (N_HGRN_LAYERS, D_MODEL, 2 * hk + 2 * hv), D_MODEL ** -0.5),
        'hgrn_w_out': nrm(ks[8], (N_HGRN_LAYERS, hv, D_MODEL), hv ** -0.5),
        'hgrn_o_gain': 1.0 + nrm(ks[9], (N_HGRN_LAYERS, HG_DV), 0.02),
        'hgrn_lb_logits': nrm(ks[10], (N_HGRN_LAYERS, hk), 0.5),
        'ret_w_in': nrm(ks[11], (N_RET_LAYERS, D_MODEL, 2 * rq + 2 * rv), D_MODEL ** -0.5),
        'ret_w_out': nrm(ks[12], (N_RET_LAYERS, rv, D_MODEL), rv ** -0.5),
        'router_w': nrm(ks[13], (DEPTH, D_MODEL, N_EXPERTS), D_MODEL ** -0.5),
        'router_b': nrm(ks[14], (DEPTH, N_EXPERTS), 0.01),
        'moe_w1': nrm(ks[15], (DEPTH, N_EXPERTS, D_MODEL, 2 * D_EXPERT), D_MODEL ** -0.5),
        'moe_b1': nrm(ks[16], (DEPTH, N_EXPERTS, 2 * D_EXPERT), 0.01),
        'moe_w2': nrm(ks[17], (DEPTH, N_EXPERTS, D_EXPERT, D_MODEL), D_EXPERT ** -0.5),
        'moe_b2': nrm(ks[18], (DEPTH, N_EXPERTS, D_MODEL), 0.01),
        'final_g': 1.0 + nrm(ks[19], (D_MODEL,), 0.02),
        'final_ada_w': nrm(ks[20], (D_MODEL, 2 * D_MODEL), 0.5 * D_MODEL ** -0.5),
        'final_ada_b': nrm(ks[21], (2 * D_MODEL,), 0.02),
    }


def reference(x, c, positions, ada_w, ada_b, norm1_g, norm2_g, hgrn_w_in, hgrn_w_out, hgrn_o_gain, hgrn_lb_logits, ret_w_in, ret_w_out, router_w, router_b, moe_w1, moe_b1, moe_w2, moe_b2, final_g, final_ada_w, final_ada_b):
    B, S, D = x.shape
    cond = jax.nn.silu(c)
    lbs = hgrn_lower_bounds(hgrn_lb_logits)
    for layer in range(DEPTH):
        mod = (cond @ ada_w[layer] + ada_b[layer])[:, None, :]
        sh1, sc1, g1, sh2, sc2, g2 = jnp.split(mod, 6, axis=-1)
        h = rms_norm(x, norm1_g[layer]) * (1.0 + sc1) + sh1
        j = layer // N_MIXERS
        if layer % N_MIXERS == 0:
            y = hgrn2_mixer(h, hgrn_w_in[j], hgrn_w_out[j], hgrn_o_gain[j], lbs[j])
        else:
            y = retention_mixer(h, positions, ret_w_in[j], ret_w_out[j])
        x = x + g1 * y
        h = rms_norm(x, norm2_g[layer]) * (1.0 + sc2) + sh2
        y = moe_ffn(h.reshape(B * S, D), router_w[layer], router_b[layer], moe_w1[layer], moe_b1[layer], moe_w2[layer], moe_b2[layer])
        x = x + g2 * y.reshape(B, S, D)
    fmod = (cond @ final_ada_w + final_ada_b)[:, None, :]
    shf, scf = jnp.split(fmod, 2, axis=-1)
    return rms_norm(x, final_g) * (1.0 + scf) + shf
```

```python
import functools

import jax
import jax.numpy as jnp
from jax import lax
from jax.experimental import pallas as pl
from jax.experimental.pallas import tpu as pltpu

F32 = jnp.float32
BF16 = jnp.bfloat16
I32 = jnp.int32
HIGHEST = lax.Precision.HIGHEST

EPS = 1e-6
N_MIXERS = 2
HG_DK = 128
HG_T = 64
HG_SUB = 16
HG_MAX_HALF_RANGE = 80.0
RET_DK = 256
RET_DV = 512
RET_CHUNK = 64
RET_T = 256
ROPE_BASE = 10000.0
N_EXPERTS = 32
TOP_K = 4
SWIGLU_ALPHA = 1.702
SWIGLU_LIMIT = 7.0
MOE_ROWS = 512
ROW_TILE = 512
V7X_VMEM_LIMIT = 56 * 1024 * 1024


def _dot(a, b):
    return jnp.dot(a, b, preferred_element_type=F32)


def _dot_nt(a, b, precision=None):
    return lax.dot_general(a, b, (((1,), (1,)), ((), ())), precision=precision,
                           preferred_element_type=F32)


def _dot_tn(a, b):
    return lax.dot_general(a, b, (((0,), (0,)), ((), ())), preferred_element_type=F32)


def _rms(x):
    return x * lax.rsqrt(jnp.mean(x * x, axis=-1, keepdims=True) + EPS)


def _sigmoid(x):
    return 1.0 / (1.0 + jnp.exp(-x))


def _ada_kernel(c_ref, w_ref, b_ref, o_ref):
    c = c_ref[...]
    cond = c * _sigmoid(c)
    o_ref[0] = jnp.dot(cond, w_ref[0], precision=HIGHEST, preferred_element_type=F32) + b_ref[0]


def _ada(c, w, b):
    nl, d, kd = w.shape
    bsz = c.shape[0]
    return pl.pallas_call(
        _ada_kernel,
        grid=(nl, kd // d),
        in_specs=[
            pl.BlockSpec((bsz, d), lambda l, j: (0, 0)),
            pl.BlockSpec((1, d, d), lambda l, j: (l, 0, j)),
            pl.BlockSpec((1, 1, d), lambda l, j: (l, 0, j)),
        ],
        out_specs=pl.BlockSpec((1, bsz, d), lambda l, j: (l, 0, j)),
        out_shape=jax.ShapeDtypeStruct((nl, bsz, kd), F32),
        name="ada_mod",
    )(c, w, b.reshape(nl, 1, kd))


def _inproj_kernel(x_ref, g_ref, sh_ref, sc_ref, w_ref, o_ref, *, col_chunk):
    h = _rms(x_ref[0]) * g_ref[...] * (1.0 + sc_ref[0]) + sh_ref[0]
    hb = h.astype(BF16)
    nout = w_ref.shape[1]
    for j in range(nout // col_chunk):
        cs = slice(j * col_chunk, (j + 1) * col_chunk)
        o_ref[0, :, cs] = _dot(hb, w_ref[:, cs]).astype(BF16)


def _inproj(x, gain, mod, w):
    bsz, s, d = x.shape
    nout = w.shape[1]
    tm = min(ROW_TILE, s)
    return pl.pallas_call(
        functools.partial(_inproj_kernel, col_chunk=1024),
        grid=(bsz, s // tm),
        in_specs=[
            pl.BlockSpec((1, tm, d), lambda b, i: (b, i, 0)),
            pl.BlockSpec((1, d), lambda b, i: (0, 0)),
            pl.BlockSpec((1, 1, d), lambda b, i: (b, 0, 0)),
            pl.BlockSpec((1, 1, d), lambda b, i: (b, 0, 1)),
            pl.BlockSpec((d, nout), lambda b, i: (0, 0)),
        ],
        out_specs=pl.BlockSpec((1, tm, nout), lambda b, i: (b, i, 0)),
        out_shape=jax.ShapeDtypeStruct((bsz, s, nout), BF16),
        compiler_params=pltpu.CompilerParams(
            dimension_semantics=("parallel", "parallel"), vmem_limit_bytes=V7X_VMEM_LIMIT),
        name="inproj",
    )(x, gain, mod, mod, w)


def _hgrn_kernel(q_ref, f_ref, i_ref, g_ref, lb_ref, gain_ref, o_ref,
                 st_ref, qs_s, kk_s, v_s, b_s, o_s):
    t = HG_T
    n_chunks = q_ref.shape[1] // t
    lb = lb_ref[...]
    one_m_lb = 1.0 - lb
    gain = gain_ref[...]
    st_ref[...] = jnp.zeros_like(st_ref)
    row = lax.broadcasted_iota(I32, (t, t), 0)
    col = lax.broadcasted_iota(I32, (t, t), 1)
    causal = row >= col
    tril = causal.astype(F32)
    sub_row = lax.broadcasted_iota(I32, (HG_SUB, 1), 0)

    def chunk(c, carry):
        r0 = pl.multiple_of(c * t, t)
        rows = pl.ds(r0, t)
        q = q_ref[0, rows, :].astype(F32)
        f = f_ref[0, rows, :].astype(F32)
        v = i_ref[0, rows, :]
        g = g_ref[0, rows, :].astype(F32)

        e = jnp.exp(-jnp.abs(f))
        inv = 1.0 / (1.0 + e)
        pos = f >= 0.0
        sig = jnp.where(pos, inv, e * inv)
        sig_neg = jnp.where(pos, e * inv, inv)
        log_sig = jnp.minimum(f, 0.0) - jnp.log(1.0 + e)
        logf = jnp.where(lb > 0.0, jnp.log(lb + one_m_lb * sig), log_sig)
        kk = one_m_lb * sig_neg
        qs = q * _sigmoid(q)
        b = jnp.dot(tril, logf, precision=HIGHEST, preferred_element_type=F32)
        b_last = b[t - 1:t, :]
        half_range = 0.5 * jnp.max(jnp.abs(b_last))
        safe = half_range <= HG_MAX_HALF_RANGE

        @pl.when(safe)
        def _():
            st = st_ref[...]
            mid = 0.5 * b_last
            qt = (qs * jnp.exp(b - mid)).astype(BF16)
            kt = (kk * jnp.exp(mid - b)).astype(BF16)
            att = jnp.where(causal, _dot_nt(qt, kt), 0.0)
            o = _dot(att.astype(BF16), v)
            o = o + _dot_nt((qs * jnp.exp(b)).astype(BF16), st.astype(BF16))
            kd = (kk * jnp.exp(b_last - b)).astype(BF16)
            st_ref[...] = st * jnp.exp(b_last) + _dot_tn(v, kd)
            o_s[...] = o

        @pl.when(jnp.logical_not(safe))
        def _():
            qs_s[...] = qs
            kk_s[...] = kk
            v_s[...] = v.astype(F32)
            b_s[...] = b
            st = st_ref[...]
            for i in range(t // HG_SUB):
                lo = i * HG_SUB
                base = b_s[lo - 1:lo, :] if i > 0 else jnp.zeros_like(lb)
                br = b_s[lo:lo + HG_SUB, :] - base
                qsb = qs_s[lo:lo + HG_SUB, :]
                o_blk = _dot_nt((qsb * jnp.exp(br)).astype(BF16), st.astype(BF16))

                def pair(s, acc, lo=lo, base=base, br=br, qsb=qsb):
                    brs = b_s[pl.ds(lo + s, 1), :] - base
                    ks = kk_s[pl.ds(lo + s, 1), :]
                    vs = v_s[pl.ds(lo + s, 1), :]
                    dec = jnp.exp(jnp.minimum(br - brs, 0.0))
                    w = jnp.sum(qsb * ks * dec, axis=-1, keepdims=True)
                    w = jnp.where(sub_row >= s, w, 0.0)
                    return acc + w * vs

                o_blk = lax.fori_loop(0, HG_SUB, pair, o_blk)
                o_s[lo:lo + HG_SUB, :] = o_blk
                br_last = br[HG_SUB - 1:HG_SUB, :]
                kdb = (kk_s[lo:lo + HG_SUB, :] * jnp.exp(br_last - br)).astype(BF16)
                st = st * jnp.exp(br_last) + _dot_tn(v_s[lo:lo + HG_SUB, :].astype(BF16), kdb)
            st_ref[...] = st

        o = o_s[...]
        out = _rms(o) * gain * (g * _sigmoid(g))
        o_ref[0, rows, :] = out.astype(BF16)
        return carry

    lax.fori_loop(0, n_chunks, chunk, 0)


def _hgrn(proj, lb, gain):
    bsz, s, w4 = proj.shape
    dk = HG_DK
    nh = w4 // (4 * dk)

    def spec(j):
        return pl.BlockSpec((1, s, dk), lambda b, h: (b, 0, h + j * nh))

    return pl.pallas_call(
        _hgrn_kernel,
        grid=(bsz, nh),
        in_specs=[spec(0), spec(1), spec(2), spec(3),
                  pl.BlockSpec((1, dk), lambda b, h: (0, h)),
                  pl.BlockSpec((1, dk), lambda b, h: (0, 0))],
        out_specs=pl.BlockSpec((1, s, dk), lambda b, h: (b, 0, h)),
        out_shape=jax.ShapeDtypeStruct((bsz, s, nh * dk), BF16),
        scratch_shapes=[pltpu.VMEM((dk, dk), F32)] + [pltpu.VMEM((HG_T, dk), F32)] * 5,
        compiler_params=pltpu.CompilerParams(dimension_semantics=("parallel", "parallel")),
        name="hgrn",
    )(proj, proj, proj, proj, lb, gain)


def _rope_kernel(pos_ref, inv_ref, cos_ref, sin_ref):
    ang = pos_ref[0].astype(F32) * inv_ref[...]
    cos_ref[0] = jnp.cos(ang)
    sin_ref[0] = jnp.sin(ang)


def _rope_tables(positions):
    bsz, s = positions.shape
    half = RET_DK // 2
    inv_freq = (1.0 / (ROPE_BASE ** jnp.linspace(0.0, 1.0, half, dtype=F32))).reshape(1, half)
    out = jax.ShapeDtypeStruct((bsz, s, half), F32)
    return pl.pallas_call(
        _rope_kernel,
        grid=(bsz,),
        in_specs=[pl.BlockSpec((1, s, 1), lambda b: (b, 0, 0)),
                  pl.BlockSpec((1, half), lambda b: (0, 0))],
        out_specs=[pl.BlockSpec((1, s, half), lambda b: (b, 0, 0))] * 2,
        out_shape=[out, out],
        name="rope_tables",
    )(positions.reshape(bsz, s, 1), inv_freq)


def _ret_kernel(q_ref, k_ref, v_ref, g_ref, cos_ref, sin_ref, lg_ref, o_ref, r_ref, d_ref):
    t = RET_T
    dk = RET_DK
    half = dk // 2
    n_steps = q_ref.shape[1] // t
    lg = lg_ref[0]
    lg_k = lg[:, :dk]
    n = lax.broadcasted_iota(I32, (t, t), 0)
    m = lax.broadcasted_iota(I32, (t, t), 1)
    dist = jnp.abs(n - m).astype(F32)
    visible = (m // RET_CHUNK) <= (n // RET_CHUNK)
    d_ref[...] = jnp.where(visible, jnp.exp(dist * lg[:, :t]), 0.0)
    idx = lax.broadcasted_iota(I32, (t, dk), 0).astype(F32)
    q_decay = jnp.exp((idx + 1.0) * lg_k)
    k_decay = jnp.exp((t - 1.0 - idx) * lg_k)
    step_decay = jnp.exp(float(t) * lg)
    r_ref[...] = jnp.zeros_like(r_ref)

    def rotate(x, cos, sin):
        x1, x2 = x[:, :half], x[:, half:]
        return jnp.concatenate([x1 * cos - x2 * sin, x2 * cos + x1 * sin], axis=-1)

    def step(c, carry):
        r0 = pl.multiple_of(c * t, t)
        rows = pl.ds(r0, t)
        cos = cos_ref[0, rows, :]
        sin = sin_ref[0, rows, :]
        q = rotate(q_ref[0, rows, :].astype(F32), cos, sin)
        k = rotate(k_ref[0, rows, :].astype(F32), cos, sin) * (dk ** -0.5)
        v = v_ref[0, rows, :]
        g = g_ref[0, rows, :].astype(F32)
        r = r_ref[...]
        sc = _dot_nt(q.astype(BF16), k.astype(BF16)) * d_ref[...]
        o = _dot(sc.astype(BF16), v) + _dot((q * q_decay).astype(BF16), r.astype(BF16))
        r_ref[...] = r * step_decay + _dot_tn((k * k_decay).astype(BF16), v)
        out = _rms(o) * (g * _sigmoid(g))
        o_ref[0, rows, :] = out.astype(BF16)
        return carry

    lax.fori_loop(0, n_steps, step, 0)


def _retention(proj, cos, sin):
    bsz, s, w = proj.shape
    dk, dv = RET_DK, RET_DV
    nh = w // (2 * dk + 2 * dv)
    hidx = jnp.arange(nh, dtype=F32)
    log_gamma = jnp.log(1.0 - 2.0 ** (-5.0 - hidx))
    lg = jnp.broadcast_to(log_gamma[:, None, None], (nh, 1, dv))
    vbase = 2 * nh * dk // dv
    return pl.pallas_call(
        _ret_kernel,
        grid=(bsz, nh),
        in_specs=[
            pl.BlockSpec((1, s, dk), lambda b, h: (b, 0, h)),
            pl.BlockSpec((1, s, dk), lambda b, h: (b, 0, nh + h)),
            pl.BlockSpec((1, s, dv), lambda b, h: (b, 0, vbase + h)),
            pl.BlockSpec((1, s, dv), lambda b, h: (b, 0, vbase + nh + h)),
            pl.BlockSpec((1, s, dk // 2), lambda b, h: (b, 0, 0)),
            pl.BlockSpec((1, s, dk // 2), lambda b, h: (b, 0, 0)),
            pl.BlockSpec((1, 1, dv), lambda b, h: (h, 0, 0)),
        ],
        out_specs=pl.BlockSpec((1, s, dv), lambda b, h: (b, 0, h)),
        out_shape=jax.ShapeDtypeStruct((bsz, s, nh * dv), BF16),
        scratch_shapes=[pltpu.VMEM((dk, dv), F32), pltpu.VMEM((RET_T, RET_T), F32)],
        compiler_params=pltpu.CompilerParams(
            dimension_semantics=("parallel", "parallel"), vmem_limit_bytes=V7X_VMEM_LIMIT),
        name="retention",
    )(proj, proj, proj, proj, cos, sin, lg)


def _outproj_kernel(o_ref, w_ref, x_ref, g1_ref, sh_ref, sc_ref, gain_ref, rw_ref, rb_ref, u_ref,
                    xo_ref, h_ref, eidx_ref, gate_ref, rank_ref, cnt_ref, base_ref):
    first = jnp.logical_and(pl.program_id(0) == 0, pl.program_id(1) == 0)

    @pl.when(first)
    def _():
        base_ref[...] = jnp.zeros_like(base_ref)

    y = _dot(o_ref[0], w_ref[...])
    xn = x_ref[0] + g1_ref[0] * y
    xo_ref[0] = xn
    h = _rms(xn) * gain_ref[...] * (1.0 + sc_ref[0]) + sh_ref[0]
    h_ref[0] = h.astype(BF16)

    work = _dot_nt(rw_ref[...], h, precision=HIGHEST) + rb_ref[...]
    ne, tm = work.shape
    eiota = lax.broadcasted_iota(I32, (ne, tm), 0)
    onehots, tops = [], []
    for k in range(TOP_K):
        mx = jnp.max(work, axis=0, keepdims=True)
        idx = jnp.min(jnp.where(work == mx, eiota, ne), axis=0, keepdims=True)
        oh = eiota == idx
        work = jnp.where(oh, -jnp.inf, work)
        eidx_ref[k:k + 1, :] = idx
        onehots.append(oh)
        tops.append(mx)
    ex = [jnp.exp(m - tops[0]) for m in tops]
    denom = ex[0] + ex[1] + ex[2] + ex[3]
    for k in range(TOP_K):
        gate_ref[k:k + 1, :] = ex[k] / denom

    mask = jnp.zeros((ne, tm), F32)
    for oh in onehots:
        mask = mask + oh.astype(F32)
    incl = _dot(mask.astype(BF16), u_ref[...])
    excl = incl - mask + base_ref[...]
    for k in range(TOP_K):
        rk = jnp.sum(jnp.where(onehots[k], excl, 0.0), axis=0, keepdims=True)
        rank_ref[k:k + 1, :] = rk.astype(I32)
    total = base_ref[...] + incl[:, tm - 1:tm]
    base_ref[...] = total
    cnt_ref[...] = total.astype(I32)


def _outproj_route(o, w_out, x, mod, gain2, router_wt, router_b):
    bsz, s, d = x.shape
    hv = o.shape[2]
    tm = min(ROW_TILE, s)
    n = bsz * s
    nt = s // tm
    ne = router_wt.shape[0]
    upper = (jnp.arange(tm)[:, None] <= jnp.arange(tm)[None, :]).astype(BF16)

    def modspec(j):
        return pl.BlockSpec((1, 1, d), lambda b, i: (b, 0, j))

    tokspec = pl.BlockSpec((TOP_K, tm), lambda b, i: (0, b * nt + i))
    return pl.pallas_call(
        _outproj_kernel,
        grid=(bsz, nt),
        in_specs=[
            pl.BlockSpec((1, tm, hv), lambda b, i: (b, i, 0)),
            pl.BlockSpec((hv, d), lambda b, i: (0, 0)),
            pl.BlockSpec((1, tm, d), lambda b, i: (b, i, 0)),
            modspec(2), modspec(3), modspec(4),
            pl.BlockSpec((1, d), lambda b, i: (0, 0)),
            pl.BlockSpec((ne, d), lambda b, i: (0, 0)),
            pl.BlockSpec((ne, 1), lambda b, i: (0, 0)),
            pl.BlockSpec((tm, tm), lambda b, i: (0, 0)),
        ],
        out_specs=[
            pl.BlockSpec((1, tm, d), lambda b, i: (b, i, 0)),
            pl.BlockSpec((1, tm, d), lambda b, i: (b, i, 0)),
            tokspec, tokspec, tokspec,
            pl.BlockSpec((ne, 1), lambda b, i: (0, 0)),
        ],
        out_shape=[
            jax.ShapeDtypeStruct((bsz, s, d), F32),
            jax.ShapeDtypeStruct((bsz, s, d), BF16),
            jax.ShapeDtypeStruct((TOP_K, n), I32),
            jax.ShapeDtypeStruct((TOP_K, n), F32),
            jax.ShapeDtypeStruct((TOP_K, n), I32),
            jax.ShapeDtypeStruct((ne, 1), I32),
        ],
        scratch_shapes=[pltpu.VMEM((ne, 1), F32)],
        compiler_params=pltpu.CompilerParams(
            dimension_semantics=("arbitrary", "arbitrary"), vmem_limit_bytes=V7X_VMEM_LIMIT),
        name="outproj_route",
    )(o, w_out, x, mod, mod, mod, gain2, router_wt, router_b, upper)


def _moe_kernel(be_ref, nb_ref, x_ref, rw_ref, w1_ref, b1_ref, w2_ref, b2_ref, y_ref):
    del be_ref
    used = pl.program_id(0) < nb_ref[0]

    @pl.when(used)
    def _():
        f = w2_ref.shape[1]
        u = _dot(x_ref[...], w1_ref[0]) + b1_ref[0]
        glu = jnp.minimum(u[:, :f], SWIGLU_LIMIT)
        lin = jnp.clip(u[:, f:], -SWIGLU_LIMIT, SWIGLU_LIMIT)
        a = glu * _sigmoid(SWIGLU_ALPHA * glu) * (lin + 1.0)
        y = _dot(a.astype(BF16), w2_ref[0]) + b2_ref[0]
        y_ref[...] = (y * rw_ref[...]).astype(BF16)

    @pl.when(jnp.logical_not(used))
    def _():
        y_ref[...] = jnp.zeros_like(y_ref)


def _moe_blocks(block_e, n_used, xs, row_w, w1, b1, w2, b2):
    n_rows, d = xs.shape
    ne, _, f2 = w1.shape
    f = f2 // 2
    nblk = n_rows // MOE_ROWS
    grid_spec = pltpu.PrefetchScalarGridSpec(
        num_scalar_prefetch=2,
        grid=(nblk,),
        in_specs=[
            pl.BlockSpec((MOE_ROWS, d), lambda i, be, nb: (i, 0)),
            pl.BlockSpec((MOE_ROWS, 1), lambda i, be, nb: (i, 0)),
            pl.BlockSpec((1, d, f2), lambda i, be, nb: (be[i], 0, 0)),
            pl.BlockSpec((1, 1, f2), lambda i, be, nb: (be[i], 0, 0)),
            pl.BlockSpec((1, f, d), lambda i, be, nb: (be[i], 0, 0)),
            pl.BlockSpec((1, 1, d), lambda i, be, nb: (be[i], 0, 0)),
        ],
        out_specs=pl.BlockSpec((MOE_ROWS, d), lambda i, be, nb: (i, 0)),
    )
    return pl.pallas_call(
        _moe_kernel,
        grid_spec=grid_spec,
        out_shape=jax.ShapeDtypeStruct((n_rows, d), BF16),
        compiler_params=pltpu.CompilerParams(
            dimension_semantics=("arbitrary",), vmem_limit_bytes=V7X_VMEM_LIMIT),
        name="moe_experts",
    )(block_e, n_used, xs, row_w, w1, b1.reshape(ne, 1, f2), w2, b2.reshape(ne, 1, d))


def _combine_kernel(x_ref, y_ref, g_ref, o_ref):
    o_ref[0] = x_ref[0] + g_ref[0] * y_ref[0].astype(F32)


def _combine(x, y, mod):
    bsz, s, d = x.shape
    tm = min(ROW_TILE, s)
    blk = pl.BlockSpec((1, tm, d), lambda b, i: (b, i, 0))
    return pl.pallas_call(
        _combine_kernel,
        grid=(bsz, s // tm),
        in_specs=[blk, blk, pl.BlockSpec((1, 1, d), lambda b, i: (b, 0, 5))],
        out_specs=blk,
        out_shape=jax.ShapeDtypeStruct((bsz, s, d), F32),
        compiler_params=pltpu.CompilerParams(dimension_semantics=("parallel", "parallel")),
        name="combine",
    )(x, y, mod)


def _final_kernel(x_ref, g_ref, sh_ref, sc_ref, o_ref):
    o_ref[0] = _rms(x_ref[0]) * g_ref[...] * (1.0 + sc_ref[0]) + sh_ref[0]


def _final(x, gain, fmod):
    bsz, s, d = x.shape
    tm = min(ROW_TILE, s)
    blk = pl.BlockSpec((1, tm, d), lambda b, i: (b, i, 0))
    return pl.pallas_call(
        _final_kernel,
        grid=(bsz, s // tm),
        in_specs=[blk, pl.BlockSpec((1, d), lambda b, i: (0, 0)),
                  pl.BlockSpec((1, 1, d), lambda b, i: (b, 0, 0)),
                  pl.BlockSpec((1, 1, d), lambda b, i: (b, 0, 1))],
        out_specs=blk,
        out_shape=jax.ShapeDtypeStruct((bsz, s, d), F32),
        compiler_params=pltpu.CompilerParams(dimension_semantics=("parallel", "parallel")),
        name="final_norm",
    )(x, gain, fmod, fmod)


def _moe_layer(h2, eidx, gate, rank, counts, w1, b1, w2, b2):
    bsz, s, d = h2.shape
    n = bsz * s
    ne = w1.shape[0]
    n_assign = n * TOP_K
    nblk = -(-n_assign // MOE_ROWS) + ne
    n_rows = nblk * MOE_ROWS
    counts = counts[:, 0]
    padded = (counts + MOE_ROWS - 1) // MOE_ROWS * MOE_ROWS
    pad_end = jnp.cumsum(padded)
    pad_start = pad_end - padded
    dest = pad_start[eidx] + rank
    tok = jnp.broadcast_to(jnp.arange(n, dtype=I32)[None, :], dest.shape)
    flat = dest.reshape(-1)
    row_tok = jnp.zeros((n_rows,), I32).at[flat].set(tok.reshape(-1))
    row_w = jnp.zeros((n_rows,), F32).at[flat].set(gate.reshape(-1))
    xs = h2.reshape(n, d)[row_tok]
    block_e = jnp.minimum(
        jnp.searchsorted(pad_end, jnp.arange(nblk, dtype=I32) * MOE_ROWS, side="right"),
        ne - 1).astype(I32)
    n_used = (pad_end[-1:] // MOE_ROWS).astype(I32)
    ys = _moe_blocks(block_e, n_used, xs, row_w.reshape(n_rows, 1), w1, b1, w2, b2)
    y = ys[dest[0]].astype(F32)
    for k in range(1, TOP_K):
        y = y + ys[dest[k]].astype(F32)
    return y.reshape(bsz, s, d)


def _hgrn_lower_bounds(lb_logits):
    p = jax.nn.softmax(lb_logits.astype(F32), axis=0)
    cum = jnp.cumsum(p, axis=0)
    return cum - cum[0:1]


def kernel(x, c, positions, ada_w, ada_b, norm1_g, norm2_g, hgrn_w_in, hgrn_w_out, hgrn_o_gain, hgrn_lb_logits, ret_w_in, ret_w_out, router_w, router_b, moe_w1, moe_b1, moe_w2, moe_b2, final_g, final_ada_w, final_ada_b):
    depth = ada_w.shape[0]
    mods = _ada(c, ada_w, ada_b)
    fmod = _ada(c, final_ada_w[None], final_ada_b[None])[0][:, None, :]
    lbs = _hgrn_lower_bounds(hgrn_lb_logits)
    cos, sin = _rope_tables(positions)
    for layer in range(depth):
        mod = mods[layer][:, None, :]
        j = layer // N_MIXERS
        if layer % N_MIXERS == 0:
            proj = _inproj(x, norm1_g[layer][None], mod, hgrn_w_in[j].astype(BF16))
            o = _hgrn(proj, lbs[j][None], hgrn_o_gain[j][None])
            w_out = hgrn_w_out[j]
        else:
            proj = _inproj(x, norm1_g[layer][None], mod, ret_w_in[j].astype(BF16))
            o = _retention(proj, cos, sin)
            w_out = ret_w_out[j]
        x, h2, eidx, gate, rank, counts = _outproj_route(
            o, w_out.astype(BF16), x, mod, norm2_g[layer][None],
            router_w[layer].T, router_b[layer][:, None])
        y = _moe_layer(h2, eidx, gate, rank, counts, moe_w1[layer].astype(BF16), moe_b1[layer],
                       moe_w2[layer].astype(BF16), moe_b2[layer])
        x = _combine(x, y, mod)
    return _final(x, final_g[None], fmod)
```

```python
import functools

import jax
import jax.numpy as jnp
from jax import lax
from jax.experimental import pallas as pl
from jax.experimental.pallas import tpu as pltpu
from jax.experimental.pallas import tpu_sc as plsc

F32 = jnp.float32
BF16 = jnp.bfloat16
I32 = jnp.int32
HIGHEST = lax.Precision.HIGHEST

EPS = 1e-6
N_MIXERS = 2
HG_DK = 128
HG_T = 64
HG_SUB = 16
HG_MAX_HALF_RANGE = 80.0
RET_DK = 256
RET_DV = 512
RET_CHUNK = 64
RET_T = 256
ROPE_BASE = 10000.0
N_EXPERTS = 32
TOP_K = 4
SWIGLU_ALPHA = 1.702
SWIGLU_LIMIT = 7.0
MOE_ROWS = 512
ROW_TILE = 512
SC_WINDOW = 128
ROW_PARTS = 2
V7X_VMEM_LIMIT = 56 * 1024 * 1024


def _dot(a, b):
    return jnp.dot(a, b, preferred_element_type=F32)


def _dot_nt(a, b, precision=None):
    return lax.dot_general(a, b, (((1,), (1,)), ((), ())), precision=precision,
                           preferred_element_type=F32)


def _dot_tn(a, b):
    return lax.dot_general(a, b, (((0,), (0,)), ((), ())), preferred_element_type=F32)


def _rms(x):
    return x * lax.rsqrt(jnp.mean(x * x, axis=-1, keepdims=True) + EPS)


def _sigmoid(x):
    return 1.0 / (1.0 + jnp.exp(-x))


def _pack_rows(h):
    half = h.shape[1] // 2
    a = lax.bitcast_convert_type(h[:, :half].astype(BF16).astype(F32), jnp.uint32)
    b = lax.bitcast_convert_type(h[:, half:].astype(BF16).astype(F32), jnp.uint32)
    return lax.bitcast_convert_type(a | (b >> 16), I32)


def _unpack_rows(w):
    u = lax.bitcast_convert_type(w, jnp.uint32)
    a = lax.bitcast_convert_type(u & jnp.uint32(0xFFFF0000), F32)
    b = lax.bitcast_convert_type(u << 16, F32)
    return jnp.concatenate([a, b], axis=1)


def _ada_kernel(c_ref, w_ref, b_ref, o_ref):
    c = c_ref[...]
    cond = c * _sigmoid(c)
    o_ref[0] = jnp.dot(cond, w_ref[0], precision=HIGHEST, preferred_element_type=F32) + b_ref[0]


def _ada(c, w, b):
    nl, d, kd = w.shape
    bsz = c.shape[0]
    return pl.pallas_call(
        _ada_kernel,
        grid=(nl, kd // d),
        in_specs=[
            pl.BlockSpec((bsz, d), lambda l, j: (0, 0)),
            pl.BlockSpec((1, d, d), lambda l, j: (l, 0, j)),
            pl.BlockSpec((1, 1, d), lambda l, j: (l, 0, j)),
        ],
        out_specs=pl.BlockSpec((1, bsz, d), lambda l, j: (l, 0, j)),
        out_shape=jax.ShapeDtypeStruct((nl, bsz, kd), F32),
        name="ada_mod",
    )(c, w, b.reshape(nl, 1, kd))


def _inproj_kernel(x_ref, g_ref, sh_ref, sc_ref, w_ref, o_ref, *, col_chunk):
    h = _rms(x_ref[0]) * g_ref[...] * (1.0 + sc_ref[0]) + sh_ref[0]
    hb = h.astype(BF16)
    nout = w_ref.shape[1]
    for j in range(nout // col_chunk):
        cs = slice(j * col_chunk, (j + 1) * col_chunk)
        o_ref[0, :, cs] = _dot(hb, w_ref[:, cs]).astype(BF16)


def _inproj(x, gain, mod, w):
    bsz, s, d = x.shape
    nout = w.shape[1]
    tm = min(ROW_TILE, s)
    return pl.pallas_call(
        functools.partial(_inproj_kernel, col_chunk=1024),
        grid=(bsz, s // tm),
        in_specs=[
            pl.BlockSpec((1, tm, d), lambda b, i: (b, i, 0)),
            pl.BlockSpec((1, d), lambda b, i: (0, 0)),
            pl.BlockSpec((1, 1, d), lambda b, i: (b, 0, 0)),
            pl.BlockSpec((1, 1, d), lambda b, i: (b, 0, 1)),
            pl.BlockSpec((d, nout), lambda b, i: (0, 0)),
        ],
        out_specs=pl.BlockSpec((1, tm, nout), lambda b, i: (b, i, 0)),
        out_shape=jax.ShapeDtypeStruct((bsz, s, nout), BF16),
        compiler_params=pltpu.CompilerParams(
            dimension_semantics=("parallel", "parallel"), vmem_limit_bytes=V7X_VMEM_LIMIT),
        name="inproj",
    )(x, gain, mod, mod, w)


def _hgrn_gates(q, f, lb, one_m_lb):
    e = jnp.exp(-jnp.abs(f))
    inv = 1.0 / (1.0 + e)
    pos = f >= 0.0
    t = e * inv
    sig = jnp.where(pos, inv, t)
    sig_neg = jnp.where(pos, t, inv)
    has_lb = lb > 0.0
    logf = jnp.log(jnp.where(has_lb, lb + one_m_lb * sig, inv)) + jnp.where(has_lb, 0.0, jnp.minimum(f, 0.0))
    return q * _sigmoid(q), one_m_lb * sig_neg, logf


def _hgrn_kernel(q_ref, f_ref, i_ref, g_ref, lb_ref, gain_ref, o_ref, kk_s, v_s, b_s):
    t = HG_T
    dk = q_ref.shape[2]
    n_chunks = q_ref.shape[1] // t
    lb = lb_ref[...]
    one_m_lb = 1.0 - lb
    gain = gain_ref[...]
    row = lax.broadcasted_iota(I32, (t, t), 0)
    col = lax.broadcasted_iota(I32, (t, t), 1)
    causal = row >= col
    tril = causal.astype(BF16)

    def finish(o, g):
        return (_rms(o) * gain * (g * _sigmoid(g))).astype(BF16)

    def chunk(c, carry):
        st, bmax = carry
        rows = pl.ds(pl.multiple_of(c * t, t), t)
        q = q_ref[0, rows, :].astype(F32)
        f = f_ref[0, rows, :].astype(F32)
        v = i_ref[0, rows, :]
        g = g_ref[0, rows, :].astype(F32)
        qs, kk, logf = _hgrn_gates(q, f, lb, one_m_lb)
        hi = logf.astype(BF16)
        lo = (logf - hi.astype(F32)).astype(BF16)
        bb = _dot(tril, jnp.concatenate([hi, lo], axis=-1))
        b = bb[:, :dk] + bb[:, dk:]
        b_last = b[t - 1:t, :]
        mid = 0.5 * b_last
        e_mid = jnp.exp(mid)
        qt = qs * jnp.exp(b - mid)
        kt = kk * jnp.exp(mid - b)
        att = jnp.where(causal, _dot_nt(qt.astype(BF16), kt.astype(BF16)), 0.0)
        o = _dot(att.astype(BF16), v)
        o = o + _dot_nt((qt * e_mid).astype(BF16), st.astype(BF16))
        st = st * (e_mid * e_mid) + _dot_tn(v, (kt * e_mid).astype(BF16))
        o_ref[0, rows, :] = finish(o, g)
        return st, jnp.maximum(bmax, jnp.abs(b_last))

    st0 = jnp.zeros((dk, dk), F32)
    _, bmax = lax.fori_loop(0, n_chunks, chunk, (st0, jnp.zeros_like(lb)), unroll=8)
    safe = 0.5 * jnp.max(bmax) <= HG_MAX_HALF_RANGE

    @pl.when(jnp.logical_not(safe))
    def _():
        n = HG_SUB
        sub_row = lax.broadcasted_iota(I32, (n, 1), 0)
        tril_n = (lax.broadcasted_iota(I32, (n, n), 0) >= lax.broadcasted_iota(I32, (n, n), 1)).astype(F32)

        def block(i, st):
            rows = pl.ds(pl.multiple_of(i * n, n), n)
            q = q_ref[0, rows, :].astype(F32)
            f = f_ref[0, rows, :].astype(F32)
            v = i_ref[0, rows, :]
            g = g_ref[0, rows, :].astype(F32)
            qs, kk, logf = _hgrn_gates(q, f, lb, one_m_lb)
            b = jnp.dot(tril_n, logf, precision=HIGHEST, preferred_element_type=F32)
            kk_s[...] = kk
            v_s[...] = v.astype(F32)
            b_s[...] = b
            o = _dot_nt((qs * jnp.exp(b)).astype(BF16), st.astype(BF16))

            def pair(s, acc):
                dec = jnp.exp(jnp.minimum(b - b_s[pl.ds(s, 1), :], 0.0))
                w = jnp.sum(qs * kk_s[pl.ds(s, 1), :] * dec, axis=-1, keepdims=True)
                return acc + jnp.where(sub_row >= s, w, 0.0) * v_s[pl.ds(s, 1), :]

            o = lax.fori_loop(0, n, pair, o)
            b_last = b[n - 1:n, :]
            st = st * jnp.exp(b_last) + _dot_tn(v, (kk * jnp.exp(b_last - b)).astype(BF16))
            o_ref[0, rows, :] = finish(o, g)
            return st

        lax.fori_loop(0, q_ref.shape[1] // n, block, st0)


def _hgrn(proj, lb, gain):
    bsz, s, w4 = proj.shape
    dk = HG_DK
    nh = w4 // (4 * dk)

    def spec(j):
        return pl.BlockSpec((1, s, dk), lambda b, h: (b, 0, h + j * nh))

    return pl.pallas_call(
        _hgrn_kernel,
        grid=(bsz, nh),
        in_specs=[spec(0), spec(1), spec(2), spec(3),
                  pl.BlockSpec((1, dk), lambda b, h: (0, h)),
                  pl.BlockSpec((1, dk), lambda b, h: (0, 0))],
        out_specs=pl.BlockSpec((1, s, dk), lambda b, h: (b, 0, h)),
        out_shape=jax.ShapeDtypeStruct((bsz, s, nh * dk), BF16),
        scratch_shapes=[pltpu.VMEM((HG_SUB, dk), F32)] * 3,
        compiler_params=pltpu.CompilerParams(dimension_semantics=("parallel", "parallel")),
        name="hgrn",
    )(proj, proj, proj, proj, lb, gain)


def _rope_kernel(pos_ref, inv_ref, cos_ref, sin_ref):
    ang = pos_ref[0].astype(F32) * inv_ref[...]
    cos_ref[0] = jnp.cos(ang)
    sin_ref[0] = jnp.sin(ang)


def _rope_tables(positions):
    bsz, s = positions.shape
    half = RET_DK // 2
    inv_freq = (1.0 / (ROPE_BASE ** jnp.linspace(0.0, 1.0, half, dtype=F32))).reshape(1, half)
    out = jax.ShapeDtypeStruct((bsz, s, half), F32)
    return pl.pallas_call(
        _rope_kernel,
        grid=(bsz,),
        in_specs=[pl.BlockSpec((1, s, 1), lambda b: (b, 0, 0)),
                  pl.BlockSpec((1, half), lambda b: (0, 0))],
        out_specs=[pl.BlockSpec((1, s, half), lambda b: (b, 0, 0))] * 2,
        out_shape=[out, out],
        name="rope_tables",
    )(positions.reshape(bsz, s, 1), inv_freq)


def _ret_kernel(q_ref, k_ref, v_ref, g_ref, cos_ref, sin_ref, lg_ref, o_ref, r_ref, d_ref):
    t = RET_T
    dk = RET_DK
    half = dk // 2
    n_steps = q_ref.shape[1] // t
    lg = lg_ref[0]
    lg_k = lg[:, :dk]
    n = lax.broadcasted_iota(I32, (t, t), 0)
    m = lax.broadcasted_iota(I32, (t, t), 1)
    dist = jnp.abs(n - m).astype(F32)
    visible = (m // RET_CHUNK) <= (n // RET_CHUNK)
    d_ref[...] = jnp.where(visible, jnp.exp(dist * lg[:, :t]), 0.0)
    idx = lax.broadcasted_iota(I32, (t, dk), 0).astype(F32)
    q_decay = jnp.exp((idx + 1.0) * lg_k)
    k_decay = jnp.exp((t - 1.0 - idx) * lg_k)
    step_decay = jnp.exp(float(t) * lg)
    r_ref[...] = jnp.zeros_like(r_ref)

    def rotate(x, cos, sin):
        x1, x2 = x[:, :half], x[:, half:]
        return jnp.concatenate([x1 * cos - x2 * sin, x2 * cos + x1 * sin], axis=-1)

    def step(c, carry):
        r0 = pl.multiple_of(c * t, t)
        rows = pl.ds(r0, t)
        cos = cos_ref[0, rows, :]
        sin = sin_ref[0, rows, :]
        q = rotate(q_ref[0, rows, :].astype(F32), cos, sin)
        k = rotate(k_ref[0, rows, :].astype(F32), cos, sin) * (dk ** -0.5)
        v = v_ref[0, rows, :]
        g = g_ref[0, rows, :].astype(F32)
        r = r_ref[...]
        sc = _dot_nt(q.astype(BF16), k.astype(BF16)) * d_ref[...]
        o = _dot(sc.astype(BF16), v) + _dot((q * q_decay).astype(BF16), r.astype(BF16))
        r_ref[...] = r * step_decay + _dot_tn((k * k_decay).astype(BF16), v)
        out = _rms(o) * (g * _sigmoid(g))
        o_ref[0, rows, :] = out.astype(BF16)
        return carry

    lax.fori_loop(0, n_steps, step, 0)


def _retention(proj, cos, sin):
    bsz, s, w = proj.shape
    dk, dv = RET_DK, RET_DV
    nh = w // (2 * dk + 2 * dv)
    hidx = jnp.arange(nh, dtype=F32)
    log_gamma = jnp.log(1.0 - 2.0 ** (-5.0 - hidx))
    lg = jnp.broadcast_to(log_gamma[:, None, None], (nh, 1, dv))
    vbase = 2 * nh * dk // dv
    return pl.pallas_call(
        _ret_kernel,
        grid=(bsz, nh),
        in_specs=[
            pl.BlockSpec((1, s, dk), lambda b, h: (b, 0, h)),
            pl.BlockSpec((1, s, dk), lambda b, h: (b, 0, nh + h)),
            pl.BlockSpec((1, s, dv), lambda b, h: (b, 0, vbase + h)),
            pl.BlockSpec((1, s, dv), lambda b, h: (b, 0, vbase + nh + h)),
            pl.BlockSpec((1, s, dk // 2), lambda b, h: (b, 0, 0)),
            pl.BlockSpec((1, s, dk // 2), lambda b, h: (b, 0, 0)),
            pl.BlockSpec((1, 1, dv), lambda b, h: (h, 0, 0)),
        ],
        out_specs=pl.BlockSpec((1, s, dv), lambda b, h: (b, 0, h)),
        out_shape=jax.ShapeDtypeStruct((bsz, s, nh * dv), BF16),
        scratch_shapes=[pltpu.VMEM((dk, dv), F32), pltpu.VMEM((RET_T, RET_T), F32)],
        compiler_params=pltpu.CompilerParams(
            dimension_semantics=("parallel", "parallel"), vmem_limit_bytes=V7X_VMEM_LIMIT),
        name="retention",
    )(proj, proj, proj, proj, cos, sin, lg)


def _outproj_kernel(o_ref, w_ref, x_ref, g1_ref, sh_ref, sc_ref, gain_ref, rw_ref, rb_ref, u_ref,
                    xo_ref, ha_ref, hb_ref, eidx_ref, gate_ref, rank_ref, cnt_ref, base_ref):
    first = jnp.logical_and(pl.program_id(0) == 0, pl.program_id(1) == 0)

    @pl.when(first)
    def _():
        base_ref[...] = jnp.zeros_like(base_ref)

    y = _dot(o_ref[0], w_ref[...])
    xn = x_ref[0] + g1_ref[0] * y
    xo_ref[0] = xn
    h = _rms(xn) * gain_ref[...] * (1.0 + sc_ref[0]) + sh_ref[0]
    packed = _pack_rows(h)
    slab = packed.shape[1] // ROW_PARTS
    ha_ref[0] = packed[:, :slab]
    hb_ref[0] = packed[:, slab:]

    work = _dot_nt(rw_ref[...], h, precision=HIGHEST) + rb_ref[...]
    ne, tm = work.shape
    eiota = lax.broadcasted_iota(I32, (ne, tm), 0)
    onehots, tops = [], []
    for k in range(TOP_K):
        mx = jnp.max(work, axis=0, keepdims=True)
        idx = jnp.min(jnp.where(work == mx, eiota, ne), axis=0, keepdims=True)
        oh = eiota == idx
        work = jnp.where(oh, -jnp.inf, work)
        eidx_ref[k:k + 1, :] = idx
        onehots.append(oh)
        tops.append(mx)
    ex = [jnp.exp(m - tops[0]) for m in tops]
    denom = ex[0] + ex[1] + ex[2] + ex[3]
    for k in range(TOP_K):
        gate_ref[k:k + 1, :] = ex[k] / denom

    mask = jnp.zeros((ne, tm), F32)
    for oh in onehots:
        mask = mask + oh.astype(F32)
    incl = _dot(mask.astype(BF16), u_ref[...])
    excl = incl - mask + base_ref[...]
    for k in range(TOP_K):
        rk = jnp.sum(jnp.where(onehots[k], excl, 0.0), axis=0, keepdims=True)
        rank_ref[k:k + 1, :] = rk.astype(I32)
    total = base_ref[...] + incl[:, tm - 1:tm]
    base_ref[...] = total
    cnt_ref[...] = total.astype(I32)


def _outproj_route(o, w_out, x, mod, gain2, router_wt, router_b):
    bsz, s, d = x.shape
    hv = o.shape[2]
    tm = min(ROW_TILE, s)
    n = bsz * s
    nt = s // tm
    ne = router_wt.shape[0]
    upper = (jnp.arange(tm)[:, None] <= jnp.arange(tm)[None, :]).astype(BF16)

    def modspec(j):
        return pl.BlockSpec((1, 1, d), lambda b, i: (b, 0, j))

    tokspec = pl.BlockSpec((TOP_K, tm), lambda b, i: (0, b * nt + i))
    return pl.pallas_call(
        _outproj_kernel,
        grid=(bsz, nt),
        in_specs=[
            pl.BlockSpec((1, tm, hv), lambda b, i: (b, i, 0)),
            pl.BlockSpec((hv, d), lambda b, i: (0, 0)),
            pl.BlockSpec((1, tm, d), lambda b, i: (b, i, 0)),
            modspec(2), modspec(3), modspec(4),
            pl.BlockSpec((1, d), lambda b, i: (0, 0)),
            pl.BlockSpec((ne, d), lambda b, i: (0, 0)),
            pl.BlockSpec((ne, 1), lambda b, i: (0, 0)),
            pl.BlockSpec((tm, tm), lambda b, i: (0, 0)),
        ],
        out_specs=[
            pl.BlockSpec((1, tm, d), lambda b, i: (b, i, 0)),
            pl.BlockSpec((1, tm, d // 4), lambda b, i: (b, i, 0)),
            pl.BlockSpec((1, tm, d // 4), lambda b, i: (b, i, 0)),
            tokspec, tokspec, tokspec,
            pl.BlockSpec((ne, 1), lambda b, i: (0, 0)),
        ],
        out_shape=[
            jax.ShapeDtypeStruct((bsz, s, d), F32),
            jax.ShapeDtypeStruct((bsz, s, d // 4), I32),
            jax.ShapeDtypeStruct((bsz, s, d // 4), I32),
            jax.ShapeDtypeStruct((TOP_K, n), I32),
            jax.ShapeDtypeStruct((TOP_K, n), F32),
            jax.ShapeDtypeStruct((TOP_K, n), I32),
            jax.ShapeDtypeStruct((ne, 1), I32),
        ],
        scratch_shapes=[pltpu.VMEM((ne, 1), F32)],
        compiler_params=pltpu.CompilerParams(
            dimension_semantics=("arbitrary", "arbitrary"), vmem_limit_bytes=V7X_VMEM_LIMIT),
        name="outproj_route",
    )(o, w_out, x, mod, mod, mod, gain2, router_wt, router_b, upper)


def _moe_kernel(be_ref, nb_ref, xa_ref, xb_ref, w1_ref, b1_ref, w2_ref, b2_ref, ya_ref, yb_ref):
    del be_ref
    used = pl.program_id(0) < nb_ref[0]

    @pl.when(used)
    def _():
        f = w2_ref.shape[1]
        x = _unpack_rows(jnp.concatenate([xa_ref[...], xb_ref[...]], axis=1)).astype(BF16)
        u = _dot(x, w1_ref[0]) + b1_ref[0]
        glu = jnp.minimum(u[:, :f], SWIGLU_LIMIT)
        lin = jnp.clip(u[:, f:], -SWIGLU_LIMIT, SWIGLU_LIMIT)
        a = glu * _sigmoid(SWIGLU_ALPHA * glu) * (lin + 1.0)
        y = _dot(a.astype(BF16), w2_ref[0]) + b2_ref[0]
        packed = _pack_rows(y)
        slab = packed.shape[1] // ROW_PARTS
        ya_ref[...] = packed[:, :slab]
        yb_ref[...] = packed[:, slab:]

    @pl.when(jnp.logical_not(used))
    def _():
        ya_ref[...] = jnp.zeros_like(ya_ref)
        yb_ref[...] = jnp.zeros_like(yb_ref)


def _moe_blocks(block_e, n_used, xs, w1, b1, w2, b2):
    n_rows, dh = xs[0].shape
    ne, d, f2 = w1.shape
    f = f2 // 2
    nblk = n_rows // MOE_ROWS
    grid_spec = pltpu.PrefetchScalarGridSpec(
        num_scalar_prefetch=2,
        grid=(nblk,),
        in_specs=[
            pl.BlockSpec((MOE_ROWS, dh), lambda i, be, nb: (i, 0)),
            pl.BlockSpec((MOE_ROWS, dh), lambda i, be, nb: (i, 0)),
            pl.BlockSpec((1, d, f2), lambda i, be, nb: (be[i], 0, 0)),
            pl.BlockSpec((1, 1, f2), lambda i, be, nb: (be[i], 0, 0)),
            pl.BlockSpec((1, f, d), lambda i, be, nb: (be[i], 0, 0)),
            pl.BlockSpec((1, 1, d), lambda i, be, nb: (be[i], 0, 0)),
        ],
        out_specs=[pl.BlockSpec((MOE_ROWS, dh), lambda i, be, nb: (i, 0))] * 2,
    )
    return pl.pallas_call(
        _moe_kernel,
        grid_spec=grid_spec,
        out_shape=[jax.ShapeDtypeStruct((n_rows, dh), I32)] * 2,
        compiler_params=pltpu.CompilerParams(
            dimension_semantics=("arbitrary",), vmem_limit_bytes=V7X_VMEM_LIMIT),
        name="moe_experts",
    )(block_e, n_used, xs[0], xs[1], w1, b1.reshape(ne, 1, f2), w2, b2.reshape(ne, 1, d))


def _sc_mesh():
    return plsc.VectorSubcoreMesh(core_axis_name="c", subcore_axis_name="s")


def _sc_scatter_rows(srcs, dests, n_rows):
    n, w = srcs[0].shape
    ns, nk = len(srcs), len(dests)
    out = jax.ShapeDtypeStruct((n_rows, w), srcs[0].dtype)

    @functools.partial(pl.kernel, out_type=[out] * ns, mesh=_sc_mesh(), scratch_types=[])
    def scatter_kernel(*refs):
        x_hbm, idx_hbm, o_hbm = refs[:ns], refs[ns:ns + nk], refs[ns + nk:]
        for x, o in zip(x_hbm, o_hbm):
            def body(x_vmem, *idx_vmem, o=o):
                for iv in idx_vmem:
                    pltpu.sync_copy(x_vmem, o.at[iv.at[0]])

            pltpu.emit_pipeline(
                body,
                grid=(n // SC_WINDOW,),
                in_specs=[pl.BlockSpec((SC_WINDOW, w), lambda i: (i, 0))]
                + [pl.BlockSpec((1, SC_WINDOW), lambda i: (0, i))] * nk,
                out_specs=[],
                core_axis_name=("c", "s"),
                dimension_semantics=(pltpu.PARALLEL,),
            )(x, *idx_hbm)

    return scatter_kernel(*srcs, *dests)


def _sc_gather_rows(tables, idx):
    m = idx.shape[1]
    w = tables[0].shape[1]
    nt = len(tables)
    out = jax.ShapeDtypeStruct((m, w), tables[0].dtype)

    @functools.partial(pl.kernel, out_type=[out] * nt, mesh=_sc_mesh(), scratch_types=[])
    def gather_kernel(*refs):
        t_hbm, i_hbm, o_hbm = refs[:nt], refs[nt], refs[nt + 1:]
        for t, o in zip(t_hbm, o_hbm):
            def body(i_vmem, o_vmem, t=t):
                pltpu.sync_copy(t.at[i_vmem.at[0]], o_vmem)

            pltpu.emit_pipeline(
                body,
                grid=(m // SC_WINDOW,),
                in_specs=[pl.BlockSpec((1, SC_WINDOW), lambda i: (0, i))],
                out_specs=[pl.BlockSpec((SC_WINDOW, w), lambda i: (i, 0))],
                core_axis_name=("c", "s"),
                dimension_semantics=(pltpu.PARALLEL,),
            )(i_hbm, o)

    return gather_kernel(*tables, idx)


def _dest_kernel(ps_ref, eidx_ref, rank_ref, o_ref):
    eidx = eidx_ref[...]
    start = jnp.zeros_like(eidx)
    for e in range(N_EXPERTS):
        start = jnp.where(eidx == e, ps_ref[e], start)
    o_ref[...] = start + rank_ref[...]


def _dest_rows(pad_start, eidx, rank):
    k, n = eidx.shape
    tn = min(8192, n)
    blk = pl.BlockSpec((k, tn), lambda i, ps: (0, i))
    return pl.pallas_call(
        _dest_kernel,
        grid_spec=pltpu.PrefetchScalarGridSpec(
            num_scalar_prefetch=1, grid=(n // tn,), in_specs=[blk, blk], out_specs=blk),
        out_shape=jax.ShapeDtypeStruct((k, n), I32),
        name="dest_rows",
    )(pad_start, eidx, rank)


def _combine_kernel(x_ref, *refs):
    ya_refs, yb_refs = refs[:TOP_K], refs[TOP_K:2 * TOP_K]
    gate_ref, g_ref, o_ref = refs[2 * TOP_K:]
    gate = gate_ref[...]
    acc = None
    for k in range(TOP_K):
        packed = jnp.concatenate([ya_refs[k][...], yb_refs[k][...]], axis=1)
        term = gate[:, k:k + 1] * _unpack_rows(packed)
        acc = term if acc is None else acc + term
    o_ref[0] = x_ref[0] + g_ref[0] * acc


def _combine(x, yg, gate_rows, mod):
    bsz, s, d = x.shape
    tm = min(ROW_TILE, s)
    nt = s // tm
    ntok = (bsz * s) // tm
    blk = pl.BlockSpec((1, tm, d), lambda b, i: (b, i, 0))

    def yspec(k):
        return pl.BlockSpec((tm, d // 4), lambda b, i: (k * ntok + b * nt + i, 0))

    return pl.pallas_call(
        _combine_kernel,
        grid=(bsz, nt),
        in_specs=[blk] + [yspec(k) for k in range(TOP_K)] * 2
        + [pl.BlockSpec((tm, TOP_K), lambda b, i: (b * nt + i, 0)),
           pl.BlockSpec((1, 1, d), lambda b, i: (b, 0, 5))],
        out_specs=blk,
        out_shape=jax.ShapeDtypeStruct((bsz, s, d), F32),
        compiler_params=pltpu.CompilerParams(dimension_semantics=("parallel", "parallel")),
        name="combine",
    )(x, *([yg[0]] * TOP_K), *([yg[1]] * TOP_K), gate_rows, mod)


def _final_kernel(x_ref, g_ref, sh_ref, sc_ref, o_ref):
    o_ref[0] = _rms(x_ref[0]) * g_ref[...] * (1.0 + sc_ref[0]) + sh_ref[0]


def _final(x, gain, fmod):
    bsz, s, d = x.shape
    tm = min(ROW_TILE, s)
    blk = pl.BlockSpec((1, tm, d), lambda b, i: (b, i, 0))
    return pl.pallas_call(
        _final_kernel,
        grid=(bsz, s // tm),
        in_specs=[blk, pl.BlockSpec((1, d), lambda b, i: (0, 0)),
                  pl.BlockSpec((1, 1, d), lambda b, i: (b, 0, 0)),
                  pl.BlockSpec((1, 1, d), lambda b, i: (b, 0, 1))],
        out_specs=blk,
        out_shape=jax.ShapeDtypeStruct((bsz, s, d), F32),
        compiler_params=pltpu.CompilerParams(dimension_semantics=("parallel", "parallel")),
        name="final_norm",
    )(x, gain, fmod, fmod)


def _moe_layer(x, h2, eidx, gate, rank, counts, mod, w1, b1, w2, b2):
    bsz, s, dh = h2[0].shape
    n = bsz * s
    ne = w1.shape[0]
    nblk = -(-(n * TOP_K) // MOE_ROWS) + ne
    n_rows = nblk * MOE_ROWS
    counts = counts[:, 0]
    padded = (counts + MOE_ROWS - 1) // MOE_ROWS * MOE_ROWS
    pad_end = jnp.cumsum(padded)
    pad_start = pad_end - padded
    block_start = jnp.arange(nblk, dtype=I32)[:, None] * MOE_ROWS
    block_e = jnp.minimum(jnp.sum(pad_end[None, :] <= block_start, axis=1), ne - 1).astype(I32)
    n_used = (pad_end[-1:] // MOE_ROWS).astype(I32)
    dest = _dest_rows(pad_start.astype(I32), eidx, rank)
    xs = _sc_scatter_rows([h.reshape(n, dh) for h in h2], [dest[k:k + 1] for k in range(TOP_K)], n_rows)
    ys = _moe_blocks(block_e, n_used, xs, w1, b1, w2, b2)
    yg = _sc_gather_rows(ys, dest.reshape(1, TOP_K * n))
    return _combine(x, yg, gate.T, mod)


def _hgrn_lower_bounds(lb_logits):
    p = jax.nn.softmax(lb_logits.astype(F32), axis=0)
    cum = jnp.cumsum(p, axis=0)
    return cum - cum[0:1]


def kernel(x, c, positions, ada_w, ada_b, norm1_g, norm2_g, hgrn_w_in, hgrn_w_out, hgrn_o_gain, hgrn_lb_logits, ret_w_in, ret_w_out, router_w, router_b, moe_w1, moe_b1, moe_w2, moe_b2, final_g, final_ada_w, final_ada_b):
    depth = ada_w.shape[0]
    mods = _ada(c, ada_w, ada_b)
    fmod = _ada(c, final_ada_w[None], final_ada_b[None])[0][:, None, :]
    lbs = _hgrn_lower_bounds(hgrn_lb_logits)
    cos, sin = _rope_tables(positions)
    for layer in range(depth):
        mod = mods[layer][:, None, :]
        j = layer // N_MIXERS
        if layer % N_MIXERS == 0:
            proj = _inproj(x, norm1_g[layer][None], mod, hgrn_w_in[j].astype(BF16))
            o = _hgrn(proj, lbs[j][None], hgrn_o_gain[j][None])
            w_out = hgrn_w_out[j]
        else:
            proj = _inproj(x, norm1_g[layer][None], mod, ret_w_in[j].astype(BF16))
            o = _retention(proj, cos, sin)
            w_out = ret_w_out[j]
        x, ha, hb, eidx, gate, rank, counts = _outproj_route(
            o, w_out.astype(BF16), x, mod, norm2_g[layer][None],
            router_w[layer].T, router_b[layer][:, None])
        x = _moe_layer(x, (ha, hb), eidx, gate, rank, counts, mod, moe_w1[layer].astype(BF16), moe_b1[layer],
                       moe_w2[layer].astype(BF16), moe_b2[layer])
    return _final(x, final_g[None], fmod)
```

```python
import functools

import jax
import jax.numpy as jnp
from jax import lax
from jax.experimental import pallas as pl
from jax.experimental.pallas import tpu as pltpu
from jax.experimental.pallas import tpu_sc as plsc

F32 = jnp.float32
BF16 = jnp.bfloat16
I32 = jnp.int32
HIGHEST = lax.Precision.HIGHEST

EPS = 1e-6
N_MIXERS = 2
HG_DK = 128
HG_T = 64
HG_SUB = 16
HG_MAX_HALF_RANGE = 80.0
RET_DK = 256
RET_DV = 512
RET_CHUNK = 64
RET_T = 256
ROPE_BASE = 10000.0
N_EXPERTS = 32
TOP_K = 4
SWIGLU_ALPHA = 1.702
SWIGLU_LIMIT = 7.0
MOE_ROWS = 512
WEIGHT_CAST_ROWS = 128
ROW_TILE = 512
INPROJ_SUB_ROWS = 256
OUTPROJ_SUB_ROWS = 512
SC_WINDOW = 128
ROW_PARTS = 2
V7X_VMEM_LIMIT = 56 * 1024 * 1024


def _dot(a, b):
    return jnp.dot(a, b, preferred_element_type=F32)


def _dot_nt(a, b, precision=None):
    return lax.dot_general(a, b, (((1,), (1,)), ((), ())), precision=precision,
                           preferred_element_type=F32)


def _dot_tn(a, b):
    return lax.dot_general(a, b, (((0,), (0,)), ((), ())), preferred_element_type=F32)


def _rms(x):
    return x * lax.rsqrt(jnp.mean(x * x, axis=-1, keepdims=True) + EPS)


def _sigmoid(x):
    return 1.0 / (1.0 + jnp.exp(-x))


def _pack_rows(h):
    half = h.shape[1] // 2
    a = lax.bitcast_convert_type(h[:, :half].astype(BF16).astype(F32), jnp.uint32)
    b = lax.bitcast_convert_type(h[:, half:].astype(BF16).astype(F32), jnp.uint32)
    return lax.bitcast_convert_type(a | (b >> 16), I32)


def _unpack_rows(w):
    u = lax.bitcast_convert_type(w, jnp.uint32)
    a = lax.bitcast_convert_type(u & jnp.uint32(0xFFFF0000), F32)
    b = lax.bitcast_convert_type(u << 16, F32)
    return jnp.concatenate([a, b], axis=1)


def _ada_kernel(c_ref, w_ref, b_ref, o_ref):
    c = c_ref[...]
    cond = c * _sigmoid(c)
    o_ref[0] = jnp.dot(cond, w_ref[0], precision=HIGHEST, preferred_element_type=F32) + b_ref[0]


def _ada(c, w, b):
    nl, d, kd = w.shape
    bsz = c.shape[0]
    return pl.pallas_call(
        _ada_kernel,
        grid=(nl, kd // d),
        in_specs=[
            pl.BlockSpec((bsz, d), lambda l, j: (0, 0)),
            pl.BlockSpec((1, d, d), lambda l, j: (l, 0, j)),
            pl.BlockSpec((1, 1, d), lambda l, j: (l, 0, j)),
        ],
        out_specs=pl.BlockSpec((1, bsz, d), lambda l, j: (l, 0, j)),
        out_shape=jax.ShapeDtypeStruct((nl, bsz, kd), F32),
        name="ada_mod",
    )(c, w, b.reshape(nl, 1, kd))


def _moe_residual(x_ref, refs, rows=slice(None)):
    ya_refs, yb_refs = refs[:TOP_K], refs[TOP_K:2 * TOP_K]
    gate_ref, g2_ref = refs[2 * TOP_K:]
    gate = gate_ref[rows, :]
    acc = None
    for k in range(TOP_K):
        packed = jnp.concatenate([ya_refs[k][rows, :], yb_refs[k][rows, :]], axis=1)
        term = gate[:, k:k + 1] * _unpack_rows(packed)
        acc = term if acc is None else acc + term
    return x_ref[0, rows, :] + g2_ref[0] * acc


N_MOE_REFS = 2 * TOP_K + 2


def _moe_residual_operands(pending, bsz, s, d, tm):
    yg, gate_rows, mod = pending
    nt = s // tm
    ntok = (bsz * s) // tm

    def yspec(k):
        return pl.BlockSpec((tm, d // 4), lambda b, i: (k * ntok + b * nt + i, 0))

    specs = [yspec(k) for k in range(TOP_K)] * 2 + [
        pl.BlockSpec((tm, TOP_K), lambda b, i: (b * nt + i, 0)),
        pl.BlockSpec((1, 1, d), lambda b, i: (b, 0, 5))]
    return specs, [yg[0]] * TOP_K + [yg[1]] * TOP_K + [gate_rows, mod]


def _inproj_kernel(x_ref, *refs, col_chunk, fused):
    if fused:
        g_ref, sh_ref, sc_ref, w_ref, o_ref, xo_ref = refs[N_MOE_REFS:]
    else:
        g_ref, sh_ref, sc_ref, w_ref, o_ref = refs
    nout = w_ref.shape[1]
    tm = x_ref.shape[1]
    sub = min(INPROJ_SUB_ROWS, tm)
    for r in range(0, tm, sub):
        rows = slice(r, r + sub)
        if fused:
            x = _moe_residual(x_ref, refs[:N_MOE_REFS], rows)
            xo_ref[0, rows, :] = x
        else:
            x = x_ref[0, rows, :]
        h = _rms(x) * g_ref[...] * (1.0 + sc_ref[0]) + sh_ref[0]
        hb = h.astype(BF16)
        for j in range(nout // col_chunk):
            cs = slice(j * col_chunk, (j + 1) * col_chunk)
            o_ref[0, rows, cs] = _dot(hb, w_ref[:, cs]).astype(BF16)


def _inproj(x, pending, gain, mod, w):
    bsz, s, d = x.shape
    nout = w.shape[1]
    tm = min(ROW_TILE, s)
    xspec = pl.BlockSpec((1, tm, d), lambda b, i: (b, i, 0))
    fused = pending is not None
    moe_specs, moe_args = _moe_residual_operands(pending, bsz, s, d, tm) if fused else ([], [])
    proj_spec = pl.BlockSpec((1, tm, nout), lambda b, i: (b, i, 0))
    proj_shape = jax.ShapeDtypeStruct((bsz, s, nout), BF16)
    out = pl.pallas_call(
        functools.partial(_inproj_kernel, col_chunk=1024, fused=fused),
        grid=(bsz, s // tm),
        in_specs=[xspec] + moe_specs + [
            pl.BlockSpec((1, d), lambda b, i: (0, 0)),
            pl.BlockSpec((1, 1, d), lambda b, i: (b, 0, 0)),
            pl.BlockSpec((1, 1, d), lambda b, i: (b, 0, 1)),
            pl.BlockSpec((d, nout), lambda b, i: (0, 0), pipeline_mode=pl.Buffered(1)),
        ],
        out_specs=[proj_spec, xspec] if fused else proj_spec,
        out_shape=[proj_shape, jax.ShapeDtypeStruct((bsz, s, d), F32)] if fused else proj_shape,
        compiler_params=pltpu.CompilerParams(
            dimension_semantics=("parallel", "parallel"), vmem_limit_bytes=V7X_VMEM_LIMIT),
        name="inproj",
    )(x, *moe_args, gain, mod, mod, w)
    return (out[1], out[0]) if fused else (x, out)


def _hgrn_gates(q, f, lb, one_m_lb):
    e = jnp.exp(-jnp.abs(f))
    inv = 1.0 / (1.0 + e)
    pos = f >= 0.0
    t = e * inv
    sig = jnp.where(pos, inv, t)
    sig_neg = jnp.where(pos, t, inv)
    has_lb = lb > 0.0
    logf = jnp.log(jnp.where(has_lb, lb + one_m_lb * sig, inv)) + jnp.where(has_lb, 0.0, jnp.minimum(f, 0.0))
    return q * _sigmoid(q), one_m_lb * sig_neg, logf


def _hgrn_kernel(q_ref, f_ref, i_ref, g_ref, lb_ref, gain_ref, o_ref, kk_s, v_s, b_s):
    t = HG_T
    dk = q_ref.shape[2]
    n_chunks = q_ref.shape[1] // t
    lb = lb_ref[...]
    one_m_lb = 1.0 - lb
    gain = gain_ref[...]
    row = lax.broadcasted_iota(I32, (t, t), 0)
    col = lax.broadcasted_iota(I32, (t, t), 1)
    causal = row >= col
    tril = causal.astype(BF16)

    def finish(o, g):
        return (_rms(o) * gain * (g * _sigmoid(g))).astype(BF16)

    def chunk(c, carry):
        st, bmax = carry
        rows = pl.ds(pl.multiple_of(c * t, t), t)
        q = q_ref[0, rows, :].astype(F32)
        f = f_ref[0, rows, :].astype(F32)
        v = i_ref[0, rows, :]
        g = g_ref[0, rows, :].astype(F32)
        qs, kk, logf = _hgrn_gates(q, f, lb, one_m_lb)
        hi = logf.astype(BF16)
        lo = (logf - hi.astype(F32)).astype(BF16)
        bb = _dot(tril, jnp.concatenate([hi, lo], axis=-1))
        b = bb[:, :dk] + bb[:, dk:]
        b_last = b[t - 1:t, :]
        mid = 0.5 * b_last
        e_mid = jnp.exp(mid)
        qt = qs * jnp.exp(b - mid)
        kt = kk * jnp.exp(mid - b)
        att = jnp.where(causal, _dot_nt(qt.astype(BF16), kt.astype(BF16)), 0.0)
        o = _dot(att.astype(BF16), v)
        o = o + _dot_nt((qt * e_mid).astype(BF16), st.astype(BF16))
        st = st * (e_mid * e_mid) + _dot_tn(v, (kt * e_mid).astype(BF16))
        o_ref[0, rows, :] = finish(o, g)
        return st, jnp.maximum(bmax, jnp.abs(b_last))

    st0 = jnp.zeros((dk, dk), F32)
    _, bmax = lax.fori_loop(0, n_chunks, chunk, (st0, jnp.zeros_like(lb)), unroll=8)
    safe = 0.5 * jnp.max(bmax) <= HG_MAX_HALF_RANGE

    @pl.when(jnp.logical_not(safe))
    def _():
        n = HG_SUB
        sub_row = lax.broadcasted_iota(I32, (n, 1), 0)
        tril_n = (lax.broadcasted_iota(I32, (n, n), 0) >= lax.broadcasted_iota(I32, (n, n), 1)).astype(F32)

        def block(i, st):
            rows = pl.ds(pl.multiple_of(i * n, n), n)
            q = q_ref[0, rows, :].astype(F32)
            f = f_ref[0, rows, :].astype(F32)
            v = i_ref[0, rows, :]
            g = g_ref[0, rows, :].astype(F32)
            qs, kk, logf = _hgrn_gates(q, f, lb, one_m_lb)
            b = jnp.dot(tril_n, logf, precision=HIGHEST, preferred_element_type=F32)
            kk_s[...] = kk
            v_s[...] = v.astype(F32)
            b_s[...] = b
            o = _dot_nt((qs * jnp.exp(b)).astype(BF16), st.astype(BF16))

            def pair(s, acc):
                dec = jnp.exp(jnp.minimum(b - b_s[pl.ds(s, 1), :], 0.0))
                w = jnp.sum(qs * kk_s[pl.ds(s, 1), :] * dec, axis=-1, keepdims=True)
                return acc + jnp.where(sub_row >= s, w, 0.0) * v_s[pl.ds(s, 1), :]

            o = lax.fori_loop(0, n, pair, o)
            b_last = b[n - 1:n, :]
            st = st * jnp.exp(b_last) + _dot_tn(v, (kk * jnp.exp(b_last - b)).astype(BF16))
            o_ref[0, rows, :] = finish(o, g)
            return st

        lax.fori_loop(0, q_ref.shape[1] // n, block, st0)


def _hgrn(proj, lb, gain):
    bsz, s, w4 = proj.shape
    dk = HG_DK
    nh = w4 // (4 * dk)

    def spec(j):
        return pl.BlockSpec((1, s, dk), lambda b, h: (b, 0, h + j * nh))

    return pl.pallas_call(
        _hgrn_kernel,
        grid=(bsz, nh),
        in_specs=[spec(0), spec(1), spec(2), spec(3),
                  pl.BlockSpec((1, dk), lambda b, h: (0, h)),
                  pl.BlockSpec((1, dk), lambda b, h: (0, 0))],
        out_specs=pl.BlockSpec((1, s, dk), lambda b, h: (b, 0, h)),
        out_shape=jax.ShapeDtypeStruct((bsz, s, nh * dk), BF16),
        scratch_shapes=[pltpu.VMEM((HG_SUB, dk), F32)] * 3,
        compiler_params=pltpu.CompilerParams(dimension_semantics=("parallel", "parallel")),
        name="hgrn",
    )(proj, proj, proj, proj, lb, gain)


def _rope_kernel(pos_ref, inv_ref, cos_ref, sin_ref):
    ang = pos_ref[0].astype(F32) * inv_ref[...]
    cos_ref[0] = jnp.cos(ang)
    sin_ref[0] = jnp.sin(ang)


def _rope_tables(positions):
    bsz, s = positions.shape
    half = RET_DK // 2
    inv_freq = (1.0 / (ROPE_BASE ** jnp.linspace(0.0, 1.0, half, dtype=F32))).reshape(1, half)
    out = jax.ShapeDtypeStruct((bsz, s, half), F32)
    return pl.pallas_call(
        _rope_kernel,
        grid=(bsz,),
        in_specs=[pl.BlockSpec((1, s, 1), lambda b: (b, 0, 0)),
                  pl.BlockSpec((1, half), lambda b: (0, 0))],
        out_specs=[pl.BlockSpec((1, s, half), lambda b: (b, 0, 0))] * 2,
        out_shape=[out, out],
        name="rope_tables",
    )(positions.reshape(bsz, s, 1), inv_freq)


def _ret_kernel(q_ref, k_ref, v_ref, g_ref, cos_ref, sin_ref, lg_ref, o_ref, r_ref, d_ref):
    t = RET_T
    dk = RET_DK
    half = dk // 2
    n_steps = q_ref.shape[1] // t
    lg = lg_ref[0]
    lg_k = lg[:, :dk]
    n = lax.broadcasted_iota(I32, (t, t), 0)
    m = lax.broadcasted_iota(I32, (t, t), 1)
    dist = jnp.abs(n - m).astype(F32)
    visible = (m // RET_CHUNK) <= (n // RET_CHUNK)
    d_ref[...] = jnp.where(visible, jnp.exp(dist * lg[:, :t]), 0.0)
    idx = lax.broadcasted_iota(I32, (t, dk), 0).astype(F32)
    q_decay = jnp.exp((idx + 1.0) * lg_k)
    k_decay = jnp.exp((t - 1.0 - idx) * lg_k)
    step_decay = jnp.exp(float(t) * lg)
    r_ref[...] = jnp.zeros_like(r_ref)

    def rotate(x, cos, sin):
        x1, x2 = x[:, :half], x[:, half:]
        return jnp.concatenate([x1 * cos - x2 * sin, x2 * cos + x1 * sin], axis=-1)

    def step(c, carry):
        r0 = pl.multiple_of(c * t, t)
        rows = pl.ds(r0, t)
        cos = cos_ref[0, rows, :]
        sin = sin_ref[0, rows, :]
        q = rotate(q_ref[0, rows, :].astype(F32), cos, sin)
        k = rotate(k_ref[0, rows, :].astype(F32), cos, sin) * (dk ** -0.5)
        v = v_ref[0, rows, :]
        g = g_ref[0, rows, :].astype(F32)
        r = r_ref[...]
        sc = _dot_nt(q.astype(BF16), k.astype(BF16)) * d_ref[...]
        o = _dot(sc.astype(BF16), v) + _dot((q * q_decay).astype(BF16), r.astype(BF16))
        r_ref[...] = r * step_decay + _dot_tn((k * k_decay).astype(BF16), v)
        out = _rms(o) * (g * _sigmoid(g))
        o_ref[0, rows, :] = out.astype(BF16)
        return carry

    lax.fori_loop(0, n_steps, step, 0)


def _retention(proj, cos, sin):
    bsz, s, w = proj.shape
    dk, dv = RET_DK, RET_DV
    nh = w // (2 * dk + 2 * dv)
    hidx = jnp.arange(nh, dtype=F32)
    log_gamma = jnp.log(1.0 - 2.0 ** (-5.0 - hidx))
    lg = jnp.broadcast_to(log_gamma[:, None, None], (nh, 1, dv))
    vbase = 2 * nh * dk // dv
    return pl.pallas_call(
        _ret_kernel,
        grid=(bsz, nh),
        in_specs=[
            pl.BlockSpec((1, s, dk), lambda b, h: (b, 0, h)),
            pl.BlockSpec((1, s, dk), lambda b, h: (b, 0, nh + h)),
            pl.BlockSpec((1, s, dv), lambda b, h: (b, 0, vbase + h)),
            pl.BlockSpec((1, s, dv), lambda b, h: (b, 0, vbase + nh + h)),
            pl.BlockSpec((1, s, dk // 2), lambda b, h: (b, 0, 0)),
            pl.BlockSpec((1, s, dk // 2), lambda b, h: (b, 0, 0)),
            pl.BlockSpec((1, 1, dv), lambda b, h: (h, 0, 0)),
        ],
        out_specs=pl.BlockSpec((1, s, dv), lambda b, h: (b, 0, h)),
        out_shape=jax.ShapeDtypeStruct((bsz, s, nh * dv), BF16),
        scratch_shapes=[pltpu.VMEM((dk, dv), F32), pltpu.VMEM((RET_T, RET_T), F32)],
        compiler_params=pltpu.CompilerParams(
            dimension_semantics=("parallel", "parallel"), vmem_limit_bytes=V7X_VMEM_LIMIT),
        name="retention",
    )(proj, proj, proj, proj, cos, sin, lg)


def _outproj_kernel(o_ref, w_ref, x_ref, g1_ref, sh_ref, sc_ref, gain_ref, rw_ref, rb_ref, u_ref,
                    xo_ref, ha_ref, hb_ref, eidx_ref, gate_ref, rank_ref, cnt_ref, base_ref):
    first = jnp.logical_and(pl.program_id(0) == 0, pl.program_id(1) == 0)

    @pl.when(first)
    def _():
        base_ref[...] = jnp.zeros_like(base_ref)

    ne = rw_ref.shape[0]
    rw = rw_ref[...]
    rw_hi = rw.astype(BF16)
    rw_lo = (rw - rw_hi.astype(F32)).astype(BF16)
    rw_both = jnp.concatenate([rw_hi, rw_lo], axis=0)
    sub = u_ref.shape[0]
    eiota = lax.broadcasted_iota(I32, (ne, sub), 0)
    base = base_ref[...]
    for r in range(0, x_ref.shape[1], sub):
        rows = slice(r, r + sub)
        y = _dot(o_ref[0, rows, :], w_ref[...])
        xn = x_ref[0, rows, :] + g1_ref[0] * y
        xo_ref[0, rows, :] = xn
        h = _rms(xn) * gain_ref[...] * (1.0 + sc_ref[0]) + sh_ref[0]
        packed = _pack_rows(h)
        slab = packed.shape[1] // ROW_PARTS
        ha_ref[0, rows, :] = packed[:, :slab]
        hb_ref[0, rows, :] = packed[:, slab:]

        h_hi = h.astype(BF16)
        h_lo = (h - h_hi.astype(F32)).astype(BF16)
        part = _dot_nt(rw_both, h_hi)
        work = part[:ne] + part[ne:] + _dot_nt(rw_hi, h_lo) + rb_ref[...]
        onehots, tops = [], []
        for k in range(TOP_K):
            mx = jnp.max(work, axis=0, keepdims=True)
            idx = jnp.min(jnp.where(work == mx, eiota, ne), axis=0, keepdims=True)
            oh = eiota == idx
            work = jnp.where(oh, -jnp.inf, work)
            eidx_ref[k:k + 1, rows] = idx
            onehots.append(oh)
            tops.append(mx)
        ex = [jnp.exp(m - tops[0]) for m in tops]
        denom = ex[0] + ex[1] + ex[2] + ex[3]
        for k in range(TOP_K):
            gate_ref[k:k + 1, rows] = ex[k] / denom

        mask = jnp.zeros((ne, sub), F32)
        for oh in onehots:
            mask = mask + oh.astype(F32)
        incl = _dot(mask.astype(BF16), u_ref[...])
        excl = incl - mask + base
        for k in range(TOP_K):
            rk = jnp.sum(jnp.where(onehots[k], excl, 0.0), axis=0, keepdims=True)
            rank_ref[k:k + 1, rows] = rk.astype(I32)
        base = base + incl[:, sub - 1:sub]
    base_ref[...] = base
    cnt_ref[...] = base.astype(I32)


def _outproj_route(o, w_out, x, mod, gain2, router_wt, router_b):
    bsz, s, d = x.shape
    hv = o.shape[2]
    tm = min(ROW_TILE, s)
    n = bsz * s
    nt = s // tm
    ne = router_wt.shape[0]
    sub = min(OUTPROJ_SUB_ROWS, tm)
    upper = (jnp.arange(sub)[:, None] <= jnp.arange(sub)[None, :]).astype(BF16)

    def modspec(j):
        return pl.BlockSpec((1, 1, d), lambda b, i: (b, 0, j))

    tokspec = pl.BlockSpec((TOP_K, tm), lambda b, i: (0, b * nt + i))
    return pl.pallas_call(
        _outproj_kernel,
        grid=(bsz, nt),
        in_specs=[
            pl.BlockSpec((1, tm, hv), lambda b, i: (b, i, 0)),
            pl.BlockSpec((hv, d), lambda b, i: (0, 0)),
            pl.BlockSpec((1, tm, d), lambda b, i: (b, i, 0)),
            modspec(2), modspec(3), modspec(4),
            pl.BlockSpec((1, d), lambda b, i: (0, 0)),
            pl.BlockSpec((ne, d), lambda b, i: (0, 0)),
            pl.BlockSpec((ne, 1), lambda b, i: (0, 0)),
            pl.BlockSpec((sub, sub), lambda b, i: (0, 0)),
        ],
        out_specs=[
            pl.BlockSpec((1, tm, d), lambda b, i: (b, i, 0)),
            pl.BlockSpec((1, tm, d // 4), lambda b, i: (b, i, 0)),
            pl.BlockSpec((1, tm, d // 4), lambda b, i: (b, i, 0)),
            tokspec, tokspec, tokspec,
            pl.BlockSpec((ne, 1), lambda b, i: (0, 0)),
        ],
        out_shape=[
            jax.ShapeDtypeStruct((bsz, s, d), F32),
            jax.ShapeDtypeStruct((bsz, s, d // 4), I32),
            jax.ShapeDtypeStruct((bsz, s, d // 4), I32),
            jax.ShapeDtypeStruct((TOP_K, n), I32),
            jax.ShapeDtypeStruct((TOP_K, n), F32),
            jax.ShapeDtypeStruct((TOP_K, n), I32),
            jax.ShapeDtypeStruct((ne, 1), I32),
        ],
        scratch_shapes=[pltpu.VMEM((ne, 1), F32)],
        compiler_params=pltpu.CompilerParams(
            dimension_semantics=("arbitrary", "arbitrary"), vmem_limit_bytes=V7X_VMEM_LIMIT),
        name="outproj_route",
    )(o, w_out, x, mod, mod, mod, gain2, router_wt, router_b, upper)


def _moe_kernel(be_ref, nb_ref, xa_ref, xb_ref, w1_ref, b1_ref, w2_ref, b2_ref, ya_ref, yb_ref,
                w1_s, w2_s):
    i = pl.program_id(0)
    used = i < nb_ref[0]
    new_expert = jnp.logical_or(i == 0, be_ref[i] != be_ref[jnp.maximum(i - 1, 0)])

    @pl.when(jnp.logical_and(used, new_expert))
    def _():
        for w_ref, w_s in ((w1_ref, w1_s), (w2_ref, w2_s)):
            for r in range(0, w_s.shape[0], WEIGHT_CAST_ROWS):
                rows = slice(r, r + WEIGHT_CAST_ROWS)
                w_s[rows, :] = w_ref[0, 0, rows, :].astype(BF16)

    @pl.when(used)
    def _():
        f = w2_s.shape[0]
        x = _unpack_rows(jnp.concatenate([xa_ref[...], xb_ref[...]], axis=1)).astype(BF16)
        u = _dot(x, w1_s[...]) + b1_ref[0, 0]
        glu = jnp.minimum(u[:, :f], SWIGLU_LIMIT)
        lin = jnp.clip(u[:, f:], -SWIGLU_LIMIT, SWIGLU_LIMIT)
        a = glu * _sigmoid(SWIGLU_ALPHA * glu) * (lin + 1.0)
        y = _dot(a.astype(BF16), w2_s[...]) + b2_ref[0, 0]
        packed = _pack_rows(y)
        slab = packed.shape[1] // ROW_PARTS
        ya_ref[...] = packed[:, :slab]
        yb_ref[...] = packed[:, slab:]

    @pl.when(jnp.logical_not(used))
    def _():
        ya_ref[...] = jnp.zeros_like(ya_ref)
        yb_ref[...] = jnp.zeros_like(yb_ref)


def _moe_blocks(block_e, n_used, xs, layer, w1, b1, w2, b2):
    n_rows, dh = xs[0].shape
    nl, ne, d, f2 = w1.shape
    f = f2 // 2
    nblk = n_rows // MOE_ROWS
    grid_spec = pltpu.PrefetchScalarGridSpec(
        num_scalar_prefetch=2,
        grid=(nblk,),
        in_specs=[
            pl.BlockSpec((MOE_ROWS, dh), lambda i, be, nb: (i, 0)),
            pl.BlockSpec((MOE_ROWS, dh), lambda i, be, nb: (i, 0)),
            pl.BlockSpec((1, 1, d, f2), lambda i, be, nb: (layer, be[i], 0, 0)),
            pl.BlockSpec((1, 1, 1, f2), lambda i, be, nb: (layer, be[i], 0, 0)),
            pl.BlockSpec((1, 1, f, d), lambda i, be, nb: (layer, be[i], 0, 0)),
            pl.BlockSpec((1, 1, 1, d), lambda i, be, nb: (layer, be[i], 0, 0)),
        ],
        out_specs=[pl.BlockSpec((MOE_ROWS, dh), lambda i, be, nb: (i, 0))] * 2,
        scratch_shapes=[pltpu.VMEM((d, f2), BF16), pltpu.VMEM((f, d), BF16)],
    )
    return pl.pallas_call(
        _moe_kernel,
        grid_spec=grid_spec,
        out_shape=[jax.ShapeDtypeStruct((n_rows, dh), I32)] * 2,
        compiler_params=pltpu.CompilerParams(
            dimension_semantics=("arbitrary",), vmem_limit_bytes=V7X_VMEM_LIMIT),
        name="moe_experts",
    )(block_e, n_used, xs[0], xs[1], w1, b1.reshape(nl, ne, 1, f2), w2, b2.reshape(nl, ne, 1, d))


def _sc_mesh():
    return plsc.VectorSubcoreMesh(core_axis_name="c", subcore_axis_name="s")


def _sc_scatter_rows(srcs, dests, n_rows):
    n, w = srcs[0].shape
    ns, nk = len(srcs), len(dests)
    out = jax.ShapeDtypeStruct((n_rows, w), srcs[0].dtype)

    @functools.partial(pl.kernel, out_type=[out] * ns, mesh=_sc_mesh(), scratch_types=[])
    def scatter_kernel(*refs):
        x_hbm, idx_hbm, o_hbm = refs[:ns], refs[ns:ns + nk], refs[ns + nk:]
        for x, o in zip(x_hbm, o_hbm):
            def body(x_vmem, *idx_vmem, o=o):
                for iv in idx_vmem:
                    pltpu.sync_copy(x_vmem, o.at[iv.at[0]])

            pltpu.emit_pipeline(
                body,
                grid=(n // SC_WINDOW,),
                in_specs=[pl.BlockSpec((SC_WINDOW, w), lambda i: (i, 0))]
                + [pl.BlockSpec((1, SC_WINDOW), lambda i: (0, i))] * nk,
                out_specs=[],
                core_axis_name=("c", "s"),
                dimension_semantics=(pltpu.PARALLEL,),
            )(x, *idx_hbm)

    return scatter_kernel(*srcs, *dests)


def _sc_gather_rows(tables, idx):
    m = idx.shape[1]
    w = tables[0].shape[1]
    nt = len(tables)
    out = jax.ShapeDtypeStruct((m, w), tables[0].dtype)

    @functools.partial(pl.kernel, out_type=[out] * nt, mesh=_sc_mesh(), scratch_types=[])
    def gather_kernel(*refs):
        t_hbm, i_hbm, o_hbm = refs[:nt], refs[nt], refs[nt + 1:]
        for t, o in zip(t_hbm, o_hbm):
            def body(i_vmem, o_vmem, t=t):
                pltpu.sync_copy(t.at[i_vmem.at[0]], o_vmem)

            pltpu.emit_pipeline(
                body,
                grid=(m // SC_WINDOW,),
                in_specs=[pl.BlockSpec((1, SC_WINDOW), lambda i: (0, i))],
                out_specs=[pl.BlockSpec((SC_WINDOW, w), lambda i: (i, 0))],
                core_axis_name=("c", "s"),
                dimension_semantics=(pltpu.PARALLEL,),
            )(i_hbm, o)

    return gather_kernel(*tables, idx)


def _dest_kernel(ps_ref, eidx_ref, rank_ref, o_ref):
    eidx = eidx_ref[...]
    start = jnp.zeros_like(eidx)
    for e in range(N_EXPERTS):
        start = jnp.where(eidx == e, ps_ref[e], start)
    o_ref[...] = start + rank_ref[...]


def _dest_rows(pad_start, eidx, rank):
    k, n = eidx.shape
    tn = min(8192, n)
    blk = pl.BlockSpec((k, tn), lambda i, ps: (0, i))
    return pl.pallas_call(
        _dest_kernel,
        grid_spec=pltpu.PrefetchScalarGridSpec(
            num_scalar_prefetch=1, grid=(n // tn,), in_specs=[blk, blk], out_specs=blk),
        out_shape=jax.ShapeDtypeStruct((k, n), I32),
        name="dest_rows",
    )(pad_start, eidx, rank)


def _final_kernel(x_ref, *refs):
    g_ref, sh_ref, sc_ref, o_ref = refs[N_MOE_REFS:]
    x = _moe_residual(x_ref, refs[:N_MOE_REFS])
    o_ref[0] = _rms(x) * g_ref[...] * (1.0 + sc_ref[0]) + sh_ref[0]


def _final(x, pending, gain, fmod):
    bsz, s, d = x.shape
    tm = min(ROW_TILE, s)
    blk = pl.BlockSpec((1, tm, d), lambda b, i: (b, i, 0))
    moe_specs, moe_args = _moe_residual_operands(pending, bsz, s, d, tm)
    return pl.pallas_call(
        _final_kernel,
        grid=(bsz, s // tm),
        in_specs=[blk] + moe_specs + [
            pl.BlockSpec((1, d), lambda b, i: (0, 0)),
            pl.BlockSpec((1, 1, d), lambda b, i: (b, 0, 0)),
            pl.BlockSpec((1, 1, d), lambda b, i: (b, 0, 1))],
        out_specs=blk,
        out_shape=jax.ShapeDtypeStruct((bsz, s, d), F32),
        compiler_params=pltpu.CompilerParams(dimension_semantics=("parallel", "parallel")),
        name="final_norm",
    )(x, *moe_args, gain, fmod, fmod)


def _moe_layer(h2, eidx, gate, rank, counts, layer, w1, b1, w2, b2):
    bsz, s, dh = h2[0].shape
    n = bsz * s
    ne = w1.shape[1]
    nblk = -(-(n * TOP_K) // MOE_ROWS) + ne
    n_rows = nblk * MOE_ROWS
    counts = counts[:, 0]
    padded = (counts + MOE_ROWS - 1) // MOE_ROWS * MOE_ROWS
    pad_end = jnp.cumsum(padded)
    pad_start = pad_end - padded
    block_start = jnp.arange(nblk, dtype=I32)[:, None] * MOE_ROWS
    block_e = jnp.minimum(jnp.sum(pad_end[None, :] <= block_start, axis=1), ne - 1).astype(I32)
    n_used = (pad_end[-1:] // MOE_ROWS).astype(I32)
    dest = _dest_rows(pad_start.astype(I32), eidx, rank)
    xs = _sc_scatter_rows([h.reshape(n, dh) for h in h2], [dest[k:k + 1] for k in range(TOP_K)], n_rows)
    ys = _moe_blocks(block_e, n_used, xs, layer, w1, b1, w2, b2)
    yg = _sc_gather_rows(ys, dest.reshape(1, TOP_K * n))
    return yg, gate.T


def _hgrn_lower_bounds(lb_logits):
    p = jax.nn.softmax(lb_logits.astype(F32), axis=0)
    cum = jnp.cumsum(p, axis=0)
    return cum - cum[0:1]


def kernel(x, c, positions, ada_w, ada_b, norm1_g, norm2_g, hgrn_w_in, hgrn_w_out, hgrn_o_gain, hgrn_lb_logits, ret_w_in, ret_w_out, router_w, router_b, moe_w1, moe_b1, moe_w2, moe_b2, final_g, final_ada_w, final_ada_b):
    depth = ada_w.shape[0]
    mods = _ada(c, ada_w, ada_b)
    fmod = _ada(c, final_ada_w[None], final_ada_b[None])[0][:, None, :]
    lbs = _hgrn_lower_bounds(hgrn_lb_logits)
    cos, sin = _rope_tables(positions)
    pending = None
    for layer in range(depth):
        mod = mods[layer][:, None, :]
        j = layer // N_MIXERS
        if layer % N_MIXERS == 0:
            x, proj = _inproj(x, pending, norm1_g[layer][None], mod, hgrn_w_in[j].astype(BF16))
            o = _hgrn(proj, lbs[j][None], hgrn_o_gain[j][None])
            w_out = hgrn_w_out[j]
        else:
            x, proj = _inproj(x, pending, norm1_g[layer][None], mod, ret_w_in[j].astype(BF16))
            o = _retention(proj, cos, sin)
            w_out = ret_w_out[j]
        x, ha, hb, eidx, gate, rank, counts = _outproj_route(
            o, w_out.astype(BF16), x, mod, norm2_g[layer][None],
            router_w[layer].T, router_b[layer][:, None])
        yg, gate_rows = _moe_layer((ha, hb), eidx, gate, rank, counts, layer, moe_w1, moe_b1, moe_w2, moe_b2)
        pending = (yg, gate_rows, mod)
    return _final(x, pending, final_g[None], fmod)
```

```python
import functools

import jax
import jax.numpy as jnp
from jax import lax
from jax.experimental import pallas as pl
from jax.experimental.pallas import tpu as pltpu
from jax.experimental.pallas import tpu_sc as plsc

F32 = jnp.float32
BF16 = jnp.bfloat16
I32 = jnp.int32
HIGHEST = lax.Precision.HIGHEST

EPS = 1e-6
N_MIXERS = 2
HG_DK = 128
HG_T = 64
HG_SUB = 16
HG_MAX_HALF_RANGE = 80.0
RET_DK = 256
RET_DV = 512
RET_CHUNK = 64
RET_T = 256
ROPE_BASE = 10000.0
N_EXPERTS = 32
TOP_K = 4
SWIGLU_ALPHA = 1.702
SWIGLU_LIMIT = 7.0
MOE_ROWS = 512
WEIGHT_CAST_ROWS = 128
ROW_TILE = 512
INPROJ_SUB_ROWS = 256
BATCH_GROUPS = 2
OUTPROJ_SUB_ROWS = 512
SC_WINDOW = 128
ROW_PARTS = 2
V7X_VMEM_LIMIT = 56 * 1024 * 1024


def _dot(a, b):
    return jnp.dot(a, b, preferred_element_type=F32)


def _dot_nt(a, b, precision=None):
    return lax.dot_general(a, b, (((1,), (1,)), ((), ())), precision=precision,
                           preferred_element_type=F32)


def _dot_tn(a, b):
    return lax.dot_general(a, b, (((0,), (0,)), ((), ())), preferred_element_type=F32)


def _rms(x):
    return x * lax.rsqrt(jnp.mean(x * x, axis=-1, keepdims=True) + EPS)


def _sigmoid(x):
    return 1.0 / (1.0 + jnp.exp(-x))


def _pack_rows(h):
    half = h.shape[1] // 2
    a = lax.bitcast_convert_type(h[:, :half].astype(BF16).astype(F32), jnp.uint32)
    b = lax.bitcast_convert_type(h[:, half:].astype(BF16).astype(F32), jnp.uint32)
    return lax.bitcast_convert_type(a | (b >> 16), I32)


def _unpack_rows(w):
    u = lax.bitcast_convert_type(w, jnp.uint32)
    a = lax.bitcast_convert_type(u & jnp.uint32(0xFFFF0000), F32)
    b = lax.bitcast_convert_type(u << 16, F32)
    return jnp.concatenate([a, b], axis=1)


def _ada_kernel(c_ref, w_ref, b_ref, o_ref):
    c = c_ref[...]
    cond = c * _sigmoid(c)
    o_ref[0] = jnp.dot(cond, w_ref[0], precision=HIGHEST, preferred_element_type=F32) + b_ref[0]


def _ada(c, w, b):
    nl, d, kd = w.shape
    bsz = c.shape[0]
    return pl.pallas_call(
        _ada_kernel,
        grid=(nl, kd // d),
        in_specs=[
            pl.BlockSpec((bsz, d), lambda l, j: (0, 0)),
            pl.BlockSpec((1, d, d), lambda l, j: (l, 0, j)),
            pl.BlockSpec((1, 1, d), lambda l, j: (l, 0, j)),
        ],
        out_specs=pl.BlockSpec((1, bsz, d), lambda l, j: (l, 0, j)),
        out_shape=jax.ShapeDtypeStruct((nl, bsz, kd), F32),
        name="ada_mod",
    )(c, w, b.reshape(nl, 1, kd))


def _moe_residual(x_ref, refs, rows=slice(None)):
    ya_refs, yb_refs = refs[:TOP_K], refs[TOP_K:2 * TOP_K]
    gate_ref, g2_ref = refs[2 * TOP_K:]
    gate = gate_ref[rows, :]
    acc = None
    for k in range(TOP_K):
        packed = jnp.concatenate([ya_refs[k][rows, :], yb_refs[k][rows, :]], axis=1)
        term = gate[:, k:k + 1] * _unpack_rows(packed)
        acc = term if acc is None else acc + term
    return x_ref[0, rows, :] + g2_ref[0] * acc


N_MOE_REFS = 2 * TOP_K + 2


def _moe_residual_operands(pending, bsz, s, d, tm):
    yg, gate_rows, mod = pending
    nt = s // tm
    ntok = (bsz * s) // tm

    def yspec(k):
        return pl.BlockSpec((tm, d // 4), lambda b, i: (k * ntok + b * nt + i, 0))

    specs = [yspec(k) for k in range(TOP_K)] * 2 + [
        pl.BlockSpec((tm, TOP_K), lambda b, i: (b * nt + i, 0)),
        pl.BlockSpec((1, 1, d), lambda b, i: (b, 0, 5))]
    return specs, [yg[0]] * TOP_K + [yg[1]] * TOP_K + [gate_rows, mod]


def _inproj_kernel(x_ref, *refs, col_chunk, fused):
    if fused:
        g_ref, sh_ref, sc_ref, w_ref, o_ref, xo_ref = refs[N_MOE_REFS:]
    else:
        g_ref, sh_ref, sc_ref, w_ref, o_ref = refs
    nout = w_ref.shape[1]
    tm = x_ref.shape[1]
    sub = min(INPROJ_SUB_ROWS, tm)
    for r in range(0, tm, sub):
        rows = slice(r, r + sub)
        if fused:
            x = _moe_residual(x_ref, refs[:N_MOE_REFS], rows)
            xo_ref[0, rows, :] = x
        else:
            x = x_ref[0, rows, :]
        h = _rms(x) * g_ref[...] * (1.0 + sc_ref[0]) + sh_ref[0]
        hb = h.astype(BF16)
        for j in range(nout // col_chunk):
            cs = slice(j * col_chunk, (j + 1) * col_chunk)
            o_ref[0, rows, cs] = _dot(hb, w_ref[:, cs]).astype(BF16)


def _inproj(x, pending, gain, mod, w):
    bsz, s, d = x.shape
    nout = w.shape[1]
    tm = min(ROW_TILE, s)
    xspec = pl.BlockSpec((1, tm, d), lambda b, i: (b, i, 0))
    fused = pending is not None
    moe_specs, moe_args = _moe_residual_operands(pending, bsz, s, d, tm) if fused else ([], [])
    proj_spec = pl.BlockSpec((1, tm, nout), lambda b, i: (b, i, 0))
    proj_shape = jax.ShapeDtypeStruct((bsz, s, nout), BF16)
    out = pl.pallas_call(
        functools.partial(_inproj_kernel, col_chunk=1024, fused=fused),
        grid=(bsz, s // tm),
        in_specs=[xspec] + moe_specs + [
            pl.BlockSpec((1, d), lambda b, i: (0, 0)),
            pl.BlockSpec((1, 1, d), lambda b, i: (b, 0, 0)),
            pl.BlockSpec((1, 1, d), lambda b, i: (b, 0, 1)),
            pl.BlockSpec((d, nout), lambda b, i: (0, 0), pipeline_mode=pl.Buffered(1)),
        ],
        out_specs=[proj_spec, xspec] if fused else proj_spec,
        out_shape=[proj_shape, jax.ShapeDtypeStruct((bsz, s, d), F32)] if fused else proj_shape,
        compiler_params=pltpu.CompilerParams(
            dimension_semantics=("parallel", "parallel"), vmem_limit_bytes=V7X_VMEM_LIMIT),
        name="inproj",
    )(x, *moe_args, gain, mod, mod, w)
    return (out[1], out[0]) if fused else (x, out)


def _hgrn_gates(q, f, lb, one_m_lb):
    e = jnp.exp(-jnp.abs(f))
    inv = 1.0 / (1.0 + e)
    pos = f >= 0.0
    t = e * inv
    sig = jnp.where(pos, inv, t)
    sig_neg = jnp.where(pos, t, inv)
    has_lb = lb > 0.0
    logf = jnp.log(jnp.where(has_lb, lb + one_m_lb * sig, inv)) + jnp.where(has_lb, 0.0, jnp.minimum(f, 0.0))
    return q * _sigmoid(q), one_m_lb * sig_neg, logf


def _hgrn_kernel(q_ref, f_ref, i_ref, g_ref, lb_ref, gain_ref, o_ref, kk_s, v_s, b_s):
    t = HG_T
    dk = q_ref.shape[2]
    n_chunks = q_ref.shape[1] // t
    lb = lb_ref[...]
    one_m_lb = 1.0 - lb
    gain = gain_ref[...]
    row = lax.broadcasted_iota(I32, (t, t), 0)
    col = lax.broadcasted_iota(I32, (t, t), 1)
    causal = row >= col
    tril = causal.astype(BF16)

    def finish(o, g):
        return (_rms(o) * gain * (g * _sigmoid(g))).astype(BF16)

    def chunk(c, carry):
        st, bmax = carry
        rows = pl.ds(pl.multiple_of(c * t, t), t)
        q = q_ref[0, rows, :].astype(F32)
        f = f_ref[0, rows, :].astype(F32)
        v = i_ref[0, rows, :]
        g = g_ref[0, rows, :].astype(F32)
        qs, kk, logf = _hgrn_gates(q, f, lb, one_m_lb)
        hi = logf.astype(BF16)
        lo = (logf - hi.astype(F32)).astype(BF16)
        bb = _dot(tril, jnp.concatenate([hi, lo], axis=-1))
        b = bb[:, :dk] + bb[:, dk:]
        b_last = b[t - 1:t, :]
        mid = 0.5 * b_last
        e_mid = jnp.exp(mid)
        qt = qs * jnp.exp(b - mid)
        kt = kk * jnp.exp(mid - b)
        att = jnp.where(causal, _dot_nt(qt.astype(BF16), kt.astype(BF16)), 0.0)
        o = _dot(att.astype(BF16), v)
        o = o + _dot_nt((qt * e_mid).astype(BF16), st.astype(BF16))
        st = st * (e_mid * e_mid) + _dot_tn(v, (kt * e_mid).astype(BF16))
        o_ref[0, rows, :] = finish(o, g)
        return st, jnp.maximum(bmax, jnp.abs(b_last))

    st0 = jnp.zeros((dk, dk), F32)
    _, bmax = lax.fori_loop(0, n_chunks, chunk, (st0, jnp.zeros_like(lb)), unroll=8)
    safe = 0.5 * jnp.max(bmax) <= HG_MAX_HALF_RANGE

    @pl.when(jnp.logical_not(safe))
    def _():
        n = HG_SUB
        sub_row = lax.broadcasted_iota(I32, (n, 1), 0)
        tril_n = (lax.broadcasted_iota(I32, (n, n), 0) >= lax.broadcasted_iota(I32, (n, n), 1)).astype(F32)

        def block(i, st):
            rows = pl.ds(pl.multiple_of(i * n, n), n)
            q = q_ref[0, rows, :].astype(F32)
            f = f_ref[0, rows, :].astype(F32)
            v = i_ref[0, rows, :]
            g = g_ref[0, rows, :].astype(F32)
            qs, kk, logf = _hgrn_gates(q, f, lb, one_m_lb)
            b = jnp.dot(tril_n, logf, precision=HIGHEST, preferred_element_type=F32)
            kk_s[...] = kk
            v_s[...] = v.astype(F32)
            b_s[...] = b
            o = _dot_nt((qs * jnp.exp(b)).astype(BF16), st.astype(BF16))

            def pair(s, acc):
                dec = jnp.exp(jnp.minimum(b - b_s[pl.ds(s, 1), :], 0.0))
                w = jnp.sum(qs * kk_s[pl.ds(s, 1), :] * dec, axis=-1, keepdims=True)
                return acc + jnp.where(sub_row >= s, w, 0.0) * v_s[pl.ds(s, 1), :]

            o = lax.fori_loop(0, n, pair, o)
            b_last = b[n - 1:n, :]
            st = st * jnp.exp(b_last) + _dot_tn(v, (kk * jnp.exp(b_last - b)).astype(BF16))
            o_ref[0, rows, :] = finish(o, g)
            return st

        lax.fori_loop(0, q_ref.shape[1] // n, block, st0)


def _hgrn(proj, lb, gain):
    bsz, s, w4 = proj.shape
    dk = HG_DK
    nh = w4 // (4 * dk)

    def spec(j):
        return pl.BlockSpec((1, s, dk), lambda b, h: (b, 0, h + j * nh))

    return pl.pallas_call(
        _hgrn_kernel,
        grid=(bsz, nh),
        in_specs=[spec(0), spec(1), spec(2), spec(3),
                  pl.BlockSpec((1, dk), lambda b, h: (0, h)),
                  pl.BlockSpec((1, dk), lambda b, h: (0, 0))],
        out_specs=pl.BlockSpec((1, s, dk), lambda b, h: (b, 0, h)),
        out_shape=jax.ShapeDtypeStruct((bsz, s, nh * dk), BF16),
        scratch_shapes=[pltpu.VMEM((HG_SUB, dk), F32)] * 3,
        compiler_params=pltpu.CompilerParams(dimension_semantics=("parallel", "parallel")),
        name="hgrn",
    )(proj, proj, proj, proj, lb, gain)


def _rope_kernel(pos_ref, inv_ref, cos_ref, sin_ref):
    ang = pos_ref[0].astype(F32) * inv_ref[...]
    cos_ref[0] = jnp.cos(ang)
    sin_ref[0] = jnp.sin(ang)


def _rope_tables(positions):
    bsz, s = positions.shape
    half = RET_DK // 2
    inv_freq = (1.0 / (ROPE_BASE ** jnp.linspace(0.0, 1.0, half, dtype=F32))).reshape(1, half)
    out = jax.ShapeDtypeStruct((bsz, s, half), F32)
    return pl.pallas_call(
        _rope_kernel,
        grid=(bsz,),
        in_specs=[pl.BlockSpec((1, s, 1), lambda b: (b, 0, 0)),
                  pl.BlockSpec((1, half), lambda b: (0, 0))],
        out_specs=[pl.BlockSpec((1, s, half), lambda b: (b, 0, 0))] * 2,
        out_shape=[out, out],
        name="rope_tables",
    )(positions.reshape(bsz, s, 1), inv_freq)


def _ret_kernel(q_ref, k_ref, v_ref, g_ref, cos_ref, sin_ref, lg_ref, o_ref, r_ref, d_ref):
    t = RET_T
    dk = RET_DK
    half = dk // 2
    n_steps = q_ref.shape[1] // t
    lg = lg_ref[0]
    lg_k = lg[:, :dk]
    n = lax.broadcasted_iota(I32, (t, t), 0)
    m = lax.broadcasted_iota(I32, (t, t), 1)
    dist = jnp.abs(n - m).astype(F32)
    visible = (m // RET_CHUNK) <= (n // RET_CHUNK)
    d_ref[...] = jnp.where(visible, jnp.exp(dist * lg[:, :t]), 0.0)
    idx = lax.broadcasted_iota(I32, (t, dk), 0).astype(F32)
    q_decay = jnp.exp((idx + 1.0) * lg_k)
    k_decay = jnp.exp((t - 1.0 - idx) * lg_k)
    step_decay = jnp.exp(float(t) * lg)
    r_ref[...] = jnp.zeros_like(r_ref)

    def rotate(x, cos, sin):
        x1, x2 = x[:, :half], x[:, half:]
        return jnp.concatenate([x1 * cos - x2 * sin, x2 * cos + x1 * sin], axis=-1)

    def step(c, carry):
        r0 = pl.multiple_of(c * t, t)
        rows = pl.ds(r0, t)
        cos = cos_ref[0, rows, :]
        sin = sin_ref[0, rows, :]
        q = rotate(q_ref[0, rows, :].astype(F32), cos, sin)
        k = rotate(k_ref[0, rows, :].astype(F32), cos, sin) * (dk ** -0.5)
        v = v_ref[0, rows, :]
        g = g_ref[0, rows, :].astype(F32)
        r = r_ref[...]
        sc = _dot_nt(q.astype(BF16), k.astype(BF16)) * d_ref[...]
        o = _dot(sc.astype(BF16), v) + _dot((q * q_decay).astype(BF16), r.astype(BF16))
        r_ref[...] = r * step_decay + _dot_tn((k * k_decay).astype(BF16), v)
        out = _rms(o) * (g * _sigmoid(g))
        o_ref[0, rows, :] = out.astype(BF16)
        return carry

    lax.fori_loop(0, n_steps, step, 0)


def _retention(proj, cos, sin):
    bsz, s, w = proj.shape
    dk, dv = RET_DK, RET_DV
    nh = w // (2 * dk + 2 * dv)
    hidx = jnp.arange(nh, dtype=F32)
    log_gamma = jnp.log(1.0 - 2.0 ** (-5.0 - hidx))
    lg = jnp.broadcast_to(log_gamma[:, None, None], (nh, 1, dv))
    vbase = 2 * nh * dk // dv
    return pl.pallas_call(
        _ret_kernel,
        grid=(bsz, nh),
        in_specs=[
            pl.BlockSpec((1, s, dk), lambda b, h: (b, 0, h)),
            pl.BlockSpec((1, s, dk), lambda b, h: (b, 0, nh + h)),
            pl.BlockSpec((1, s, dv), lambda b, h: (b, 0, vbase + h)),
            pl.BlockSpec((1, s, dv), lambda b, h: (b, 0, vbase + nh + h)),
            pl.BlockSpec((1, s, dk // 2), lambda b, h: (b, 0, 0)),
            pl.BlockSpec((1, s, dk // 2), lambda b, h: (b, 0, 0)),
            pl.BlockSpec((1, 1, dv), lambda b, h: (h, 0, 0)),
        ],
        out_specs=pl.BlockSpec((1, s, dv), lambda b, h: (b, 0, h)),
        out_shape=jax.ShapeDtypeStruct((bsz, s, nh * dv), BF16),
        scratch_shapes=[pltpu.VMEM((dk, dv), F32), pltpu.VMEM((RET_T, RET_T), F32)],
        compiler_params=pltpu.CompilerParams(
            dimension_semantics=("parallel", "parallel"), vmem_limit_bytes=V7X_VMEM_LIMIT),
        name="retention",
    )(proj, proj, proj, proj, cos, sin, lg)


def _outproj_kernel(o_ref, w_ref, x_ref, g1_ref, sh_ref, sc_ref, gain_ref, rw_ref, rb_ref, u_ref,
                    xo_ref, ha_ref, hb_ref, eidx_ref, gate_ref, rank_ref, cnt_ref, base_ref):
    first = jnp.logical_and(pl.program_id(0) == 0, pl.program_id(1) == 0)

    @pl.when(first)
    def _():
        base_ref[...] = jnp.zeros_like(base_ref)

    ne = rw_ref.shape[0]
    rw = rw_ref[...]
    rw_hi = rw.astype(BF16)
    rw_lo = (rw - rw_hi.astype(F32)).astype(BF16)
    rw_both = jnp.concatenate([rw_hi, rw_lo], axis=0)
    sub = u_ref.shape[0]
    eiota = lax.broadcasted_iota(I32, (ne, sub), 0)
    base = base_ref[...]
    for r in range(0, x_ref.shape[1], sub):
        rows = slice(r, r + sub)
        y = _dot(o_ref[0, rows, :], w_ref[...])
        xn = x_ref[0, rows, :] + g1_ref[0] * y
        xo_ref[0, rows, :] = xn
        h = _rms(xn) * gain_ref[...] * (1.0 + sc_ref[0]) + sh_ref[0]
        packed = _pack_rows(h)
        slab = packed.shape[1] // ROW_PARTS
        ha_ref[0, rows, :] = packed[:, :slab]
        hb_ref[0, rows, :] = packed[:, slab:]

        h_hi = h.astype(BF16)
        h_lo = (h - h_hi.astype(F32)).astype(BF16)
        part = _dot_nt(rw_both, h_hi)
        work = part[:ne] + part[ne:] + _dot_nt(rw_hi, h_lo) + rb_ref[...]
        onehots, tops = [], []
        for k in range(TOP_K):
            mx = jnp.max(work, axis=0, keepdims=True)
            idx = jnp.min(jnp.where(work == mx, eiota, ne), axis=0, keepdims=True)
            oh = eiota == idx
            work = jnp.where(oh, -jnp.inf, work)
            eidx_ref[k:k + 1, rows] = idx
            onehots.append(oh)
            tops.append(mx)
        ex = [jnp.exp(m - tops[0]) for m in tops]
        denom = ex[0] + ex[1] + ex[2] + ex[3]
        for k in range(TOP_K):
            gate_ref[k:k + 1, rows] = ex[k] / denom

        mask = jnp.zeros((ne, sub), F32)
        for oh in onehots:
            mask = mask + oh.astype(F32)
        incl = _dot(mask.astype(BF16), u_ref[...])
        excl = incl - mask + base
        for k in range(TOP_K):
            rk = jnp.sum(jnp.where(onehots[k], excl, 0.0), axis=0, keepdims=True)
            rank_ref[k:k + 1, rows] = rk.astype(I32)
        base = base + incl[:, sub - 1:sub]
    base_ref[...] = base
    cnt_ref[...] = base.astype(I32)


def _outproj_route(o, w_out, x, mod, gain2, router_wt, router_b):
    bsz, s, d = x.shape
    hv = o.shape[2]
    tm = min(ROW_TILE, s)
    n = bsz * s
    nt = s // tm
    ne = router_wt.shape[0]
    sub = min(OUTPROJ_SUB_ROWS, tm)
    upper = (jnp.arange(sub)[:, None] <= jnp.arange(sub)[None, :]).astype(BF16)

    def modspec(j):
        return pl.BlockSpec((1, 1, d), lambda b, i: (b, 0, j))

    tokspec = pl.BlockSpec((TOP_K, tm), lambda b, i: (0, b * nt + i))
    return pl.pallas_call(
        _outproj_kernel,
        grid=(bsz, nt),
        in_specs=[
            pl.BlockSpec((1, tm, hv), lambda b, i: (b, i, 0)),
            pl.BlockSpec((hv, d), lambda b, i: (0, 0)),
            pl.BlockSpec((1, tm, d), lambda b, i: (b, i, 0)),
            modspec(2), modspec(3), modspec(4),
            pl.BlockSpec((1, d), lambda b, i: (0, 0)),
            pl.BlockSpec((ne, d), lambda b, i: (0, 0)),
            pl.BlockSpec((ne, 1), lambda b, i: (0, 0)),
            pl.BlockSpec((sub, sub), lambda b, i: (0, 0)),
        ],
        out_specs=[
            pl.BlockSpec((1, tm, d), lambda b, i: (b, i, 0)),
            pl.BlockSpec((1, tm, d // 4), lambda b, i: (b, i, 0)),
            pl.BlockSpec((1, tm, d // 4), lambda b, i: (b, i, 0)),
            tokspec, tokspec, tokspec,
            pl.BlockSpec((ne, 1), lambda b, i: (0, 0)),
        ],
        out_shape=[
            jax.ShapeDtypeStruct((bsz, s, d), F32),
            jax.ShapeDtypeStruct((bsz, s, d // 4), I32),
            jax.ShapeDtypeStruct((bsz, s, d // 4), I32),
            jax.ShapeDtypeStruct((TOP_K, n), I32),
            jax.ShapeDtypeStruct((TOP_K, n), F32),
            jax.ShapeDtypeStruct((TOP_K, n), I32),
            jax.ShapeDtypeStruct((ne, 1), I32),
        ],
        scratch_shapes=[pltpu.VMEM((ne, 1), F32)],
        compiler_params=pltpu.CompilerParams(
            dimension_semantics=("arbitrary", "arbitrary"), vmem_limit_bytes=V7X_VMEM_LIMIT),
        name="outproj_route",
    )(o, w_out, x, mod, mod, mod, gain2, router_wt, router_b, upper)


def _moe_kernel(be_ref, nb_ref, xa_ref, xb_ref, w1_ref, b1_ref, w2_ref, b2_ref, ya_ref, yb_ref,
                w1_s, w2_s):
    i = pl.program_id(0)
    used = i < nb_ref[0]
    new_expert = jnp.logical_or(i == 0, be_ref[i] != be_ref[jnp.maximum(i - 1, 0)])

    @pl.when(jnp.logical_and(used, new_expert))
    def _():
        for w_ref, w_s in ((w1_ref, w1_s), (w2_ref, w2_s)):
            for r in range(0, w_s.shape[0], WEIGHT_CAST_ROWS):
                rows = slice(r, r + WEIGHT_CAST_ROWS)
                w_s[rows, :] = w_ref[0, 0, rows, :].astype(BF16)

    @pl.when(used)
    def _():
        f = w2_s.shape[0]
        x = _unpack_rows(jnp.concatenate([xa_ref[...], xb_ref[...]], axis=1)).astype(BF16)
        u = _dot(x, w1_s[...]) + b1_ref[0, 0]
        glu = jnp.minimum(u[:, :f], SWIGLU_LIMIT)
        lin = jnp.clip(u[:, f:], -SWIGLU_LIMIT, SWIGLU_LIMIT)
        a = glu * _sigmoid(SWIGLU_ALPHA * glu) * (lin + 1.0)
        y = _dot(a.astype(BF16), w2_s[...]) + b2_ref[0, 0]
        packed = _pack_rows(y)
        slab = packed.shape[1] // ROW_PARTS
        ya_ref[...] = packed[:, :slab]
        yb_ref[...] = packed[:, slab:]

    @pl.when(jnp.logical_not(used))
    def _():
        ya_ref[...] = jnp.zeros_like(ya_ref)
        yb_ref[...] = jnp.zeros_like(yb_ref)


def _moe_blocks(block_e, n_used, xs, layer, w1, b1, w2, b2):
    n_rows, dh = xs[0].shape
    nl, ne, d, f2 = w1.shape
    f = f2 // 2
    nblk = n_rows // MOE_ROWS
    grid_spec = pltpu.PrefetchScalarGridSpec(
        num_scalar_prefetch=2,
        grid=(nblk,),
        in_specs=[
            pl.BlockSpec((MOE_ROWS, dh), lambda i, be, nb: (i, 0)),
            pl.BlockSpec((MOE_ROWS, dh), lambda i, be, nb: (i, 0)),
            pl.BlockSpec((1, 1, d, f2), lambda i, be, nb: (layer, be[i], 0, 0)),
            pl.BlockSpec((1, 1, 1, f2), lambda i, be, nb: (layer, be[i], 0, 0)),
            pl.BlockSpec((1, 1, f, d), lambda i, be, nb: (layer, be[i], 0, 0)),
            pl.BlockSpec((1, 1, 1, d), lambda i, be, nb: (layer, be[i], 0, 0)),
        ],
        out_specs=[pl.BlockSpec((MOE_ROWS, dh), lambda i, be, nb: (i, 0))] * 2,
        scratch_shapes=[pltpu.VMEM((d, f2), BF16), pltpu.VMEM((f, d), BF16)],
    )
    return pl.pallas_call(
        _moe_kernel,
        grid_spec=grid_spec,
        out_shape=[jax.ShapeDtypeStruct((n_rows, dh), I32)] * 2,
        compiler_params=pltpu.CompilerParams(
            dimension_semantics=("arbitrary",), vmem_limit_bytes=V7X_VMEM_LIMIT),
        name="moe_experts",
    )(block_e, n_used, xs[0], xs[1], w1, b1.reshape(nl, ne, 1, f2), w2, b2.reshape(nl, ne, 1, d))


def _sc_mesh():
    return plsc.VectorSubcoreMesh(core_axis_name="c", subcore_axis_name="s")


def _sc_scatter_rows(srcs, dests, n_rows):
    n, w = srcs[0].shape
    ns, nk = len(srcs), len(dests)
    out = jax.ShapeDtypeStruct((n_rows, w), srcs[0].dtype)

    @functools.partial(pl.kernel, out_type=[out] * ns, mesh=_sc_mesh(), scratch_types=[])
    def scatter_kernel(*refs):
        x_hbm, idx_hbm, o_hbm = refs[:ns], refs[ns:ns + nk], refs[ns + nk:]
        for x, o in zip(x_hbm, o_hbm):
            def body(x_vmem, *idx_vmem, o=o):
                for iv in idx_vmem:
                    pltpu.sync_copy(x_vmem, o.at[iv.at[0]])

            pltpu.emit_pipeline(
                body,
                grid=(n // SC_WINDOW,),
                in_specs=[pl.BlockSpec((SC_WINDOW, w), lambda i: (i, 0))]
                + [pl.BlockSpec((1, SC_WINDOW), lambda i: (0, i))] * nk,
                out_specs=[],
                core_axis_name=("c", "s"),
                dimension_semantics=(pltpu.PARALLEL,),
            )(x, *idx_hbm)

    return scatter_kernel(*srcs, *dests)


def _sc_gather_rows(tables, idx):
    m = idx.shape[1]
    w = tables[0].shape[1]
    nt = len(tables)
    out = jax.ShapeDtypeStruct((m, w), tables[0].dtype)

    @functools.partial(pl.kernel, out_type=[out] * nt, mesh=_sc_mesh(), scratch_types=[])
    def gather_kernel(*refs):
        t_hbm, i_hbm, o_hbm = refs[:nt], refs[nt], refs[nt + 1:]
        for t, o in zip(t_hbm, o_hbm):
            def body(i_vmem, o_vmem, t=t):
                pltpu.sync_copy(t.at[i_vmem.at[0]], o_vmem)

            pltpu.emit_pipeline(
                body,
                grid=(m // SC_WINDOW,),
                in_specs=[pl.BlockSpec((1, SC_WINDOW), lambda i: (0, i))],
                out_specs=[pl.BlockSpec((SC_WINDOW, w), lambda i: (i, 0))],
                core_axis_name=("c", "s"),
                dimension_semantics=(pltpu.PARALLEL,),
            )(i_hbm, o)

    return gather_kernel(*tables, idx)


def _dest_kernel(ps_ref, eidx_ref, rank_ref, o_ref):
    eidx = eidx_ref[...]
    start = jnp.zeros_like(eidx)
    for e in range(N_EXPERTS):
        start = jnp.where(eidx == e, ps_ref[e], start)
    o_ref[...] = start + rank_ref[...]


def _dest_rows(pad_start, eidx, rank):
    k, n = eidx.shape
    tn = min(8192, n)
    blk = pl.BlockSpec((k, tn), lambda i, ps: (0, i))
    return pl.pallas_call(
        _dest_kernel,
        grid_spec=pltpu.PrefetchScalarGridSpec(
            num_scalar_prefetch=1, grid=(n // tn,), in_specs=[blk, blk], out_specs=blk),
        out_shape=jax.ShapeDtypeStruct((k, n), I32),
        name="dest_rows",
    )(pad_start, eidx, rank)


def _final_kernel(x_ref, *refs):
    g_ref, sh_ref, sc_ref, o_ref = refs[N_MOE_REFS:]
    x = _moe_residual(x_ref, refs[:N_MOE_REFS])
    o_ref[0] = _rms(x) * g_ref[...] * (1.0 + sc_ref[0]) + sh_ref[0]


def _final(x, pending, gain, fmod):
    bsz, s, d = x.shape
    tm = min(ROW_TILE, s)
    blk = pl.BlockSpec((1, tm, d), lambda b, i: (b, i, 0))
    moe_specs, moe_args = _moe_residual_operands(pending, bsz, s, d, tm)
    return pl.pallas_call(
        _final_kernel,
        grid=(bsz, s // tm),
        in_specs=[blk] + moe_specs + [
            pl.BlockSpec((1, d), lambda b, i: (0, 0)),
            pl.BlockSpec((1, 1, d), lambda b, i: (b, 0, 0)),
            pl.BlockSpec((1, 1, d), lambda b, i: (b, 0, 1))],
        out_specs=blk,
        out_shape=jax.ShapeDtypeStruct((bsz, s, d), F32),
        compiler_params=pltpu.CompilerParams(dimension_semantics=("parallel", "parallel")),
        name="final_norm",
    )(x, *moe_args, gain, fmod, fmod)


def _moe_layer(h2, eidx, gate, rank, counts, layer, w1, b1, w2, b2):
    bsz, s, dh = h2[0].shape
    n = bsz * s
    ne = w1.shape[1]
    nblk = -(-(n * TOP_K) // MOE_ROWS) + ne
    n_rows = nblk * MOE_ROWS
    counts = counts[:, 0]
    padded = (counts + MOE_ROWS - 1) // MOE_ROWS * MOE_ROWS
    pad_end = jnp.cumsum(padded)
    pad_start = pad_end - padded
    block_start = jnp.arange(nblk, dtype=I32)[:, None] * MOE_ROWS
    block_e = jnp.minimum(jnp.sum(pad_end[None, :] <= block_start, axis=1), ne - 1).astype(I32)
    n_used = (pad_end[-1:] // MOE_ROWS).astype(I32)
    dest = _dest_rows(pad_start.astype(I32), eidx, rank)
    xs = _sc_scatter_rows([h.reshape(n, dh) for h in h2], [dest[k:k + 1] for k in range(TOP_K)], n_rows)
    ys = _moe_blocks(block_e, n_used, xs, layer, w1, b1, w2, b2)
    yg = _sc_gather_rows(ys, dest.reshape(1, TOP_K * n))
    return yg, gate.T


def _hgrn_lower_bounds(lb_logits):
    p = jax.nn.softmax(lb_logits.astype(F32), axis=0)
    cum = jnp.cumsum(p, axis=0)
    return cum - cum[0:1]


def kernel(x, c, positions, ada_w, ada_b, norm1_g, norm2_g, hgrn_w_in, hgrn_w_out, hgrn_o_gain, hgrn_lb_logits, ret_w_in, ret_w_out, router_w, router_b, moe_w1, moe_b1, moe_w2, moe_b2, final_g, final_ada_w, final_ada_b):
    depth = ada_w.shape[0]
    mods = _ada(c, ada_w, ada_b)
    fmod = _ada(c, final_ada_w[None], final_ada_b[None])[0][:, None, :]
    lbs = _hgrn_lower_bounds(hgrn_lb_logits)
    cos, sin = _rope_tables(positions)
    bsz = x.shape[0]
    n_groups = BATCH_GROUPS if bsz % BATCH_GROUPS == 0 else 1
    gb = bsz // n_groups
    groups = [slice(g * gb, (g + 1) * gb) for g in range(n_groups)]
    xs = [x[g] for g in groups]
    pending = [None] * n_groups
    w_in = [hgrn_w_in.astype(BF16), ret_w_in.astype(BF16)]
    w_out = [hgrn_w_out.astype(BF16), ret_w_out.astype(BF16)]
    for layer in range(depth):
        j = layer // N_MIXERS
        mixer = layer % N_MIXERS
        for gi, g in enumerate(groups):
            mod = mods[layer][g][:, None, :]
            xg, proj = _inproj(xs[gi], pending[gi], norm1_g[layer][None], mod, w_in[mixer][j])
            if mixer == 0:
                o = _hgrn(proj, lbs[j][None], hgrn_o_gain[j][None])
            else:
                o = _retention(proj, cos[g], sin[g])
            xg, ha, hb, eidx, gate, rank, counts = _outproj_route(
                o, w_out[mixer][j], xg, mod, norm2_g[layer][None],
                router_w[layer].T, router_b[layer][:, None])
            yg, gate_rows = _moe_layer((ha, hb), eidx, gate, rank, counts, layer,
                                       moe_w1, moe_b1, moe_w2, moe_b2)
            xs[gi] = xg
            pending[gi] = (yg, gate_rows, mod)
    outs = [_final(xs[gi], pending[gi], final_g[None], fmod[g]) for gi, g in enumerate(groups)]
    return outs[0] if n_groups == 1 else jnp.concatenate(outs, axis=0)
```

```python
import functools

import jax
import jax.numpy as jnp
from jax import lax
from jax.experimental import pallas as pl
from jax.experimental.pallas import tpu as pltpu
from jax.experimental.pallas import tpu_sc as plsc

F32 = jnp.float32
BF16 = jnp.bfloat16
I32 = jnp.int32
HIGHEST = lax.Precision.HIGHEST

EPS = 1e-6
N_MIXERS = 2
HG_DK = 128
HG_T = 64
HG_PAIR = 2
HG_SUB = 16
HG_MAX_HALF_RANGE = 80.0
RET_DK = 256
RET_DV = 512
RET_CHUNK = 64
RET_T = 256
ROPE_BASE = 10000.0
N_EXPERTS = 32
TOP_K = 4
SWIGLU_ALPHA = 1.702
SWIGLU_LIMIT = 7.0
MOE_ROWS = 512
MOE_SUB_ROWS = 256
WEIGHT_CAST_ROWS = 128
ROW_TILE = 512
INPROJ_SUB_ROWS = 256
OUTPROJ_SUB_ROWS = 512
SC_WINDOW = 128
ROW_PARTS = 2
V7X_VMEM_LIMIT = 56 * 1024 * 1024


def _dot(a, b):
    return jnp.dot(a, b, preferred_element_type=F32)


def _dot_nt(a, b, precision=None):
    return lax.dot_general(a, b, (((1,), (1,)), ((), ())), precision=precision,
                           preferred_element_type=F32)


def _dot_tn(a, b):
    return lax.dot_general(a, b, (((0,), (0,)), ((), ())), preferred_element_type=F32)


def _rms(x):
    return x * lax.rsqrt(jnp.mean(x * x, axis=-1, keepdims=True) + EPS)


def _sigmoid(x):
    return 1.0 / (1.0 + jnp.exp(-x))


def _pack_rows(h):
    half = h.shape[1] // 2
    a = lax.bitcast_convert_type(h[:, :half].astype(BF16).astype(F32), jnp.uint32)
    b = lax.bitcast_convert_type(h[:, half:].astype(BF16).astype(F32), jnp.uint32)
    return lax.bitcast_convert_type(a | (b >> 16), I32)


def _unpack_rows(w):
    u = lax.bitcast_convert_type(w, jnp.uint32)
    a = lax.bitcast_convert_type(u & jnp.uint32(0xFFFF0000), F32)
    b = lax.bitcast_convert_type(u << 16, F32)
    return jnp.concatenate([a, b], axis=1)


def _ada_kernel(c_ref, w_ref, b_ref, o_ref):
    c = c_ref[...]
    cond = c * _sigmoid(c)
    o_ref[0] = jnp.dot(cond, w_ref[0], precision=HIGHEST, preferred_element_type=F32) + b_ref[0]


def _ada(c, w, b):
    nl, d, kd = w.shape
    bsz = c.shape[0]
    return pl.pallas_call(
        _ada_kernel,
        grid=(nl, kd // d),
        in_specs=[
            pl.BlockSpec((bsz, d), lambda l, j: (0, 0)),
            pl.BlockSpec((1, d, d), lambda l, j: (l, 0, j)),
            pl.BlockSpec((1, 1, d), lambda l, j: (l, 0, j)),
        ],
        out_specs=pl.BlockSpec((1, bsz, d), lambda l, j: (l, 0, j)),
        out_shape=jax.ShapeDtypeStruct((nl, bsz, kd), F32),
        name="ada_mod",
    )(c, w, b.reshape(nl, 1, kd))


def _moe_residual(x_ref, refs, rows=slice(None)):
    ya_refs, yb_refs = refs[:TOP_K], refs[TOP_K:2 * TOP_K]
    gate_ref, g2_ref = refs[2 * TOP_K:]
    gate = gate_ref[rows, :]
    acc = None
    for k in range(TOP_K):
        packed = jnp.concatenate([ya_refs[k][rows, :], yb_refs[k][rows, :]], axis=1)
        term = gate[:, k:k + 1] * _unpack_rows(packed)
        acc = term if acc is None else acc + term
    return x_ref[0, rows, :] + g2_ref[0] * acc


N_MOE_REFS = 2 * TOP_K + 2


def _moe_residual_operands(pending, bsz, s, d, tm):
    yg, gate_rows, mod = pending
    nt = s // tm
    ntok = (bsz * s) // tm

    def yspec(k):
        return pl.BlockSpec((tm, d // 4), lambda b, i: (k * ntok + b * nt + i, 0))

    specs = [yspec(k) for k in range(TOP_K)] * 2 + [
        pl.BlockSpec((tm, TOP_K), lambda b, i: (b * nt + i, 0)),
        pl.BlockSpec((1, 1, d), lambda b, i: (b, 0, 5))]
    return specs, [yg[0]] * TOP_K + [yg[1]] * TOP_K + [gate_rows, mod]


def _inproj_kernel(x_ref, *refs, col_chunk, fused):
    if fused:
        g_ref, sh_ref, sc_ref, w_ref, o_ref, xo_ref = refs[N_MOE_REFS:]
    else:
        g_ref, sh_ref, sc_ref, w_ref, o_ref = refs
    nout = w_ref.shape[1]
    tm = x_ref.shape[1]
    sub = min(INPROJ_SUB_ROWS, tm)
    for r in range(0, tm, sub):
        rows = slice(r, r + sub)
        if fused:
            x = _moe_residual(x_ref, refs[:N_MOE_REFS], rows)
            xo_ref[0, rows, :] = x
        else:
            x = x_ref[0, rows, :]
        h = _rms(x) * g_ref[...] * (1.0 + sc_ref[0]) + sh_ref[0]
        hb = h.astype(BF16)
        for j in range(nout // col_chunk):
            cs = slice(j * col_chunk, (j + 1) * col_chunk)
            o_ref[0, rows, cs] = _dot(hb, w_ref[:, cs]).astype(BF16)


def _inproj(x, pending, gain, mod, w):
    bsz, s, d = x.shape
    nout = w.shape[1]
    tm = min(ROW_TILE, s)
    xspec = pl.BlockSpec((1, tm, d), lambda b, i: (b, i, 0))
    fused = pending is not None
    moe_specs, moe_args = _moe_residual_operands(pending, bsz, s, d, tm) if fused else ([], [])
    proj_spec = pl.BlockSpec((1, tm, nout), lambda b, i: (b, i, 0))
    proj_shape = jax.ShapeDtypeStruct((bsz, s, nout), BF16)
    out = pl.pallas_call(
        functools.partial(_inproj_kernel, col_chunk=1024, fused=fused),
        grid=(bsz, s // tm),
        in_specs=[xspec] + moe_specs + [
            pl.BlockSpec((1, d), lambda b, i: (0, 0)),
            pl.BlockSpec((1, 1, d), lambda b, i: (b, 0, 0)),
            pl.BlockSpec((1, 1, d), lambda b, i: (b, 0, 1)),
            pl.BlockSpec((d, nout), lambda b, i: (0, 0), pipeline_mode=pl.Buffered(1)),
        ],
        out_specs=[proj_spec, xspec] if fused else proj_spec,
        out_shape=[proj_shape, jax.ShapeDtypeStruct((bsz, s, d), F32)] if fused else proj_shape,
        compiler_params=pltpu.CompilerParams(
            dimension_semantics=("parallel", "parallel"), vmem_limit_bytes=V7X_VMEM_LIMIT),
        name="inproj",
    )(x, *moe_args, gain, mod, mod, w)
    return (out[1], out[0]) if fused else (x, out)


def _hgrn_gates(q, f, lb, one_m_lb):
    e = jnp.exp(-jnp.abs(f))
    inv = 1.0 / (1.0 + e)
    pos = f >= 0.0
    t = e * inv
    sig = jnp.where(pos, inv, t)
    sig_neg = jnp.where(pos, t, inv)
    has_lb = lb > 0.0
    logf = jnp.log(jnp.where(has_lb, lb + one_m_lb * sig, inv)) + jnp.where(has_lb, 0.0, jnp.minimum(f, 0.0))
    return q * _sigmoid(q), one_m_lb * sig_neg, logf


def _hgrn_kernel(q_ref, f_ref, i_ref, g_ref, lb_ref, gain_ref, o_ref, kk_s, v_s, b_s):
    t = HG_T
    dk = HG_DK
    w = q_ref.shape[2]
    n_chunks = q_ref.shape[1] // t
    lb = lb_ref[...]
    one_m_lb = 1.0 - lb
    gain = gain_ref[...]
    heads = [slice(h * dk, (h + 1) * dk) for h in range(HG_PAIR)]
    row = lax.broadcasted_iota(I32, (t, HG_PAIR * t), 0)
    col = lax.broadcasted_iota(I32, (t, HG_PAIR * t), 1)
    causal = row >= (col % t)
    tril = causal[:, :t].astype(BF16)

    def finish(o, g):
        return (_rms(o) * gain * (g * _sigmoid(g))).astype(BF16)

    def block_diag(parts):
        rows = []
        for h, p in enumerate(parts):
            z = jnp.zeros_like(p)
            rows.append(jnp.concatenate([p if j == h else z for j in range(HG_PAIR)], axis=1))
        return jnp.concatenate(rows, axis=0)

    def chunk(c, carry):
        sts, bmax = carry
        rows = pl.ds(pl.multiple_of(c * t, t), t)
        q = q_ref[0, rows, :].astype(F32)
        f = f_ref[0, rows, :].astype(F32)
        v = i_ref[0, rows, :]
        g = g_ref[0, rows, :].astype(F32)
        qs, kk, logf = _hgrn_gates(q, f, lb, one_m_lb)
        hi = logf.astype(BF16)
        lo = (logf - hi.astype(F32)).astype(BF16)
        bb = _dot(tril, jnp.concatenate([hi, lo], axis=-1))
        b = bb[:, :w] + bb[:, w:]
        b_last = b[t - 1:t, :]
        mid = 0.5 * b_last
        e_mid = jnp.exp(mid)
        qt = qs * jnp.exp(b - mid)
        kt = kk * jnp.exp(mid - b)
        ktb = kt.astype(BF16)
        att = _dot_nt(qt.astype(BF16), block_diag([ktb[:, hs] for hs in heads]))
        att = jnp.where(causal, att, 0.0).astype(BF16)
        o = _dot(att, block_diag([v[:, hs] for hs in heads]))
        st_bd = block_diag([st.astype(BF16) for st in sts])
        o = o + _dot_nt((qt * e_mid).astype(BF16), st_bd)
        kd = (kt * e_mid).astype(BF16)
        decay = e_mid * e_mid
        sts = tuple(st * decay[:, hs] + _dot_tn(v[:, hs], kd[:, hs]) for st, hs in zip(sts, heads))
        out = [finish(o[:, hs], g[:, hs]) for hs in heads]
        o_ref[0, rows, :] = jnp.concatenate(out, axis=1)
        return sts, jnp.maximum(bmax, jnp.abs(b_last))

    st0 = jnp.zeros((dk, dk), F32)
    _, bmax = lax.fori_loop(0, n_chunks, chunk, ((st0,) * HG_PAIR, jnp.zeros_like(lb)), unroll=8)
    safe = 0.5 * jnp.max(bmax) <= HG_MAX_HALF_RANGE

    @pl.when(jnp.logical_not(safe))
    def _():
        n = HG_SUB
        sub_row = lax.broadcasted_iota(I32, (n, 1), 0)
        tril_n = (lax.broadcasted_iota(I32, (n, n), 0) >= lax.broadcasted_iota(I32, (n, n), 1)).astype(F32)

        for hs in heads:
            def block(i, st, hs=hs):
                rows = pl.ds(pl.multiple_of(i * n, n), n)
                q = q_ref[0, rows, hs].astype(F32)
                f = f_ref[0, rows, hs].astype(F32)
                v = i_ref[0, rows, hs]
                g = g_ref[0, rows, hs].astype(F32)
                qs, kk, logf = _hgrn_gates(q, f, lb[:, hs], one_m_lb[:, hs])
                b = jnp.dot(tril_n, logf, precision=HIGHEST, preferred_element_type=F32)
                kk_s[...] = kk
                v_s[...] = v.astype(F32)
                b_s[...] = b
                o = _dot_nt((qs * jnp.exp(b)).astype(BF16), st.astype(BF16))

                def pair(s, acc):
                    dec = jnp.exp(jnp.minimum(b - b_s[pl.ds(s, 1), :], 0.0))
                    wgt = jnp.sum(qs * kk_s[pl.ds(s, 1), :] * dec, axis=-1, keepdims=True)
                    return acc + jnp.where(sub_row >= s, wgt, 0.0) * v_s[pl.ds(s, 1), :]

                o = lax.fori_loop(0, n, pair, o)
                b_last = b[n - 1:n, :]
                st = st * jnp.exp(b_last) + _dot_tn(v, (kk * jnp.exp(b_last - b)).astype(BF16))
                o_ref[0, rows, hs] = finish(o, g)
                return st

            lax.fori_loop(0, q_ref.shape[1] // n, block, st0)


def _hgrn(proj, lb, gain):
    bsz, s, w4 = proj.shape
    w = HG_PAIR * HG_DK
    npair = w4 // (4 * w)

    def spec(j):
        return pl.BlockSpec((1, s, w), lambda b, p: (b, 0, p + j * npair))

    return pl.pallas_call(
        _hgrn_kernel,
        grid=(bsz, npair),
        in_specs=[spec(0), spec(1), spec(2), spec(3),
                  pl.BlockSpec((1, w), lambda b, p: (0, p)),
                  pl.BlockSpec((1, HG_DK), lambda b, p: (0, 0))],
        out_specs=pl.BlockSpec((1, s, w), lambda b, p: (b, 0, p)),
        out_shape=jax.ShapeDtypeStruct((bsz, s, npair * w), BF16),
        scratch_shapes=[pltpu.VMEM((HG_SUB, HG_DK), F32)] * 3,
        compiler_params=pltpu.CompilerParams(dimension_semantics=("parallel", "parallel")),
        name="hgrn",
    )(proj, proj, proj, proj, lb, gain)


def _rope_kernel(pos_ref, inv_ref, cos_ref, sin_ref):
    ang = pos_ref[0].astype(F32) * inv_ref[...]
    cos_ref[0] = jnp.cos(ang)
    sin_ref[0] = jnp.sin(ang)


def _rope_tables(positions):
    bsz, s = positions.shape
    half = RET_DK // 2
    inv_freq = (1.0 / (ROPE_BASE ** jnp.linspace(0.0, 1.0, half, dtype=F32))).reshape(1, half)
    out = jax.ShapeDtypeStruct((bsz, s, half), F32)
    return pl.pallas_call(
        _rope_kernel,
        grid=(bsz,),
        in_specs=[pl.BlockSpec((1, s, 1), lambda b: (b, 0, 0)),
                  pl.BlockSpec((1, half), lambda b: (0, 0))],
        out_specs=[pl.BlockSpec((1, s, half), lambda b: (b, 0, 0))] * 2,
        out_shape=[out, out],
        name="rope_tables",
    )(positions.reshape(bsz, s, 1), inv_freq)


def _ret_kernel(q_ref, k_ref, v_ref, g_ref, cos_ref, sin_ref, lg_ref, o_ref, r_ref, d_ref):
    t = RET_T
    dk = RET_DK
    half = dk // 2
    n_steps = q_ref.shape[1] // t
    lg = lg_ref[0]
    lg_k = lg[:, :dk]
    n = lax.broadcasted_iota(I32, (t, t), 0)
    m = lax.broadcasted_iota(I32, (t, t), 1)
    dist = jnp.abs(n - m).astype(F32)
    visible = (m // RET_CHUNK) <= (n // RET_CHUNK)
    d_ref[...] = jnp.where(visible, jnp.exp(dist * lg[:, :t]), 0.0)
    idx = lax.broadcasted_iota(I32, (t, dk), 0).astype(F32)
    q_decay = jnp.exp((idx + 1.0) * lg_k)
    k_decay = jnp.exp((t - 1.0 - idx) * lg_k)
    step_decay = jnp.exp(float(t) * lg)
    r_ref[...] = jnp.zeros_like(r_ref)

    def rotate(x, cos, sin):
        x1, x2 = x[:, :half], x[:, half:]
        return jnp.concatenate([x1 * cos - x2 * sin, x2 * cos + x1 * sin], axis=-1)

    def step(c, carry):
        r0 = pl.multiple_of(c * t, t)
        rows = pl.ds(r0, t)
        cos = cos_ref[0, rows, :]
        sin = sin_ref[0, rows, :]
        q = rotate(q_ref[0, rows, :].astype(F32), cos, sin)
        k = rotate(k_ref[0, rows, :].astype(F32), cos, sin) * (dk ** -0.5)
        v = v_ref[0, rows, :]
        g = g_ref[0, rows, :].astype(F32)
        r = r_ref[...]
        sc = _dot_nt(q.astype(BF16), k.astype(BF16)) * d_ref[...]
        o = _dot(sc.astype(BF16), v) + _dot((q * q_decay).astype(BF16), r.astype(BF16))
        r_ref[...] = r * step_decay + _dot_tn((k * k_decay).astype(BF16), v)
        out = _rms(o) * (g * _sigmoid(g))
        o_ref[0, rows, :] = out.astype(BF16)
        return carry

    lax.fori_loop(0, n_steps, step, 0)


def _retention(proj, cos, sin):
    bsz, s, w = proj.shape
    dk, dv = RET_DK, RET_DV
    nh = w // (2 * dk + 2 * dv)
    hidx = jnp.arange(nh, dtype=F32)
    log_gamma = jnp.log(1.0 - 2.0 ** (-5.0 - hidx))
    lg = jnp.broadcast_to(log_gamma[:, None, None], (nh, 1, dv))
    vbase = 2 * nh * dk // dv
    return pl.pallas_call(
        _ret_kernel,
        grid=(bsz, nh),
        in_specs=[
            pl.BlockSpec((1, s, dk), lambda b, h: (b, 0, h)),
            pl.BlockSpec((1, s, dk), lambda b, h: (b, 0, nh + h)),
            pl.BlockSpec((1, s, dv), lambda b, h: (b, 0, vbase + h)),
            pl.BlockSpec((1, s, dv), lambda b, h: (b, 0, vbase + nh + h)),
            pl.BlockSpec((1, s, dk // 2), lambda b, h: (b, 0, 0)),
            pl.BlockSpec((1, s, dk // 2), lambda b, h: (b, 0, 0)),
            pl.BlockSpec((1, 1, dv), lambda b, h: (h, 0, 0)),
        ],
        out_specs=pl.BlockSpec((1, s, dv), lambda b, h: (b, 0, h)),
        out_shape=jax.ShapeDtypeStruct((bsz, s, nh * dv), BF16),
        scratch_shapes=[pltpu.VMEM((dk, dv), F32), pltpu.VMEM((RET_T, RET_T), F32)],
        compiler_params=pltpu.CompilerParams(
            dimension_semantics=("parallel", "parallel"), vmem_limit_bytes=V7X_VMEM_LIMIT),
        name="retention",
    )(proj, proj, proj, proj, cos, sin, lg)


def _outproj_kernel(o_ref, w_ref, x_ref, g1_ref, sh_ref, sc_ref, gain_ref, rw_ref, rb_ref, u_ref,
                    xo_ref, ha_ref, hb_ref, eidx_ref, gate_ref, rank_ref, cnt_ref, base_ref):
    first = jnp.logical_and(pl.program_id(0) == 0, pl.program_id(1) == 0)

    @pl.when(first)
    def _():
        base_ref[...] = jnp.zeros_like(base_ref)

    ne = rw_ref.shape[0]
    rw = rw_ref[...]
    rw_hi = rw.astype(BF16)
    rw_lo = (rw - rw_hi.astype(F32)).astype(BF16)
    rw_both = jnp.concatenate([rw_hi, rw_lo], axis=0)
    sub = u_ref.shape[0]
    eiota = lax.broadcasted_iota(I32, (ne, sub), 0)
    base = base_ref[...]
    for r in range(0, x_ref.shape[1], sub):
        rows = slice(r, r + sub)
        y = _dot(o_ref[0, rows, :], w_ref[...])
        xn = x_ref[0, rows, :] + g1_ref[0] * y
        xo_ref[0, rows, :] = xn
        h = _rms(xn) * gain_ref[...] * (1.0 + sc_ref[0]) + sh_ref[0]
        packed = _pack_rows(h)
        slab = packed.shape[1] // ROW_PARTS
        ha_ref[0, rows, :] = packed[:, :slab]
        hb_ref[0, rows, :] = packed[:, slab:]

        h_hi = h.astype(BF16)
        h_lo = (h - h_hi.astype(F32)).astype(BF16)
        part = _dot_nt(rw_both, h_hi)
        work = part[:ne] + part[ne:] + _dot_nt(rw_hi, h_lo) + rb_ref[...]
        onehots, tops = [], []
        for k in range(TOP_K):
            mx = jnp.max(work, axis=0, keepdims=True)
            idx = jnp.min(jnp.where(work == mx, eiota, ne), axis=0, keepdims=True)
            oh = eiota == idx
            work = jnp.where(oh, -jnp.inf, work)
            eidx_ref[k:k + 1, rows] = idx
            onehots.append(oh)
            tops.append(mx)
        ex = [jnp.exp(m - tops[0]) for m in tops]
        denom = ex[0] + ex[1] + ex[2] + ex[3]
        for k in range(TOP_K):
            gate_ref[k:k + 1, rows] = ex[k] / denom

        mask = jnp.zeros((ne, sub), F32)
        for oh in onehots:
            mask = mask + oh.astype(F32)
        incl = _dot(mask.astype(BF16), u_ref[...])
        excl = incl - mask + base
        for k in range(TOP_K):
            rk = jnp.sum(jnp.where(onehots[k], excl, 0.0), axis=0, keepdims=True)
            rank_ref[k:k + 1, rows] = rk.astype(I32)
        base = base + incl[:, sub - 1:sub]
    base_ref[...] = base
    cnt_ref[...] = base.astype(I32)


def _outproj_route(o, w_out, x, mod, gain2, router_wt, router_b):
    bsz, s, d = x.shape
    hv = o.shape[2]
    tm = min(ROW_TILE, s)
    n = bsz * s
    nt = s // tm
    ne = router_wt.shape[0]
    sub = min(OUTPROJ_SUB_ROWS, tm)
    upper = (jnp.arange(sub)[:, None] <= jnp.arange(sub)[None, :]).astype(BF16)

    def modspec(j):
        return pl.BlockSpec((1, 1, d), lambda b, i: (b, 0, j))

    tokspec = pl.BlockSpec((TOP_K, tm), lambda b, i: (0, b * nt + i))
    return pl.pallas_call(
        _outproj_kernel,
        grid=(bsz, nt),
        in_specs=[
            pl.BlockSpec((1, tm, hv), lambda b, i: (b, i, 0)),
            pl.BlockSpec((hv, d), lambda b, i: (0, 0)),
            pl.BlockSpec((1, tm, d), lambda b, i: (b, i, 0)),
            modspec(2), modspec(3), modspec(4),
            pl.BlockSpec((1, d), lambda b, i: (0, 0)),
            pl.BlockSpec((ne, d), lambda b, i: (0, 0)),
            pl.BlockSpec((ne, 1), lambda b, i: (0, 0)),
            pl.BlockSpec((sub, sub), lambda b, i: (0, 0)),
        ],
        out_specs=[
            pl.BlockSpec((1, tm, d), lambda b, i: (b, i, 0)),
            pl.BlockSpec((1, tm, d // 4), lambda b, i: (b, i, 0)),
            pl.BlockSpec((1, tm, d // 4), lambda b, i: (b, i, 0)),
            tokspec, tokspec, tokspec,
            pl.BlockSpec((ne, 1), lambda b, i: (0, 0)),
        ],
        out_shape=[
            jax.ShapeDtypeStruct((bsz, s, d), F32),
            jax.ShapeDtypeStruct((bsz, s, d // 4), I32),
            jax.ShapeDtypeStruct((bsz, s, d // 4), I32),
            jax.ShapeDtypeStruct((TOP_K, n), I32),
            jax.ShapeDtypeStruct((TOP_K, n), F32),
            jax.ShapeDtypeStruct((TOP_K, n), I32),
            jax.ShapeDtypeStruct((ne, 1), I32),
        ],
        scratch_shapes=[pltpu.VMEM((ne, 1), F32)],
        compiler_params=pltpu.CompilerParams(
            dimension_semantics=("arbitrary", "arbitrary"), vmem_limit_bytes=V7X_VMEM_LIMIT),
        name="outproj_route",
    )(o, w_out, x, mod, mod, mod, gain2, router_wt, router_b, upper)


def _moe_kernel(be_ref, nb_ref, xa_ref, xb_ref, w1_ref, b1_ref, w2_ref, b2_ref, ya_ref, yb_ref,
                w1_s, w2_s):
    i = pl.program_id(0)
    used = i < nb_ref[0]
    new_expert = jnp.logical_or(i == 0, be_ref[i] != be_ref[jnp.maximum(i - 1, 0)])

    @pl.when(jnp.logical_and(used, new_expert))
    def _():
        for w_ref, w_s in ((w1_ref, w1_s), (w2_ref, w2_s)):
            for r in range(0, w_s.shape[0], WEIGHT_CAST_ROWS):
                rows = slice(r, r + WEIGHT_CAST_ROWS)
                w_s[rows, :] = w_ref[0, 0, rows, :].astype(BF16)

    @pl.when(used)
    def _():
        f = w2_s.shape[0]
        for r in range(0, xa_ref.shape[0], MOE_SUB_ROWS):
            rows = slice(r, r + MOE_SUB_ROWS)
            x = _unpack_rows(jnp.concatenate([xa_ref[rows, :], xb_ref[rows, :]], axis=1)).astype(BF16)
            u = _dot(x, w1_s[...]) + b1_ref[0, 0]
            glu = jnp.minimum(u[:, :f], SWIGLU_LIMIT)
            lin = jnp.clip(u[:, f:], -SWIGLU_LIMIT, SWIGLU_LIMIT)
            a = glu * _sigmoid(SWIGLU_ALPHA * glu) * (lin + 1.0)
            y = _dot(a.astype(BF16), w2_s[...]) + b2_ref[0, 0]
            packed = _pack_rows(y)
            slab = packed.shape[1] // ROW_PARTS
            ya_ref[rows, :] = packed[:, :slab]
            yb_ref[rows, :] = packed[:, slab:]

    @pl.when(jnp.logical_not(used))
    def _():
        ya_ref[...] = jnp.zeros_like(ya_ref)
        yb_ref[...] = jnp.zeros_like(yb_ref)


def _moe_blocks(block_e, n_used, xs, layer, w1, b1, w2, b2):
    n_rows, dh = xs[0].shape
    nl, ne, d, f2 = w1.shape
    f = f2 // 2
    nblk = n_rows // MOE_ROWS
    grid_spec = pltpu.PrefetchScalarGridSpec(
        num_scalar_prefetch=2,
        grid=(nblk,),
        in_specs=[
            pl.BlockSpec((MOE_ROWS, dh), lambda i, be, nb: (i, 0)),
            pl.BlockSpec((MOE_ROWS, dh), lambda i, be, nb: (i, 0)),
            pl.BlockSpec((1, 1, d, f2), lambda i, be, nb: (layer, be[i], 0, 0)),
            pl.BlockSpec((1, 1, 1, f2), lambda i, be, nb: (layer, be[i], 0, 0)),
            pl.BlockSpec((1, 1, f, d), lambda i, be, nb: (layer, be[i], 0, 0)),
            pl.BlockSpec((1, 1, 1, d), lambda i, be, nb: (layer, be[i], 0, 0)),
        ],
        out_specs=[pl.BlockSpec((MOE_ROWS, dh), lambda i, be, nb: (i, 0))] * 2,
        scratch_shapes=[pltpu.VMEM((d, f2), BF16), pltpu.VMEM((f, d), BF16)],
    )
    return pl.pallas_call(
        _moe_kernel,
        grid_spec=grid_spec,
        out_shape=[jax.ShapeDtypeStruct((n_rows, dh), I32)] * 2,
        compiler_params=pltpu.CompilerParams(
            dimension_semantics=("arbitrary",), vmem_limit_bytes=V7X_VMEM_LIMIT),
        name="moe_experts",
    )(block_e, n_used, xs[0], xs[1], w1, b1.reshape(nl, ne, 1, f2), w2, b2.reshape(nl, ne, 1, d))


def _sc_mesh():
    return plsc.VectorSubcoreMesh(core_axis_name="c", subcore_axis_name="s")


def _sc_scatter_rows(srcs, dests, n_rows):
    n, w = srcs[0].shape
    ns, nk = len(srcs), len(dests)
    out = jax.ShapeDtypeStruct((n_rows, w), srcs[0].dtype)

    @functools.partial(pl.kernel, out_type=[out] * ns, mesh=_sc_mesh(), scratch_types=[])
    def scatter_kernel(*refs):
        x_hbm, idx_hbm, o_hbm = refs[:ns], refs[ns:ns + nk], refs[ns + nk:]
        for x, o in zip(x_hbm, o_hbm):
            def body(x_vmem, *idx_vmem, o=o):
                for iv in idx_vmem:
                    pltpu.sync_copy(x_vmem, o.at[iv.at[0]])

            pltpu.emit_pipeline(
                body,
                grid=(n // SC_WINDOW,),
                in_specs=[pl.BlockSpec((SC_WINDOW, w), lambda i: (i, 0))]
                + [pl.BlockSpec((1, SC_WINDOW), lambda i: (0, i))] * nk,
                out_specs=[],
                core_axis_name=("c", "s"),
                dimension_semantics=(pltpu.PARALLEL,),
            )(x, *idx_hbm)

    return scatter_kernel(*srcs, *dests)


def _sc_gather_rows(tables, idx):
    m = idx.shape[1]
    w = tables[0].shape[1]
    nt = len(tables)
    out = jax.ShapeDtypeStruct((m, w), tables[0].dtype)

    @functools.partial(pl.kernel, out_type=[out] * nt, mesh=_sc_mesh(), scratch_types=[])
    def gather_kernel(*refs):
        t_hbm, i_hbm, o_hbm = refs[:nt], refs[nt], refs[nt + 1:]
        for t, o in zip(t_hbm, o_hbm):
            def body(i_vmem, o_vmem, t=t):
                pltpu.sync_copy(t.at[i_vmem.at[0]], o_vmem)

            pltpu.emit_pipeline(
                body,
                grid=(m // SC_WINDOW,),
                in_specs=[pl.BlockSpec((1, SC_WINDOW), lambda i: (0, i))],
                out_specs=[pl.BlockSpec((SC_WINDOW, w), lambda i: (i, 0))],
                core_axis_name=("c", "s"),
                dimension_semantics=(pltpu.PARALLEL,),
            )(i_hbm, o)

    return gather_kernel(*tables, idx)


def _dest_kernel(ps_ref, eidx_ref, rank_ref, o_ref):
    eidx = eidx_ref[...]
    start = jnp.zeros_like(eidx)
    for e in range(N_EXPERTS):
        start = jnp.where(eidx == e, ps_ref[e], start)
    o_ref[...] = start + rank_ref[...]


def _dest_rows(pad_start, eidx, rank):
    k, n = eidx.shape
    tn = min(8192, n)
    blk = pl.BlockSpec((k, tn), lambda i, ps: (0, i))
    return pl.pallas_call(
        _dest_kernel,
        grid_spec=pltpu.PrefetchScalarGridSpec(
            num_scalar_prefetch=1, grid=(n // tn,), in_specs=[blk, blk], out_specs=blk),
        out_shape=jax.ShapeDtypeStruct((k, n), I32),
        name="dest_rows",
    )(pad_start, eidx, rank)


def _final_kernel(x_ref, *refs):
    g_ref, sh_ref, sc_ref, o_ref = refs[N_MOE_REFS:]
    x = _moe_residual(x_ref, refs[:N_MOE_REFS])
    o_ref[0] = _rms(x) * g_ref[...] * (1.0 + sc_ref[0]) + sh_ref[0]


def _final(x, pending, gain, fmod):
    bsz, s, d = x.shape
    tm = min(ROW_TILE, s)
    blk = pl.BlockSpec((1, tm, d), lambda b, i: (b, i, 0))
    moe_specs, moe_args = _moe_residual_operands(pending, bsz, s, d, tm)
    return pl.pallas_call(
        _final_kernel,
        grid=(bsz, s // tm),
        in_specs=[blk] + moe_specs + [
            pl.BlockSpec((1, d), lambda b, i: (0, 0)),
            pl.BlockSpec((1, 1, d), lambda b, i: (b, 0, 0)),
            pl.BlockSpec((1, 1, d), lambda b, i: (b, 0, 1))],
        out_specs=blk,
        out_shape=jax.ShapeDtypeStruct((bsz, s, d), F32),
        compiler_params=pltpu.CompilerParams(dimension_semantics=("parallel", "parallel")),
        name="final_norm",
    )(x, *moe_args, gain, fmod, fmod)


def _moe_layer(h2, eidx, gate, rank, counts, layer, w1, b1, w2, b2):
    bsz, s, dh = h2[0].shape
    n = bsz * s
    ne = w1.shape[1]
    nblk = -(-(n * TOP_K) // MOE_ROWS) + ne
    n_rows = nblk * MOE_ROWS
    counts = counts[:, 0]
    padded = (counts + MOE_ROWS - 1) // MOE_ROWS * MOE_ROWS
    pad_end = jnp.cumsum(padded)
    pad_start = pad_end - padded
    block_start = jnp.arange(nblk, dtype=I32)[:, None] * MOE_ROWS
    block_e = jnp.minimum(jnp.sum(pad_end[None, :] <= block_start, axis=1), ne - 1).astype(I32)
    n_used = (pad_end[-1:] // MOE_ROWS).astype(I32)
    dest = _dest_rows(pad_start.astype(I32), eidx, rank)
    xs = _sc_scatter_rows([h.reshape(n, dh) for h in h2], [dest[k:k + 1] for k in range(TOP_K)], n_rows)
    ys = _moe_blocks(block_e, n_used, xs, layer, w1, b1, w2, b2)
    yg = _sc_gather_rows(ys, dest.reshape(1, TOP_K * n))
    return yg, gate.T


def _hgrn_lower_bounds(lb_logits):
    p = jax.nn.softmax(lb_logits.astype(F32), axis=0)
    cum = jnp.cumsum(p, axis=0)
    return cum - cum[0:1]


def kernel(x, c, positions, ada_w, ada_b, norm1_g, norm2_g, hgrn_w_in, hgrn_w_out, hgrn_o_gain, hgrn_lb_logits, ret_w_in, ret_w_out, router_w, router_b, moe_w1, moe_b1, moe_w2, moe_b2, final_g, final_ada_w, final_ada_b):
    depth = ada_w.shape[0]
    mods = _ada(c, ada_w, ada_b)
    fmod = _ada(c, final_ada_w[None], final_ada_b[None])[0][:, None, :]
    lbs = _hgrn_lower_bounds(hgrn_lb_logits)
    cos, sin = _rope_tables(positions)
    pending = None
    for layer in range(depth):
        mod = mods[layer][:, None, :]
        j = layer // N_MIXERS
        if layer % N_MIXERS == 0:
            x, proj = _inproj(x, pending, norm1_g[layer][None], mod, hgrn_w_in[j].astype(BF16))
            o = _hgrn(proj, lbs[j][None], hgrn_o_gain[j][None])
            w_out = hgrn_w_out[j]
        else:
            x, proj = _inproj(x, pending, norm1_g[layer][None], mod, ret_w_in[j].astype(BF16))
            o = _retention(proj, cos, sin)
            w_out = ret_w_out[j]
        x, ha, hb, eidx, gate, rank, counts = _outproj_route(
            o, w_out.astype(BF16), x, mod, norm2_g[layer][None],
            router_w[layer].T, router_b[layer][:, None])
        yg, gate_rows = _moe_layer((ha, hb), eidx, gate, rank, counts, layer, moe_w1, moe_b1, moe_w2, moe_b2)
        pending = (yg, gate_rows, mod)
    return _final(x, pending, final_g[None], fmod)
```

```python
import functools

import jax
import jax.numpy as jnp
from jax import lax
from jax.experimental import pallas as pl
from jax.experimental.pallas import tpu as pltpu
from jax.experimental.pallas import tpu_sc as plsc

F32 = jnp.float32
BF16 = jnp.bfloat16
I32 = jnp.int32
HIGHEST = lax.Precision.HIGHEST

EPS = 1e-6
N_MIXERS = 2
HG_DK = 128
HG_T = 64
HG_PAIR = 2
HG_SUB = 16
HG_MAX_HALF_RANGE = 80.0
RET_DK = 256
RET_DV = 512
RET_CHUNK = 64
RET_T = 256
ROPE_BASE = 10000.0
N_EXPERTS = 32
TOP_K = 4
SWIGLU_ALPHA = 1.702
SWIGLU_LIMIT = 7.0
MOE_ROWS = 1024
MOE_SUB_ROWS = 512
WEIGHT_CAST_ROWS = 128
ROW_TILE = 512
INPROJ_SUB_ROWS = 256
OUTPROJ_SUB_ROWS = 512
SC_WINDOW = 128
ROW_PARTS = 2
V7X_VMEM_LIMIT = 56 * 1024 * 1024


def _dot(a, b):
    return jnp.dot(a, b, preferred_element_type=F32)


def _dot_nt(a, b, precision=None):
    return lax.dot_general(a, b, (((1,), (1,)), ((), ())), precision=precision,
                           preferred_element_type=F32)


def _dot_tn(a, b):
    return lax.dot_general(a, b, (((0,), (0,)), ((), ())), preferred_element_type=F32)


def _rms(x):
    return x * lax.rsqrt(jnp.mean(x * x, axis=-1, keepdims=True) + EPS)


def _sigmoid(x):
    return 1.0 / (1.0 + jnp.exp(-x))


def _pack_rows(h):
    half = h.shape[1] // 2
    a = lax.bitcast_convert_type(h[:, :half].astype(BF16).astype(F32), jnp.uint32)
    b = lax.bitcast_convert_type(h[:, half:].astype(BF16).astype(F32), jnp.uint32)
    return lax.bitcast_convert_type(a | (b >> 16), I32)


def _unpack_rows(w):
    u = lax.bitcast_convert_type(w, jnp.uint32)
    a = lax.bitcast_convert_type(u & jnp.uint32(0xFFFF0000), F32)
    b = lax.bitcast_convert_type(u << 16, F32)
    return jnp.concatenate([a, b], axis=1)


def _ada_kernel(c_ref, w_ref, b_ref, o_ref):
    c = c_ref[...]
    cond = c * _sigmoid(c)
    o_ref[0] = jnp.dot(cond, w_ref[0], precision=HIGHEST, preferred_element_type=F32) + b_ref[0]


def _ada(c, w, b):
    nl, d, kd = w.shape
    bsz = c.shape[0]
    return pl.pallas_call(
        _ada_kernel,
        grid=(nl, kd // d),
        in_specs=[
            pl.BlockSpec((bsz, d), lambda l, j: (0, 0)),
            pl.BlockSpec((1, d, d), lambda l, j: (l, 0, j)),
            pl.BlockSpec((1, 1, d), lambda l, j: (l, 0, j)),
        ],
        out_specs=pl.BlockSpec((1, bsz, d), lambda l, j: (l, 0, j)),
        out_shape=jax.ShapeDtypeStruct((nl, bsz, kd), F32),
        name="ada_mod",
    )(c, w, b.reshape(nl, 1, kd))


def _moe_residual(x_ref, refs, rows=slice(None)):
    ya_refs, yb_refs = refs[:TOP_K], refs[TOP_K:2 * TOP_K]
    gate_ref, g2_ref = refs[2 * TOP_K:]
    gate = gate_ref[rows, :]
    acc = None
    for k in range(TOP_K):
        packed = jnp.concatenate([ya_refs[k][rows, :], yb_refs[k][rows, :]], axis=1)
        term = gate[:, k:k + 1] * _unpack_rows(packed)
        acc = term if acc is None else acc + term
    return x_ref[0, rows, :] + g2_ref[0] * acc


N_MOE_REFS = 2 * TOP_K + 2


def _moe_residual_operands(pending, bsz, s, d, tm):
    yg, gate_rows, mod = pending
    nt = s // tm
    ntok = (bsz * s) // tm

    def yspec(k):
        return pl.BlockSpec((tm, d // 4), lambda b, i: (k * ntok + b * nt + i, 0))

    specs = [yspec(k) for k in range(TOP_K)] * 2 + [
        pl.BlockSpec((tm, TOP_K), lambda b, i: (b * nt + i, 0)),
        pl.BlockSpec((1, 1, d), lambda b, i: (b, 0, 5))]
    return specs, [yg[0]] * TOP_K + [yg[1]] * TOP_K + [gate_rows, mod]


def _inproj_kernel(x_ref, *refs, col_chunk, fused):
    if fused:
        g_ref, sh_ref, sc_ref, w_ref, o_ref, xo_ref = refs[N_MOE_REFS:]
    else:
        g_ref, sh_ref, sc_ref, w_ref, o_ref = refs
    nout = w_ref.shape[1]
    tm = x_ref.shape[1]
    sub = min(INPROJ_SUB_ROWS, tm)
    for r in range(0, tm, sub):
        rows = slice(r, r + sub)
        if fused:
            x = _moe_residual(x_ref, refs[:N_MOE_REFS], rows)
            xo_ref[0, rows, :] = x
        else:
            x = x_ref[0, rows, :]
        h = _rms(x) * g_ref[...] * (1.0 + sc_ref[0]) + sh_ref[0]
        hb = h.astype(BF16)
        for j in range(nout // col_chunk):
            cs = slice(j * col_chunk, (j + 1) * col_chunk)
            o_ref[0, rows, cs] = _dot(hb, w_ref[:, cs]).astype(BF16)


def _inproj(x, pending, gain, mod, w):
    bsz, s, d = x.shape
    nout = w.shape[1]
    tm = min(ROW_TILE, s)
    xspec = pl.BlockSpec((1, tm, d), lambda b, i: (b, i, 0))
    fused = pending is not None
    moe_specs, moe_args = _moe_residual_operands(pending, bsz, s, d, tm) if fused else ([], [])
    proj_spec = pl.BlockSpec((1, tm, nout), lambda b, i: (b, i, 0))
    proj_shape = jax.ShapeDtypeStruct((bsz, s, nout), BF16)
    out = pl.pallas_call(
        functools.partial(_inproj_kernel, col_chunk=1024, fused=fused),
        grid=(bsz, s // tm),
        in_specs=[xspec] + moe_specs + [
            pl.BlockSpec((1, d), lambda b, i: (0, 0)),
            pl.BlockSpec((1, 1, d), lambda b, i: (b, 0, 0)),
            pl.BlockSpec((1, 1, d), lambda b, i: (b, 0, 1)),
            pl.BlockSpec((d, nout), lambda b, i: (0, 0), pipeline_mode=pl.Buffered(1)),
        ],
        out_specs=[proj_spec, xspec] if fused else proj_spec,
        out_shape=[proj_shape, jax.ShapeDtypeStruct((bsz, s, d), F32)] if fused else proj_shape,
        compiler_params=pltpu.CompilerParams(
            dimension_semantics=("parallel", "parallel"), vmem_limit_bytes=V7X_VMEM_LIMIT),
        name="inproj",
    )(x, *moe_args, gain, mod, mod, w)
    return (out[1], out[0]) if fused else (x, out)


def _hgrn_gates(q, f, lb, one_m_lb):
    e = jnp.exp(-jnp.abs(f))
    inv = 1.0 / (1.0 + e)
    pos = f >= 0.0
    t = e * inv
    sig = jnp.where(pos, inv, t)
    sig_neg = jnp.where(pos, t, inv)
    has_lb = lb > 0.0
    logf = jnp.log(jnp.where(has_lb, lb + one_m_lb * sig, inv)) + jnp.where(has_lb, 0.0, jnp.minimum(f, 0.0))
    return q * _sigmoid(q), one_m_lb * sig_neg, logf


def _hgrn_kernel(q_ref, f_ref, i_ref, g_ref, lb_ref, gain_ref, o_ref, kk_s, v_s, b_s):
    t = HG_T
    dk = HG_DK
    w = q_ref.shape[2]
    n_chunks = q_ref.shape[1] // t
    lb = lb_ref[...]
    one_m_lb = 1.0 - lb
    gain = gain_ref[...]
    heads = [slice(h * dk, (h + 1) * dk) for h in range(HG_PAIR)]
    row = lax.broadcasted_iota(I32, (t, HG_PAIR * t), 0)
    col = lax.broadcasted_iota(I32, (t, HG_PAIR * t), 1)
    causal = row >= (col % t)
    tril = causal[:, :t].astype(BF16)

    def finish(o, g):
        return (_rms(o) * gain * (g * _sigmoid(g))).astype(BF16)

    def block_diag(parts):
        rows = []
        for h, p in enumerate(parts):
            z = jnp.zeros_like(p)
            rows.append(jnp.concatenate([p if j == h else z for j in range(HG_PAIR)], axis=1))
        return jnp.concatenate(rows, axis=0)

    def chunk(c, carry):
        sts, bmax = carry
        rows = pl.ds(pl.multiple_of(c * t, t), t)
        q = q_ref[0, rows, :].astype(F32)
        f = f_ref[0, rows, :].astype(F32)
        v = i_ref[0, rows, :]
        g = g_ref[0, rows, :].astype(F32)
        qs, kk, logf = _hgrn_gates(q, f, lb, one_m_lb)
        hi = logf.astype(BF16)
        lo = (logf - hi.astype(F32)).astype(BF16)
        bb = _dot(tril, jnp.concatenate([hi, lo], axis=-1))
        b = bb[:, :w] + bb[:, w:]
        b_last = b[t - 1:t, :]
        mid = 0.5 * b_last
        e_mid = jnp.exp(mid)
        qt = qs * jnp.exp(b - mid)
        kt = kk * jnp.exp(mid - b)
        ktb = kt.astype(BF16)
        att = _dot_nt(qt.astype(BF16), block_diag([ktb[:, hs] for hs in heads]))
        att = jnp.where(causal, att, 0.0).astype(BF16)
        o = _dot(att, block_diag([v[:, hs] for hs in heads]))
        st_bd = block_diag([st.astype(BF16) for st in sts])
        o = o + _dot_nt((qt * e_mid).astype(BF16), st_bd)
        kd = (kt * e_mid).astype(BF16)
        decay = e_mid * e_mid
        sts = tuple(st * decay[:, hs] + _dot_tn(v[:, hs], kd[:, hs]) for st, hs in zip(sts, heads))
        out = [finish(o[:, hs], g[:, hs]) for hs in heads]
        o_ref[0, rows, :] = jnp.concatenate(out, axis=1)
        return sts, jnp.maximum(bmax, jnp.abs(b_last))

    st0 = jnp.zeros((dk, dk), F32)
    _, bmax = lax.fori_loop(0, n_chunks, chunk, ((st0,) * HG_PAIR, jnp.zeros_like(lb)), unroll=8)
    safe = 0.5 * jnp.max(bmax) <= HG_MAX_HALF_RANGE

    @pl.when(jnp.logical_not(safe))
    def _():
        n = HG_SUB
        sub_row = lax.broadcasted_iota(I32, (n, 1), 0)
        tril_n = (lax.broadcasted_iota(I32, (n, n), 0) >= lax.broadcasted_iota(I32, (n, n), 1)).astype(F32)

        for hs in heads:
            def block(i, st, hs=hs):
                rows = pl.ds(pl.multiple_of(i * n, n), n)
                q = q_ref[0, rows, hs].astype(F32)
                f = f_ref[0, rows, hs].astype(F32)
                v = i_ref[0, rows, hs]
                g = g_ref[0, rows, hs].astype(F32)
                qs, kk, logf = _hgrn_gates(q, f, lb[:, hs], one_m_lb[:, hs])
                b = jnp.dot(tril_n, logf, precision=HIGHEST, preferred_element_type=F32)
                kk_s[...] = kk
                v_s[...] = v.astype(F32)
                b_s[...] = b
                o = _dot_nt((qs * jnp.exp(b)).astype(BF16), st.astype(BF16))

                def pair(s, acc):
                    dec = jnp.exp(jnp.minimum(b - b_s[pl.ds(s, 1), :], 0.0))
                    wgt = jnp.sum(qs * kk_s[pl.ds(s, 1), :] * dec, axis=-1, keepdims=True)
                    return acc + jnp.where(sub_row >= s, wgt, 0.0) * v_s[pl.ds(s, 1), :]

                o = lax.fori_loop(0, n, pair, o)
                b_last = b[n - 1:n, :]
                st = st * jnp.exp(b_last) + _dot_tn(v, (kk * jnp.exp(b_last - b)).astype(BF16))
                o_ref[0, rows, hs] = finish(o, g)
                return st

            lax.fori_loop(0, q_ref.shape[1] // n, block, st0)


def _hgrn(proj, lb, gain):
    bsz, s, w4 = proj.shape
    w = HG_PAIR * HG_DK
    npair = w4 // (4 * w)

    def spec(j):
        return pl.BlockSpec((1, s, w), lambda b, p: (b, 0, p + j * npair))

    return pl.pallas_call(
        _hgrn_kernel,
        grid=(bsz, npair),
        in_specs=[spec(0), spec(1), spec(2), spec(3),
                  pl.BlockSpec((1, w), lambda b, p: (0, p)),
                  pl.BlockSpec((1, HG_DK), lambda b, p: (0, 0))],
        out_specs=pl.BlockSpec((1, s, w), lambda b, p: (b, 0, p)),
        out_shape=jax.ShapeDtypeStruct((bsz, s, npair * w), BF16),
        scratch_shapes=[pltpu.VMEM((HG_SUB, HG_DK), F32)] * 3,
        compiler_params=pltpu.CompilerParams(dimension_semantics=("parallel", "parallel")),
        name="hgrn",
    )(proj, proj, proj, proj, lb, gain)


def _rope_kernel(pos_ref, inv_ref, cos_ref, sin_ref):
    ang = pos_ref[0].astype(F32) * inv_ref[...]
    cos_ref[0] = jnp.cos(ang)
    sin_ref[0] = jnp.sin(ang)


def _rope_tables(positions):
    bsz, s = positions.shape
    half = RET_DK // 2
    inv_freq = (1.0 / (ROPE_BASE ** jnp.linspace(0.0, 1.0, half, dtype=F32))).reshape(1, half)
    out = jax.ShapeDtypeStruct((bsz, s, half), F32)
    return pl.pallas_call(
        _rope_kernel,
        grid=(bsz,),
        in_specs=[pl.BlockSpec((1, s, 1), lambda b: (b, 0, 0)),
                  pl.BlockSpec((1, half), lambda b: (0, 0))],
        out_specs=[pl.BlockSpec((1, s, half), lambda b: (b, 0, 0))] * 2,
        out_shape=[out, out],
        name="rope_tables",
    )(positions.reshape(bsz, s, 1), inv_freq)


def _ret_kernel(q_ref, k_ref, v_ref, g_ref, cos_ref, sin_ref, lg_ref, o_ref, r_ref, d_ref):
    t = RET_T
    dk = RET_DK
    half = dk // 2
    n_steps = q_ref.shape[1] // t
    lg = lg_ref[0]
    lg_k = lg[:, :dk]
    n = lax.broadcasted_iota(I32, (t, t), 0)
    m = lax.broadcasted_iota(I32, (t, t), 1)
    dist = jnp.abs(n - m).astype(F32)
    visible = (m // RET_CHUNK) <= (n // RET_CHUNK)
    k_scale = dk ** -0.5
    d_ref[...] = jnp.where(visible, k_scale * jnp.exp(dist * lg[:, :t]), 0.0)
    idx = lax.broadcasted_iota(I32, (t, dk), 0).astype(F32)
    q_decay = jnp.exp((idx + 1.0) * lg_k)
    k_decay = k_scale * jnp.exp((t - 1.0 - idx) * lg_k)
    step_decay = jnp.exp(float(t) * lg)
    r_ref[...] = jnp.zeros_like(r_ref)

    def rotate(x, cos, sin):
        x1, x2 = x[:, :half], x[:, half:]
        return jnp.concatenate([x1 * cos - x2 * sin, x2 * cos + x1 * sin], axis=-1)

    def step(c, carry):
        r0 = pl.multiple_of(c * t, t)
        rows = pl.ds(r0, t)
        cos = cos_ref[0, rows, :]
        sin = sin_ref[0, rows, :]
        q = rotate(q_ref[0, rows, :].astype(F32), cos, sin)
        k = rotate(k_ref[0, rows, :].astype(F32), cos, sin)
        v = v_ref[0, rows, :]
        g = g_ref[0, rows, :].astype(F32)
        r = r_ref[...]
        sc = _dot_nt(q.astype(BF16), k.astype(BF16)) * d_ref[...]
        o = _dot(sc.astype(BF16), v) + _dot((q * q_decay).astype(BF16), r.astype(BF16))
        r_ref[...] = r * step_decay + _dot_tn((k * k_decay).astype(BF16), v)
        out = _rms(o) * (g * _sigmoid(g))
        o_ref[0, rows, :] = out.astype(BF16)
        return carry

    lax.fori_loop(0, n_steps, step, 0, unroll=2)


def _retention(proj, cos, sin):
    bsz, s, w = proj.shape
    dk, dv = RET_DK, RET_DV
    nh = w // (2 * dk + 2 * dv)
    hidx = jnp.arange(nh, dtype=F32)
    log_gamma = jnp.log(1.0 - 2.0 ** (-5.0 - hidx))
    lg = jnp.broadcast_to(log_gamma[:, None, None], (nh, 1, dv))
    vbase = 2 * nh * dk // dv
    return pl.pallas_call(
        _ret_kernel,
        grid=(bsz, nh),
        in_specs=[
            pl.BlockSpec((1, s, dk), lambda b, h: (b, 0, h)),
            pl.BlockSpec((1, s, dk), lambda b, h: (b, 0, nh + h)),
            pl.BlockSpec((1, s, dv), lambda b, h: (b, 0, vbase + h)),
            pl.BlockSpec((1, s, dv), lambda b, h: (b, 0, vbase + nh + h)),
            pl.BlockSpec((1, s, dk // 2), lambda b, h: (b, 0, 0)),
            pl.BlockSpec((1, s, dk // 2), lambda b, h: (b, 0, 0)),
            pl.BlockSpec((1, 1, dv), lambda b, h: (h, 0, 0)),
        ],
        out_specs=pl.BlockSpec((1, s, dv), lambda b, h: (b, 0, h)),
        out_shape=jax.ShapeDtypeStruct((bsz, s, nh * dv), BF16),
        scratch_shapes=[pltpu.VMEM((dk, dv), F32), pltpu.VMEM((RET_T, RET_T), F32)],
        compiler_params=pltpu.CompilerParams(
            dimension_semantics=("parallel", "parallel"), vmem_limit_bytes=V7X_VMEM_LIMIT),
        name="retention",
    )(proj, proj, proj, proj, cos, sin, lg)


def _outproj_kernel(o_ref, w_ref, x_ref, g1_ref, sh_ref, sc_ref, gain_ref, rw_ref, rb_ref, u_ref,
                    xo_ref, ha_ref, hb_ref, eidx_ref, gate_ref, rank_ref, cnt_ref, base_ref):
    first = jnp.logical_and(pl.program_id(0) == 0, pl.program_id(1) == 0)

    @pl.when(first)
    def _():
        base_ref[...] = jnp.zeros_like(base_ref)

    ne = rw_ref.shape[0]
    rw = rw_ref[...]
    rw_hi = rw.astype(BF16)
    rw_lo = (rw - rw_hi.astype(F32)).astype(BF16)
    rw_both = jnp.concatenate([rw_hi, rw_lo], axis=0)
    sub = u_ref.shape[0]
    eiota = lax.broadcasted_iota(I32, (ne, sub), 0)
    base = base_ref[...]
    for r in range(0, x_ref.shape[1], sub):
        rows = slice(r, r + sub)
        y = _dot(o_ref[0, rows, :], w_ref[...])
        xn = x_ref[0, rows, :] + g1_ref[0] * y
        xo_ref[0, rows, :] = xn
        h = _rms(xn) * gain_ref[...] * (1.0 + sc_ref[0]) + sh_ref[0]
        packed = _pack_rows(h)
        slab = packed.shape[1] // ROW_PARTS
        ha_ref[0, rows, :] = packed[:, :slab]
        hb_ref[0, rows, :] = packed[:, slab:]

        h_hi = h.astype(BF16)
        h_lo = (h - h_hi.astype(F32)).astype(BF16)
        part = _dot_nt(rw_both, h_hi)
        work = part[:ne] + part[ne:] + _dot_nt(rw_hi, h_lo) + rb_ref[...]
        onehots, tops = [], []
        for k in range(TOP_K):
            mx = jnp.max(work, axis=0, keepdims=True)
            idx = jnp.min(jnp.where(work == mx, eiota, ne), axis=0, keepdims=True)
            oh = eiota == idx
            work = jnp.where(oh, -jnp.inf, work)
            eidx_ref[k:k + 1, rows] = idx
            onehots.append(oh)
            tops.append(mx)
        ex = [jnp.exp(m - tops[0]) for m in tops]
        denom = ex[0] + ex[1] + ex[2] + ex[3]
        for k in range(TOP_K):
            gate_ref[k:k + 1, rows] = ex[k] / denom

        mask = jnp.zeros((ne, sub), F32)
        for oh in onehots:
            mask = mask + oh.astype(F32)
        incl = _dot(mask.astype(BF16), u_ref[...])
        excl = incl - mask + base
        for k in range(TOP_K):
            rk = jnp.sum(jnp.where(onehots[k], excl, 0.0), axis=0, keepdims=True)
            rank_ref[k:k + 1, rows] = rk.astype(I32)
        base = base + incl[:, sub - 1:sub]
    base_ref[...] = base
    cnt_ref[...] = base.astype(I32)


def _outproj_route(o, w_out, x, mod, gain2, router_wt, router_b):
    bsz, s, d = x.shape
    hv = o.shape[2]
    tm = min(ROW_TILE, s)
    n = bsz * s
    nt = s // tm
    ne = router_wt.shape[0]
    sub = min(OUTPROJ_SUB_ROWS, tm)
    upper = (jnp.arange(sub)[:, None] <= jnp.arange(sub)[None, :]).astype(BF16)

    def modspec(j):
        return pl.BlockSpec((1, 1, d), lambda b, i: (b, 0, j))

    tokspec = pl.BlockSpec((TOP_K, tm), lambda b, i: (0, b * nt + i))
    return pl.pallas_call(
        _outproj_kernel,
        grid=(bsz, nt),
        in_specs=[
            pl.BlockSpec((1, tm, hv), lambda b, i: (b, i, 0)),
            pl.BlockSpec((hv, d), lambda b, i: (0, 0)),
            pl.BlockSpec((1, tm, d), lambda b, i: (b, i, 0)),
            modspec(2), modspec(3), modspec(4),
            pl.BlockSpec((1, d), lambda b, i: (0, 0)),
            pl.BlockSpec((ne, d), lambda b, i: (0, 0)),
            pl.BlockSpec((ne, 1), lambda b, i: (0, 0)),
            pl.BlockSpec((sub, sub), lambda b, i: (0, 0)),
        ],
        out_specs=[
            pl.BlockSpec((1, tm, d), lambda b, i: (b, i, 0)),
            pl.BlockSpec((1, tm, d // 4), lambda b, i: (b, i, 0)),
            pl.BlockSpec((1, tm, d // 4), lambda b, i: (b, i, 0)),
            tokspec, tokspec, tokspec,
            pl.BlockSpec((ne, 1), lambda b, i: (0, 0)),
        ],
        out_shape=[
            jax.ShapeDtypeStruct((bsz, s, d), F32),
            jax.ShapeDtypeStruct((bsz, s, d // 4), I32),
            jax.ShapeDtypeStruct((bsz, s, d // 4), I32),
            jax.ShapeDtypeStruct((TOP_K, n), I32),
            jax.ShapeDtypeStruct((TOP_K, n), F32),
            jax.ShapeDtypeStruct((TOP_K, n), I32),
            jax.ShapeDtypeStruct((ne, 1), I32),
        ],
        scratch_shapes=[pltpu.VMEM((ne, 1), F32)],
        compiler_params=pltpu.CompilerParams(
            dimension_semantics=("arbitrary", "arbitrary"), vmem_limit_bytes=V7X_VMEM_LIMIT),
        name="outproj_route",
    )(o, w_out, x, mod, mod, mod, gain2, router_wt, router_b, upper)


def _moe_kernel(be_ref, nb_ref, xa_ref, xb_ref, w1_ref, b1_ref, w2_ref, b2_ref, ya_ref, yb_ref,
                w1_s, w2_s):
    i = pl.program_id(0)
    used = i < nb_ref[0]
    new_expert = jnp.logical_or(i == 0, be_ref[i] != be_ref[jnp.maximum(i - 1, 0)])

    @pl.when(jnp.logical_and(used, new_expert))
    def _():
        for w_ref, w_s in ((w1_ref, w1_s), (w2_ref, w2_s)):
            for r in range(0, w_s.shape[0], WEIGHT_CAST_ROWS):
                rows = slice(r, r + WEIGHT_CAST_ROWS)
                w_s[rows, :] = w_ref[0, 0, rows, :].astype(BF16)

    @pl.when(used)
    def _():
        f = w2_s.shape[0]
        for r in range(0, xa_ref.shape[0], MOE_SUB_ROWS):
            rows = slice(r, r + MOE_SUB_ROWS)
            x = _unpack_rows(jnp.concatenate([xa_ref[rows, :], xb_ref[rows, :]], axis=1)).astype(BF16)
            u = _dot(x, w1_s[...]) + b1_ref[0, 0]
            glu = jnp.minimum(u[:, :f], SWIGLU_LIMIT)
            lin = jnp.clip(u[:, f:], -SWIGLU_LIMIT, SWIGLU_LIMIT)
            a = glu * _sigmoid(SWIGLU_ALPHA * glu) * (lin + 1.0)
            y = _dot(a.astype(BF16), w2_s[...]) + b2_ref[0, 0]
            packed = _pack_rows(y)
            slab = packed.shape[1] // ROW_PARTS
            ya_ref[rows, :] = packed[:, :slab]
            yb_ref[rows, :] = packed[:, slab:]

    @pl.when(jnp.logical_not(used))
    def _():
        ya_ref[...] = jnp.zeros_like(ya_ref)
        yb_ref[...] = jnp.zeros_like(yb_ref)


def _moe_blocks(block_e, n_used, xs, layer, w1, b1, w2, b2):
    n_rows, dh = xs[0].shape
    nl, ne, d, f2 = w1.shape
    f = f2 // 2
    nblk = n_rows // MOE_ROWS
    grid_spec = pltpu.PrefetchScalarGridSpec(
        num_scalar_prefetch=2,
        grid=(nblk,),
        in_specs=[
            pl.BlockSpec((MOE_ROWS, dh), lambda i, be, nb: (i, 0)),
            pl.BlockSpec((MOE_ROWS, dh), lambda i, be, nb: (i, 0)),
            pl.BlockSpec((1, 1, d, f2), lambda i, be, nb: (layer, be[i], 0, 0)),
            pl.BlockSpec((1, 1, 1, f2), lambda i, be, nb: (layer, be[i], 0, 0)),
            pl.BlockSpec((1, 1, f, d), lambda i, be, nb: (layer, be[i], 0, 0)),
            pl.BlockSpec((1, 1, 1, d), lambda i, be, nb: (layer, be[i], 0, 0)),
        ],
        out_specs=[pl.BlockSpec((MOE_ROWS, dh), lambda i, be, nb: (i, 0))] * 2,
        scratch_shapes=[pltpu.VMEM((d, f2), BF16), pltpu.VMEM((f, d), BF16)],
    )
    return pl.pallas_call(
        _moe_kernel,
        grid_spec=grid_spec,
        out_shape=[jax.ShapeDtypeStruct((n_rows, dh), I32)] * 2,
        compiler_params=pltpu.CompilerParams(
            dimension_semantics=("arbitrary",), vmem_limit_bytes=V7X_VMEM_LIMIT),
        name="moe_experts",
    )(block_e, n_used, xs[0], xs[1], w1, b1.reshape(nl, ne, 1, f2), w2, b2.reshape(nl, ne, 1, d))


def _sc_mesh():
    return plsc.VectorSubcoreMesh(core_axis_name="c", subcore_axis_name="s")


def _sc_scatter_rows(srcs, dests, n_rows):
    n, w = srcs[0].shape
    ns, nk = len(srcs), len(dests)
    out = jax.ShapeDtypeStruct((n_rows, w), srcs[0].dtype)

    @functools.partial(pl.kernel, out_type=[out] * ns, mesh=_sc_mesh(), scratch_types=[])
    def scatter_kernel(*refs):
        x_hbm, idx_hbm, o_hbm = refs[:ns], refs[ns:ns + nk], refs[ns + nk:]
        for x, o in zip(x_hbm, o_hbm):
            def body(x_vmem, *idx_vmem, o=o):
                for iv in idx_vmem:
                    pltpu.sync_copy(x_vmem, o.at[iv.at[0]])

            pltpu.emit_pipeline(
                body,
                grid=(n // SC_WINDOW,),
                in_specs=[pl.BlockSpec((SC_WINDOW, w), lambda i: (i, 0))]
                + [pl.BlockSpec((1, SC_WINDOW), lambda i: (0, i))] * nk,
                out_specs=[],
                core_axis_name=("c", "s"),
                dimension_semantics=(pltpu.PARALLEL,),
            )(x, *idx_hbm)

    return scatter_kernel(*srcs, *dests)


def _sc_gather_rows(tables, idx):
    m = idx.shape[1]
    w = tables[0].shape[1]
    nt = len(tables)
    out = jax.ShapeDtypeStruct((m, w), tables[0].dtype)

    @functools.partial(pl.kernel, out_type=[out] * nt, mesh=_sc_mesh(), scratch_types=[])
    def gather_kernel(*refs):
        t_hbm, i_hbm, o_hbm = refs[:nt], refs[nt], refs[nt + 1:]
        for t, o in zip(t_hbm, o_hbm):
            def body(i_vmem, o_vmem, t=t):
                pltpu.sync_copy(t.at[i_vmem.at[0]], o_vmem)

            pltpu.emit_pipeline(
                body,
                grid=(m // SC_WINDOW,),
                in_specs=[pl.BlockSpec((1, SC_WINDOW), lambda i: (0, i))],
                out_specs=[pl.BlockSpec((SC_WINDOW, w), lambda i: (i, 0))],
                core_axis_name=("c", "s"),
                dimension_semantics=(pltpu.PARALLEL,),
            )(i_hbm, o)

    return gather_kernel(*tables, idx)


def _dest_kernel(ps_ref, eidx_ref, rank_ref, o_ref):
    eidx = eidx_ref[...]
    start = jnp.zeros_like(eidx)
    for e in range(N_EXPERTS):
        start = jnp.where(eidx == e, ps_ref[e], start)
    o_ref[...] = start + rank_ref[...]


def _dest_rows(pad_start, eidx, rank):
    k, n = eidx.shape
    tn = min(8192, n)
    blk = pl.BlockSpec((k, tn), lambda i, ps: (0, i))
    return pl.pallas_call(
        _dest_kernel,
        grid_spec=pltpu.PrefetchScalarGridSpec(
            num_scalar_prefetch=1, grid=(n // tn,), in_specs=[blk, blk], out_specs=blk),
        out_shape=jax.ShapeDtypeStruct((k, n), I32),
        name="dest_rows",
    )(pad_start, eidx, rank)


def _final_kernel(x_ref, *refs):
    g_ref, sh_ref, sc_ref, o_ref = refs[N_MOE_REFS:]
    x = _moe_residual(x_ref, refs[:N_MOE_REFS])
    o_ref[0] = _rms(x) * g_ref[...] * (1.0 + sc_ref[0]) + sh_ref[0]


def _final(x, pending, gain, fmod):
    bsz, s, d = x.shape
    tm = min(ROW_TILE, s)
    blk = pl.BlockSpec((1, tm, d), lambda b, i: (b, i, 0))
    moe_specs, moe_args = _moe_residual_operands(pending, bsz, s, d, tm)
    return pl.pallas_call(
        _final_kernel,
        grid=(bsz, s // tm),
        in_specs=[blk] + moe_specs + [
            pl.BlockSpec((1, d), lambda b, i: (0, 0)),
            pl.BlockSpec((1, 1, d), lambda b, i: (b, 0, 0)),
            pl.BlockSpec((1, 1, d), lambda b, i: (b, 0, 1))],
        out_specs=blk,
        out_shape=jax.ShapeDtypeStruct((bsz, s, d), F32),
        compiler_params=pltpu.CompilerParams(dimension_semantics=("parallel", "parallel")),
        name="final_norm",
    )(x, *moe_args, gain, fmod, fmod)


def _moe_layer(h2, eidx, gate, rank, counts, layer, w1, b1, w2, b2):
    bsz, s, dh = h2[0].shape
    n = bsz * s
    ne = w1.shape[1]
    nblk = -(-(n * TOP_K) // MOE_ROWS) + ne
    n_rows = nblk * MOE_ROWS
    counts = counts[:, 0]
    padded = (counts + MOE_ROWS - 1) // MOE_ROWS * MOE_ROWS
    pad_end = jnp.cumsum(padded)
    pad_start = pad_end - padded
    block_start = jnp.arange(nblk, dtype=I32)[:, None] * MOE_ROWS
    block_e = jnp.minimum(jnp.sum(pad_end[None, :] <= block_start, axis=1), ne - 1).astype(I32)
    n_used = (pad_end[-1:] // MOE_ROWS).astype(I32)
    dest = _dest_rows(pad_start.astype(I32), eidx, rank)
    xs = _sc_scatter_rows([h.reshape(n, dh) for h in h2], [dest[k:k + 1] for k in range(TOP_K)], n_rows)
    ys = _moe_blocks(block_e, n_used, xs, layer, w1, b1, w2, b2)
    yg = _sc_gather_rows(ys, dest.reshape(1, TOP_K * n))
    return yg, gate.T


def _hgrn_lower_bounds(lb_logits):
    p = jax.nn.softmax(lb_logits.astype(F32), axis=0)
    cum = jnp.cumsum(p, axis=0)
    return cum - cum[0:1]


def kernel(x, c, positions, ada_w, ada_b, norm1_g, norm2_g, hgrn_w_in, hgrn_w_out, hgrn_o_gain, hgrn_lb_logits, ret_w_in, ret_w_out, router_w, router_b, moe_w1, moe_b1, moe_w2, moe_b2, final_g, final_ada_w, final_ada_b):
    depth = ada_w.shape[0]
    mods = _ada(c, ada_w, ada_b)
    fmod = _ada(c, final_ada_w[None], final_ada_b[None])[0][:, None, :]
    lbs = _hgrn_lower_bounds(hgrn_lb_logits)
    cos, sin = _rope_tables(positions)
    pending = None
    for layer in range(depth):
        mod = mods[layer][:, None, :]
        j = layer // N_MIXERS
        if layer % N_MIXERS == 0:
            x, proj = _inproj(x, pending, norm1_g[layer][None], mod, hgrn_w_in[j].astype(BF16))
            o = _hgrn(proj, lbs[j][None], hgrn_o_gain[j][None])
            w_out = hgrn_w_out[j]
        else:
            x, proj = _inproj(x, pending, norm1_g[layer][None], mod, ret_w_in[j].astype(BF16))
            o = _retention(proj, cos, sin)
            w_out = ret_w_out[j]
        x, ha, hb, eidx, gate, rank, counts = _outproj_route(
            o, w_out.astype(BF16), x, mod, norm2_g[layer][None],
            router_w[layer].T, router_b[layer][:, None])
        yg, gate_rows = _moe_layer((ha, hb), eidx, gate, rank, counts, layer, moe_w1, moe_b1, moe_w2, moe_b2)
        pending = (yg, gate_rows, mod)
    return _final(x, pending, final_g[None], fmod)
```

```python
import functools

import jax
import jax.numpy as jnp
from jax import lax
from jax.experimental import pallas as pl
from jax.experimental.pallas import tpu as pltpu
from jax.experimental.pallas import tpu_sc as plsc

F32 = jnp.float32
BF16 = jnp.bfloat16
I32 = jnp.int32
HIGHEST = lax.Precision.HIGHEST

EPS = 1e-6
N_MIXERS = 2
HG_DK = 128
HG_T = 64
HG_PAIR = 2
HG_SUB = 16
HG_MAX_HALF_RANGE = 80.0
RET_DK = 256
RET_DV = 512
RET_CHUNK = 64
RET_T = 256
ROPE_BASE = 10000.0
N_EXPERTS = 32
TOP_K = 4
SWIGLU_ALPHA = 1.702
SWIGLU_LIMIT = 7.0
MOE_ROWS = 1024
MOE_SUB_ROWS = 512
WEIGHT_CAST_ROWS = 128
ROW_TILE = 512
INPROJ_SUB_ROWS = 256
OUTPROJ_SUB_ROWS = 512
SC_WINDOW = 128
ROW_PARTS = 2
V7X_VMEM_LIMIT = 56 * 1024 * 1024


def _dot(a, b):
    return jnp.dot(a, b, preferred_element_type=F32)


def _dot_nt(a, b, precision=None):
    return lax.dot_general(a, b, (((1,), (1,)), ((), ())), precision=precision,
                           preferred_element_type=F32)


def _dot_tn(a, b):
    return lax.dot_general(a, b, (((0,), (0,)), ((), ())), preferred_element_type=F32)


def _rms(x):
    return x * lax.rsqrt(jnp.mean(x * x, axis=-1, keepdims=True) + EPS)


def _sigmoid(x):
    return 1.0 / (1.0 + jnp.exp(-x))


def _pack_rows(h):
    half = h.shape[1] // 2
    a = lax.bitcast_convert_type(h[:, :half].astype(BF16).astype(F32), jnp.uint32)
    b = lax.bitcast_convert_type(h[:, half:].astype(BF16).astype(F32), jnp.uint32)
    return lax.bitcast_convert_type(a | (b >> 16), I32)


def _unpack_rows(w):
    u = lax.bitcast_convert_type(w, jnp.uint32)
    a = lax.bitcast_convert_type(u & jnp.uint32(0xFFFF0000), F32)
    b = lax.bitcast_convert_type(u << 16, F32)
    return jnp.concatenate([a, b], axis=1)


def _ada_kernel(c_ref, w_ref, b_ref, o_ref):
    c = c_ref[...]
    cond = c * _sigmoid(c)
    o_ref[0] = jnp.dot(cond, w_ref[0], precision=HIGHEST, preferred_element_type=F32) + b_ref[0]


def _ada(c, w, b):
    nl, d, kd = w.shape
    bsz = c.shape[0]
    return pl.pallas_call(
        _ada_kernel,
        grid=(nl, kd // d),
        in_specs=[
            pl.BlockSpec((bsz, d), lambda l, j: (0, 0)),
            pl.BlockSpec((1, d, d), lambda l, j: (l, 0, j)),
            pl.BlockSpec((1, 1, d), lambda l, j: (l, 0, j)),
        ],
        out_specs=pl.BlockSpec((1, bsz, d), lambda l, j: (l, 0, j)),
        out_shape=jax.ShapeDtypeStruct((nl, bsz, kd), F32),
        name="ada_mod",
    )(c, w, b.reshape(nl, 1, kd))


def _moe_residual(x_ref, refs, rows=slice(None)):
    ya_refs, yb_refs = refs[:TOP_K], refs[TOP_K:2 * TOP_K]
    gate_ref, g2_ref = refs[2 * TOP_K:]
    gate = gate_ref[rows, :]
    acc = None
    for k in range(TOP_K):
        packed = jnp.concatenate([ya_refs[k][rows, :], yb_refs[k][rows, :]], axis=1)
        term = gate[:, k:k + 1] * _unpack_rows(packed)
        acc = term if acc is None else acc + term
    return x_ref[0, rows, :] + g2_ref[0] * acc


N_MOE_REFS = 2 * TOP_K + 2


def _moe_residual_operands(pending, bsz, s, d, tm, tile):
    yg, gate_rows, mod = pending
    nt = s // tm
    ntok = (bsz * s) // tm

    def flat(*g):
        b, i = tile(*g)
        return b * nt + i

    def yspec(k):
        return pl.BlockSpec((tm, d // 4), lambda *g: (k * ntok + flat(*g), 0))

    specs = [yspec(k) for k in range(TOP_K)] * 2 + [
        pl.BlockSpec((tm, TOP_K), lambda *g: (flat(*g), 0)),
        pl.BlockSpec((1, 1, d), lambda *g: (tile(*g)[0], 0, 5))]
    return specs, [yg[0]] * TOP_K + [yg[1]] * TOP_K + [gate_rows, mod]


def _inproj_kernel(x_ref, *refs, col_chunk, fused):
    if fused:
        g_ref, sh_ref, sc_ref, w_ref, o_ref, xo_ref = refs[N_MOE_REFS:]
    else:
        g_ref, sh_ref, sc_ref, w_ref, o_ref = refs
    nout = w_ref.shape[1]
    tm = x_ref.shape[1]
    sub = min(INPROJ_SUB_ROWS, tm)
    for r in range(0, tm, sub):
        rows = slice(r, r + sub)
        if fused:
            x = _moe_residual(x_ref, refs[:N_MOE_REFS], rows)
            xo_ref[0, rows, :] = x
        else:
            x = x_ref[0, rows, :]
        h = _rms(x) * g_ref[...] * (1.0 + sc_ref[0]) + sh_ref[0]
        hb = h.astype(BF16)
        for j in range(nout // col_chunk):
            cs = slice(j * col_chunk, (j + 1) * col_chunk)
            o_ref[0, rows, cs] = _dot(hb, w_ref[:, cs]).astype(BF16)


def _inproj(x, pending, gain, mod, w):
    bsz, s, d = x.shape
    nout = w.shape[1]
    tm = min(ROW_TILE, s)
    xspec = pl.BlockSpec((1, tm, d), lambda b, i: (b, i, 0))
    fused = pending is not None
    moe_specs, moe_args = (_moe_residual_operands(pending, bsz, s, d, tm, lambda b, i: (b, i))
                           if fused else ([], []))
    proj_spec = pl.BlockSpec((1, tm, nout), lambda b, i: (b, i, 0))
    proj_shape = jax.ShapeDtypeStruct((bsz, s, nout), BF16)
    out = pl.pallas_call(
        functools.partial(_inproj_kernel, col_chunk=1024, fused=fused),
        grid=(bsz, s // tm),
        in_specs=[xspec] + moe_specs + [
            pl.BlockSpec((1, d), lambda b, i: (0, 0)),
            pl.BlockSpec((1, 1, d), lambda b, i: (b, 0, 0)),
            pl.BlockSpec((1, 1, d), lambda b, i: (b, 0, 1)),
            pl.BlockSpec((d, nout), lambda b, i: (0, 0), pipeline_mode=pl.Buffered(1)),
        ],
        out_specs=[proj_spec, xspec] if fused else proj_spec,
        out_shape=[proj_shape, jax.ShapeDtypeStruct((bsz, s, d), F32)] if fused else proj_shape,
        compiler_params=pltpu.CompilerParams(
            dimension_semantics=("parallel", "parallel"), vmem_limit_bytes=V7X_VMEM_LIMIT),
        name="inproj",
    )(x, *moe_args, gain, mod, mod, w)
    return (out[1], out[0]) if fused else (x, out)


def _hgrn_gates(q, f, lb, one_m_lb):
    e = jnp.exp(-jnp.abs(f))
    inv = 1.0 / (1.0 + e)
    pos = f >= 0.0
    t = e * inv
    sig = jnp.where(pos, inv, t)
    sig_neg = jnp.where(pos, t, inv)
    has_lb = lb > 0.0
    logf = jnp.log(jnp.where(has_lb, lb + one_m_lb * sig, inv)) + jnp.where(has_lb, 0.0, jnp.minimum(f, 0.0))
    return q * _sigmoid(q), one_m_lb * sig_neg, logf


def _hgrn_kernel(q_ref, f_ref, i_ref, g_ref, lb_ref, gain_ref, o_ref, kk_s, v_s, b_s):
    t = HG_T
    dk = HG_DK
    w = q_ref.shape[2]
    n_chunks = q_ref.shape[1] // t
    lb = lb_ref[...]
    one_m_lb = 1.0 - lb
    gain = gain_ref[...]
    heads = [slice(h * dk, (h + 1) * dk) for h in range(HG_PAIR)]
    row = lax.broadcasted_iota(I32, (t, HG_PAIR * t), 0)
    col = lax.broadcasted_iota(I32, (t, HG_PAIR * t), 1)
    causal = row >= (col % t)
    tril = causal[:, :t].astype(BF16)

    def finish(o, g):
        return (_rms(o) * gain * (g * _sigmoid(g))).astype(BF16)

    def block_diag(parts):
        rows = []
        for h, p in enumerate(parts):
            z = jnp.zeros_like(p)
            rows.append(jnp.concatenate([p if j == h else z for j in range(HG_PAIR)], axis=1))
        return jnp.concatenate(rows, axis=0)

    def chunk(c, carry):
        sts, bmax = carry
        rows = pl.ds(pl.multiple_of(c * t, t), t)
        q = q_ref[0, rows, :].astype(F32)
        f = f_ref[0, rows, :].astype(F32)
        v = i_ref[0, rows, :]
        g = g_ref[0, rows, :].astype(F32)
        qs, kk, logf = _hgrn_gates(q, f, lb, one_m_lb)
        hi = logf.astype(BF16)
        lo = (logf - hi.astype(F32)).astype(BF16)
        bb = _dot(tril, jnp.concatenate([hi, lo], axis=-1))
        b = bb[:, :w] + bb[:, w:]
        b_last = b[t - 1:t, :]
        mid = 0.5 * b_last
        e_mid = jnp.exp(mid)
        qt = qs * jnp.exp(b - mid)
        kt = kk * jnp.exp(mid - b)
        ktb = kt.astype(BF16)
        att = _dot_nt(qt.astype(BF16), block_diag([ktb[:, hs] for hs in heads]))
        att = jnp.where(causal, att, 0.0).astype(BF16)
        o = _dot(att, block_diag([v[:, hs] for hs in heads]))
        st_bd = block_diag([st.astype(BF16) for st in sts])
        o = o + _dot_nt((qt * e_mid).astype(BF16), st_bd)
        kd = (kt * e_mid).astype(BF16)
        decay = e_mid * e_mid
        sts = tuple(st * decay[:, hs] + _dot_tn(v[:, hs], kd[:, hs]) for st, hs in zip(sts, heads))
        out = [finish(o[:, hs], g[:, hs]) for hs in heads]
        o_ref[0, rows, :] = jnp.concatenate(out, axis=1)
        return sts, jnp.maximum(bmax, jnp.abs(b_last))

    st0 = jnp.zeros((dk, dk), F32)
    _, bmax = lax.fori_loop(0, n_chunks, chunk, ((st0,) * HG_PAIR, jnp.zeros_like(lb)), unroll=16)
    safe = 0.5 * jnp.max(bmax) <= HG_MAX_HALF_RANGE

    @pl.when(jnp.logical_not(safe))
    def _():
        n = HG_SUB
        sub_row = lax.broadcasted_iota(I32, (n, 1), 0)
        tril_n = (lax.broadcasted_iota(I32, (n, n), 0) >= lax.broadcasted_iota(I32, (n, n), 1)).astype(F32)

        for hs in heads:
            def block(i, st, hs=hs):
                rows = pl.ds(pl.multiple_of(i * n, n), n)
                q = q_ref[0, rows, hs].astype(F32)
                f = f_ref[0, rows, hs].astype(F32)
                v = i_ref[0, rows, hs]
                g = g_ref[0, rows, hs].astype(F32)
                qs, kk, logf = _hgrn_gates(q, f, lb[:, hs], one_m_lb[:, hs])
                b = jnp.dot(tril_n, logf, precision=HIGHEST, preferred_element_type=F32)
                kk_s[...] = kk
                v_s[...] = v.astype(F32)
                b_s[...] = b
                o = _dot_nt((qs * jnp.exp(b)).astype(BF16), st.astype(BF16))

                def pair(s, acc):
                    dec = jnp.exp(jnp.minimum(b - b_s[pl.ds(s, 1), :], 0.0))
                    wgt = jnp.sum(qs * kk_s[pl.ds(s, 1), :] * dec, axis=-1, keepdims=True)
                    return acc + jnp.where(sub_row >= s, wgt, 0.0) * v_s[pl.ds(s, 1), :]

                o = lax.fori_loop(0, n, pair, o)
                b_last = b[n - 1:n, :]
                st = st * jnp.exp(b_last) + _dot_tn(v, (kk * jnp.exp(b_last - b)).astype(BF16))
                o_ref[0, rows, hs] = finish(o, g)
                return st

            lax.fori_loop(0, q_ref.shape[1] // n, block, st0)


def _hgrn(proj, lb, gain):
    bsz, s, w4 = proj.shape
    w = HG_PAIR * HG_DK
    npair = w4 // (4 * w)

    def spec(j):
        return pl.BlockSpec((1, s, w), lambda b, p: (b, 0, p + j * npair))

    return pl.pallas_call(
        _hgrn_kernel,
        grid=(bsz, npair),
        in_specs=[spec(0), spec(1), spec(2), spec(3),
                  pl.BlockSpec((1, w), lambda b, p: (0, p)),
                  pl.BlockSpec((1, HG_DK), lambda b, p: (0, 0))],
        out_specs=pl.BlockSpec((1, s, w), lambda b, p: (b, 0, p)),
        out_shape=jax.ShapeDtypeStruct((bsz, s, npair * w), BF16),
        scratch_shapes=[pltpu.VMEM((HG_SUB, HG_DK), F32)] * 3,
        compiler_params=pltpu.CompilerParams(dimension_semantics=("parallel", "parallel")),
        name="hgrn",
    )(proj, proj, proj, proj, lb, gain)


def _rope_kernel(pos_ref, inv_ref, cos_ref, sin_ref):
    ang = pos_ref[0].astype(F32) * inv_ref[...]
    cos_ref[0] = jnp.cos(ang)
    sin_ref[0] = jnp.sin(ang)


def _rope_tables(positions):
    bsz, s = positions.shape
    half = RET_DK // 2
    inv_freq = (1.0 / (ROPE_BASE ** jnp.linspace(0.0, 1.0, half, dtype=F32))).reshape(1, half)
    out = jax.ShapeDtypeStruct((bsz, s, half), F32)
    return pl.pallas_call(
        _rope_kernel,
        grid=(bsz,),
        in_specs=[pl.BlockSpec((1, s, 1), lambda b: (b, 0, 0)),
                  pl.BlockSpec((1, half), lambda b: (0, 0))],
        out_specs=[pl.BlockSpec((1, s, half), lambda b: (b, 0, 0))] * 2,
        out_shape=[out, out],
        name="rope_tables",
    )(positions.reshape(bsz, s, 1), inv_freq)


def _ret_kernel(q_ref, k_ref, v_ref, g_ref, cos_ref, sin_ref, lg_ref, o_ref, r_ref, d_ref):
    t = RET_T
    dk = RET_DK
    half = dk // 2
    n_steps = q_ref.shape[1] // t
    lg = lg_ref[0]
    lg_k = lg[:, :dk]
    n = lax.broadcasted_iota(I32, (t, t), 0)
    m = lax.broadcasted_iota(I32, (t, t), 1)
    dist = jnp.abs(n - m).astype(F32)
    visible = (m // RET_CHUNK) <= (n // RET_CHUNK)
    k_scale = dk ** -0.5
    d_ref[...] = jnp.where(visible, k_scale * jnp.exp(dist * lg[:, :t]), 0.0)
    idx = lax.broadcasted_iota(I32, (t, dk), 0).astype(F32)
    q_decay = jnp.exp((idx + 1.0) * lg_k)
    k_decay = k_scale * jnp.exp((t - 1.0 - idx) * lg_k)
    step_decay = jnp.exp(float(t) * lg)
    r_ref[...] = jnp.zeros_like(r_ref)

    def rotate(x, cos, sin):
        x1, x2 = x[:, :half], x[:, half:]
        return jnp.concatenate([x1 * cos - x2 * sin, x2 * cos + x1 * sin], axis=-1)

    def step(c, carry):
        r0 = pl.multiple_of(c * t, t)
        rows = pl.ds(r0, t)
        cos = cos_ref[0, rows, :]
        sin = sin_ref[0, rows, :]
        q = rotate(q_ref[0, rows, :].astype(F32), cos, sin)
        k = rotate(k_ref[0, rows, :].astype(F32), cos, sin)
        v = v_ref[0, rows, :]
        g = g_ref[0, rows, :].astype(F32)
        r = r_ref[...]
        sc = _dot_nt(q.astype(BF16), k.astype(BF16)) * d_ref[...]
        o = _dot(sc.astype(BF16), v) + _dot((q * q_decay).astype(BF16), r.astype(BF16))
        r_ref[...] = r * step_decay + _dot_tn((k * k_decay).astype(BF16), v)
        out = _rms(o) * (g * _sigmoid(g))
        o_ref[0, rows, :] = out.astype(BF16)
        return carry

    lax.fori_loop(0, n_steps, step, 0, unroll=2)


def _retention(proj, cos, sin):
    bsz, s, w = proj.shape
    dk, dv = RET_DK, RET_DV
    nh = w // (2 * dk + 2 * dv)
    hidx = jnp.arange(nh, dtype=F32)
    log_gamma = jnp.log(1.0 - 2.0 ** (-5.0 - hidx))
    lg = jnp.broadcast_to(log_gamma[:, None, None], (nh, 1, dv))
    vbase = 2 * nh * dk // dv
    return pl.pallas_call(
        _ret_kernel,
        grid=(bsz, nh),
        in_specs=[
            pl.BlockSpec((1, s, dk), lambda b, h: (b, 0, h)),
            pl.BlockSpec((1, s, dk), lambda b, h: (b, 0, nh + h)),
            pl.BlockSpec((1, s, dv), lambda b, h: (b, 0, vbase + h)),
            pl.BlockSpec((1, s, dv), lambda b, h: (b, 0, vbase + nh + h)),
            pl.BlockSpec((1, s, dk // 2), lambda b, h: (b, 0, 0)),
            pl.BlockSpec((1, s, dk // 2), lambda b, h: (b, 0, 0)),
            pl.BlockSpec((1, 1, dv), lambda b, h: (h, 0, 0)),
        ],
        out_specs=pl.BlockSpec((1, s, dv), lambda b, h: (b, 0, h)),
        out_shape=jax.ShapeDtypeStruct((bsz, s, nh * dv), BF16),
        scratch_shapes=[pltpu.VMEM((dk, dv), F32), pltpu.VMEM((RET_T, RET_T), F32)],
        compiler_params=pltpu.CompilerParams(
            dimension_semantics=("parallel", "parallel"), vmem_limit_bytes=V7X_VMEM_LIMIT),
        name="retention",
    )(proj, proj, proj, proj, cos, sin, lg)


def _outproj_kernel(o_ref, w_ref, x_ref, g1_ref, sh_ref, sc_ref, gain_ref, rw_ref, rb_ref, u_ref,
                    xo_ref, ha_ref, hb_ref, eidx_ref, gate_ref, rank_ref, cnt_ref, base_ref):
    first = jnp.logical_and(pl.program_id(0) == 0, pl.program_id(1) == 0)

    @pl.when(first)
    def _():
        base_ref[...] = jnp.zeros_like(base_ref)

    ne = rw_ref.shape[0]
    rw = rw_ref[...]
    rw_hi = rw.astype(BF16)
    rw_lo = (rw - rw_hi.astype(F32)).astype(BF16)
    rw_both = jnp.concatenate([rw_hi, rw_lo], axis=0)
    sub = u_ref.shape[0]
    eiota = lax.broadcasted_iota(I32, (ne, sub), 0)
    base = base_ref[...]
    for r in range(0, x_ref.shape[1], sub):
        rows = slice(r, r + sub)
        y = _dot(o_ref[0, rows, :], w_ref[...])
        xn = x_ref[0, rows, :] + g1_ref[0] * y
        xo_ref[0, rows, :] = xn
        h = _rms(xn) * gain_ref[...] * (1.0 + sc_ref[0]) + sh_ref[0]
        packed = _pack_rows(h)
        slab = packed.shape[1] // ROW_PARTS
        ha_ref[0, rows, :] = packed[:, :slab]
        hb_ref[0, rows, :] = packed[:, slab:]

        part = _dot_nt(rw_both, h.astype(BF16))
        work = part[:ne] + part[ne:] + rb_ref[...]
        onehots, tops = [], []
        for k in range(TOP_K):
            mx = jnp.max(work, axis=0, keepdims=True)
            idx = jnp.min(jnp.where(work == mx, eiota, ne), axis=0, keepdims=True)
            oh = eiota == idx
            work = jnp.where(oh, -jnp.inf, work)
            eidx_ref[k:k + 1, rows] = idx
            onehots.append(oh)
            tops.append(mx)
        ex = [jnp.exp(m - tops[0]) for m in tops]
        denom = ex[0] + ex[1] + ex[2] + ex[3]
        for k in range(TOP_K):
            gate_ref[k:k + 1, rows] = ex[k] / denom

        mask = jnp.zeros((ne, sub), F32)
        for oh in onehots:
            mask = mask + oh.astype(F32)
        incl = _dot(mask.astype(BF16), u_ref[...])
        excl = incl - mask + base
        for k in range(TOP_K):
            rk = jnp.sum(jnp.where(onehots[k], excl, 0.0), axis=0, keepdims=True)
            rank_ref[k:k + 1, rows] = rk.astype(I32)
        base = base + incl[:, sub - 1:sub]
    base_ref[...] = base
    cnt_ref[...] = base.astype(I32)


def _outproj_route(o, w_out, x, mod, gain2, router_wt, router_b):
    bsz, s, d = x.shape
    hv = o.shape[2]
    tm = min(ROW_TILE, s)
    n = bsz * s
    nt = s // tm
    ne = router_wt.shape[0]
    sub = min(OUTPROJ_SUB_ROWS, tm)
    upper = (jnp.arange(sub)[:, None] <= jnp.arange(sub)[None, :]).astype(BF16)

    def modspec(j):
        return pl.BlockSpec((1, 1, d), lambda b, i: (b, 0, j))

    tokspec = pl.BlockSpec((TOP_K, tm), lambda b, i: (0, b * nt + i))
    return pl.pallas_call(
        _outproj_kernel,
        grid=(bsz, nt),
        in_specs=[
            pl.BlockSpec((1, tm, hv), lambda b, i: (b, i, 0)),
            pl.BlockSpec((hv, d), lambda b, i: (0, 0)),
            pl.BlockSpec((1, tm, d), lambda b, i: (b, i, 0)),
            modspec(2), modspec(3), modspec(4),
            pl.BlockSpec((1, d), lambda b, i: (0, 0)),
            pl.BlockSpec((ne, d), lambda b, i: (0, 0)),
            pl.BlockSpec((ne, 1), lambda b, i: (0, 0)),
            pl.BlockSpec((sub, sub), lambda b, i: (0, 0)),
        ],
        out_specs=[
            pl.BlockSpec((1, tm, d), lambda b, i: (b, i, 0)),
            pl.BlockSpec((1, tm, d // 4), lambda b, i: (b, i, 0)),
            pl.BlockSpec((1, tm, d // 4), lambda b, i: (b, i, 0)),
            tokspec, tokspec, tokspec,
            pl.BlockSpec((ne, 1), lambda b, i: (0, 0)),
        ],
        out_shape=[
            jax.ShapeDtypeStruct((bsz, s, d), F32),
            jax.ShapeDtypeStruct((bsz, s, d // 4), I32),
            jax.ShapeDtypeStruct((bsz, s, d // 4), I32),
            jax.ShapeDtypeStruct((TOP_K, n), I32),
            jax.ShapeDtypeStruct((TOP_K, n), F32),
            jax.ShapeDtypeStruct((TOP_K, n), I32),
            jax.ShapeDtypeStruct((ne, 1), I32),
        ],
        scratch_shapes=[pltpu.VMEM((ne, 1), F32)],
        compiler_params=pltpu.CompilerParams(
            dimension_semantics=("arbitrary", "arbitrary"), vmem_limit_bytes=V7X_VMEM_LIMIT),
        name="outproj_route",
    )(o, w_out, x, mod, mod, mod, gain2, router_wt, router_b, upper)


def _moe_kernel(be_ref, nb_ref, xa_ref, xb_ref, w1_ref, b1_ref, w2_ref, b2_ref, ya_ref, yb_ref,
                w1_s, w2_s):
    i = pl.program_id(0)
    used = i < nb_ref[0]
    new_expert = jnp.logical_or(i == 0, be_ref[i] != be_ref[jnp.maximum(i - 1, 0)])

    @pl.when(jnp.logical_and(used, new_expert))
    def _():
        for w_ref, w_s in ((w1_ref, w1_s), (w2_ref, w2_s)):
            for r in range(0, w_s.shape[0], WEIGHT_CAST_ROWS):
                rows = slice(r, r + WEIGHT_CAST_ROWS)
                w_s[rows, :] = w_ref[0, 0, rows, :].astype(BF16)

    @pl.when(used)
    def _():
        f = w2_s.shape[0]
        for r in range(0, xa_ref.shape[0], MOE_SUB_ROWS):
            rows = slice(r, r + MOE_SUB_ROWS)
            x = _unpack_rows(jnp.concatenate([xa_ref[rows, :], xb_ref[rows, :]], axis=1)).astype(BF16)
            u = _dot(x, w1_s[...]) + b1_ref[0, 0]
            glu = jnp.minimum(u[:, :f], SWIGLU_LIMIT)
            lin = jnp.clip(u[:, f:], -SWIGLU_LIMIT, SWIGLU_LIMIT)
            a = glu * _sigmoid(SWIGLU_ALPHA * glu) * (lin + 1.0)
            y = _dot(a.astype(BF16), w2_s[...]) + b2_ref[0, 0]
            packed = _pack_rows(y)
            slab = packed.shape[1] // ROW_PARTS
            ya_ref[rows, :] = packed[:, :slab]
            yb_ref[rows, :] = packed[:, slab:]

    @pl.when(jnp.logical_not(used))
    def _():
        ya_ref[...] = jnp.zeros_like(ya_ref)
        yb_ref[...] = jnp.zeros_like(yb_ref)


def _moe_blocks(block_e, n_used, xs, layer, w1, b1, w2, b2):
    n_rows, dh = xs[0].shape
    nl, ne, d, f2 = w1.shape
    f = f2 // 2
    nblk = n_rows // MOE_ROWS
    grid_spec = pltpu.PrefetchScalarGridSpec(
        num_scalar_prefetch=2,
        grid=(nblk,),
        in_specs=[
            pl.BlockSpec((MOE_ROWS, dh), lambda i, be, nb: (i, 0)),
            pl.BlockSpec((MOE_ROWS, dh), lambda i, be, nb: (i, 0)),
            pl.BlockSpec((1, 1, d, f2), lambda i, be, nb: (layer, be[i], 0, 0)),
            pl.BlockSpec((1, 1, 1, f2), lambda i, be, nb: (layer, be[i], 0, 0)),
            pl.BlockSpec((1, 1, f, d), lambda i, be, nb: (layer, be[i], 0, 0)),
            pl.BlockSpec((1, 1, 1, d), lambda i, be, nb: (layer, be[i], 0, 0)),
        ],
        out_specs=[pl.BlockSpec((MOE_ROWS, dh), lambda i, be, nb: (i, 0))] * 2,
        scratch_shapes=[pltpu.VMEM((d, f2), BF16), pltpu.VMEM((f, d), BF16)],
    )
    return pl.pallas_call(
        _moe_kernel,
        grid_spec=grid_spec,
        out_shape=[jax.ShapeDtypeStruct((n_rows, dh), I32)] * 2,
        compiler_params=pltpu.CompilerParams(
            dimension_semantics=("arbitrary",), vmem_limit_bytes=V7X_VMEM_LIMIT),
        name="moe_experts",
    )(block_e, n_used, xs[0], xs[1], w1, b1.reshape(nl, ne, 1, f2), w2, b2.reshape(nl, ne, 1, d))


def _sc_mesh():
    return plsc.VectorSubcoreMesh(core_axis_name="c", subcore_axis_name="s")


def _sc_scatter_rows(srcs, dests, n_rows):
    n, w = srcs[0].shape
    ns, nk = len(srcs), len(dests)
    out = jax.ShapeDtypeStruct((n_rows, w), srcs[0].dtype)

    @functools.partial(pl.kernel, out_type=[out] * ns, mesh=_sc_mesh(),
                       scratch_types=[pltpu.SemaphoreType.DMA((nk,))])
    def scatter_kernel(*refs):
        x_hbm, idx_hbm, o_hbm, sems = refs[:ns], refs[ns:ns + nk], refs[ns + nk:-1], refs[-1]
        for x, o in zip(x_hbm, o_hbm):
            def body(x_vmem, *idx_vmem, o=o):
                copies = [pltpu.async_copy(x_vmem, o.at[iv.at[0]], sems.at[k])
                          for k, iv in enumerate(idx_vmem)]
                for cp in copies:
                    cp.wait()

            pltpu.emit_pipeline(
                body,
                grid=(n // SC_WINDOW,),
                in_specs=[pl.BlockSpec((SC_WINDOW, w), lambda i: (i, 0))]
                + [pl.BlockSpec((1, SC_WINDOW), lambda i: (0, i))] * nk,
                out_specs=[],
                core_axis_name=("c", "s"),
                dimension_semantics=(pltpu.PARALLEL,),
            )(x, *idx_hbm)

    return scatter_kernel(*srcs, *dests)


def _sc_gather_rows(tables, idx):
    m = idx.shape[1]
    w = tables[0].shape[1]
    nt = len(tables)
    out = jax.ShapeDtypeStruct((m, w), tables[0].dtype)

    @functools.partial(pl.kernel, out_type=[out] * nt, mesh=_sc_mesh(), scratch_types=[])
    def gather_kernel(*refs):
        t_hbm, i_hbm, o_hbm = refs[:nt], refs[nt], refs[nt + 1:]
        for t, o in zip(t_hbm, o_hbm):
            def body(i_vmem, o_vmem, t=t):
                pltpu.sync_copy(t.at[i_vmem.at[0]], o_vmem)

            pltpu.emit_pipeline(
                body,
                grid=(m // SC_WINDOW,),
                in_specs=[pl.BlockSpec((1, SC_WINDOW), lambda i: (0, i))],
                out_specs=[pl.BlockSpec((SC_WINDOW, w), lambda i: (i, 0))],
                core_axis_name=("c", "s"),
                dimension_semantics=(pltpu.PARALLEL,),
            )(i_hbm, o)

    return gather_kernel(*tables, idx)


def _dest_kernel(ps_ref, eidx_ref, rank_ref, o_ref):
    eidx = eidx_ref[...]
    start = jnp.zeros_like(eidx)
    for e in range(N_EXPERTS):
        start = jnp.where(eidx == e, ps_ref[e], start)
    o_ref[...] = start + rank_ref[...]


def _dest_rows(pad_start, eidx, rank):
    k, n = eidx.shape
    tn = min(8192, n)
    blk = pl.BlockSpec((k, tn), lambda i, ps: (0, i))
    return pl.pallas_call(
        _dest_kernel,
        grid_spec=pltpu.PrefetchScalarGridSpec(
            num_scalar_prefetch=1, grid=(n // tn,), in_specs=[blk, blk], out_specs=blk),
        out_shape=jax.ShapeDtypeStruct((k, n), I32),
        name="dest_rows",
    )(pad_start, eidx, rank)


def _final_kernel(x_ref, *refs):
    g_ref, sh_ref, sc_ref, o_ref = refs[N_MOE_REFS:]
    x = _moe_residual(x_ref, refs[:N_MOE_REFS])
    o_ref[0] = _rms(x) * g_ref[...] * (1.0 + sc_ref[0]) + sh_ref[0]


def _final(x, pending, gain, fmod):
    bsz, s, d = x.shape
    tm = min(ROW_TILE, s)
    blk = pl.BlockSpec((1, tm, d), lambda b, i: (b, i, 0))
    moe_specs, moe_args = _moe_residual_operands(pending, bsz, s, d, tm, lambda b, i: (b, i))
    return pl.pallas_call(
        _final_kernel,
        grid=(bsz, s // tm),
        in_specs=[blk] + moe_specs + [
            pl.BlockSpec((1, d), lambda b, i: (0, 0)),
            pl.BlockSpec((1, 1, d), lambda b, i: (b, 0, 0)),
            pl.BlockSpec((1, 1, d), lambda b, i: (b, 0, 1))],
        out_specs=blk,
        out_shape=jax.ShapeDtypeStruct((bsz, s, d), F32),
        compiler_params=pltpu.CompilerParams(dimension_semantics=("parallel", "parallel")),
        name="final_norm",
    )(x, *moe_args, gain, fmod, fmod)


def _moe_layer(h2, eidx, gate, rank, counts, layer, w1, b1, w2, b2):
    bsz, s, dh = h2[0].shape
    n = bsz * s
    ne = w1.shape[1]
    nblk = -(-(n * TOP_K) // MOE_ROWS) + ne
    n_rows = nblk * MOE_ROWS
    counts = counts[:, 0]
    padded = (counts + MOE_ROWS - 1) // MOE_ROWS * MOE_ROWS
    pad_end = jnp.cumsum(padded)
    pad_start = pad_end - padded
    block_start = jnp.arange(nblk, dtype=I32)[:, None] * MOE_ROWS
    block_e = jnp.minimum(jnp.sum(pad_end[None, :] <= block_start, axis=1), ne - 1).astype(I32)
    n_used = (pad_end[-1:] // MOE_ROWS).astype(I32)
    dest = _dest_rows(pad_start.astype(I32), eidx, rank)
    xs = _sc_scatter_rows([h.reshape(n, dh) for h in h2], [dest[k:k + 1] for k in range(TOP_K)], n_rows)
    ys = _moe_blocks(block_e, n_used, xs, layer, w1, b1, w2, b2)
    yg = _sc_gather_rows(ys, dest.reshape(1, TOP_K * n))
    return yg, gate.T


def _hgrn_lower_bounds(lb_logits):
    p = jax.nn.softmax(lb_logits.astype(F32), axis=0)
    cum = jnp.cumsum(p, axis=0)
    return cum - cum[0:1]


def kernel(x, c, positions, ada_w, ada_b, norm1_g, norm2_g, hgrn_w_in, hgrn_w_out, hgrn_o_gain, hgrn_lb_logits, ret_w_in, ret_w_out, router_w, router_b, moe_w1, moe_b1, moe_w2, moe_b2, final_g, final_ada_w, final_ada_b):
    depth = ada_w.shape[0]
    mods = _ada(c, ada_w, ada_b)
    fmod = _ada(c, final_ada_w[None], final_ada_b[None])[0][:, None, :]
    lbs = _hgrn_lower_bounds(hgrn_lb_logits)
    cos, sin = _rope_tables(positions)
    pending = None
    for layer in range(depth):
        mod = mods[layer][:, None, :]
        j = layer // N_MIXERS
        if layer % N_MIXERS == 0:
            x, proj = _inproj(x, pending, norm1_g[layer][None], mod, hgrn_w_in[j].astype(BF16))
            o = _hgrn(proj, lbs[j][None], hgrn_o_gain[j][None])
            w_out = hgrn_w_out[j]
        else:
            x, proj = _inproj(x, pending, norm1_g[layer][None], mod, ret_w_in[j].astype(BF16))
            o = _retention(proj, cos, sin)
            w_out = ret_w_out[j]
        x, ha, hb, eidx, gate, rank, counts = _outproj_route(
            o, w_out.astype(BF16), x, mod, norm2_g[layer][None],
            router_w[layer].T, router_b[layer][:, None])
        yg, gate_rows = _moe_layer((ha, hb), eidx, gate, rank, counts, layer, moe_w1, moe_b1, moe_w2, moe_b2)
        pending = (yg, gate_rows, mod)
    return _final(x, pending, final_g[None], fmod)
```

```python
import functools

import jax
import jax.numpy as jnp
from jax import lax
from jax.experimental import pallas as pl
from jax.experimental.pallas import tpu as pltpu
from jax.experimental.pallas import tpu_sc as plsc

F32 = jnp.float32
BF16 = jnp.bfloat16
I32 = jnp.int32
HIGHEST = lax.Precision.HIGHEST

EPS = 1e-6
N_MIXERS = 2
HG_DK = 128
HG_T = 64
HG_PAIR = 2
HG_SUB = 16
HG_MAX_HALF_RANGE = 80.0
RET_DK = 256
RET_DV = 512
RET_CHUNK = 64
RET_T = 256
ROPE_BASE = 10000.0
N_EXPERTS = 32
TOP_K = 4
SWIGLU_ALPHA = 1.702
SWIGLU_LIMIT = 7.0
MOE_ROWS = 1024
MOE_SUB_ROWS = 512
WEIGHT_CAST_ROWS = 128
ROW_TILE = 512
INPROJ_SUB_ROWS = 256
OUTPROJ_SUB_ROWS = 256
SC_WINDOW = 128
ROW_PARTS = 2
V7X_VMEM_LIMIT = 56 * 1024 * 1024


def _dot(a, b):
    return jnp.dot(a, b, preferred_element_type=F32)


def _dot_nt(a, b, precision=None):
    return lax.dot_general(a, b, (((1,), (1,)), ((), ())), precision=precision,
                           preferred_element_type=F32)


def _dot_tn(a, b):
    return lax.dot_general(a, b, (((0,), (0,)), ((), ())), preferred_element_type=F32)


def _rms(x):
    return x * lax.rsqrt(jnp.mean(x * x, axis=-1, keepdims=True) + EPS)


def _sigmoid(x):
    return 1.0 / (1.0 + jnp.exp(-x))


def _pack_rows(h):
    half = h.shape[1] // 2
    a = lax.bitcast_convert_type(h[:, :half].astype(BF16).astype(F32), jnp.uint32)
    b = lax.bitcast_convert_type(h[:, half:].astype(BF16).astype(F32), jnp.uint32)
    return lax.bitcast_convert_type(a | (b >> 16), I32)


def _unpack_rows(w):
    u = lax.bitcast_convert_type(w, jnp.uint32)
    a = lax.bitcast_convert_type(u & jnp.uint32(0xFFFF0000), F32)
    b = lax.bitcast_convert_type(u << 16, F32)
    return jnp.concatenate([a, b], axis=1)


def _ada_kernel(c_ref, w_ref, b_ref, o_ref):
    c = c_ref[...]
    cond = c * _sigmoid(c)
    o_ref[0] = jnp.dot(cond, w_ref[0], precision=HIGHEST, preferred_element_type=F32) + b_ref[0]


def _ada(c, w, b):
    nl, d, kd = w.shape
    bsz = c.shape[0]
    return pl.pallas_call(
        _ada_kernel,
        grid=(nl, kd // d),
        in_specs=[
            pl.BlockSpec((bsz, d), lambda l, j: (0, 0)),
            pl.BlockSpec((1, d, d), lambda l, j: (l, 0, j)),
            pl.BlockSpec((1, 1, d), lambda l, j: (l, 0, j)),
        ],
        out_specs=pl.BlockSpec((1, bsz, d), lambda l, j: (l, 0, j)),
        out_shape=jax.ShapeDtypeStruct((nl, bsz, kd), F32),
        name="ada_mod",
    )(c, w, b.reshape(nl, 1, kd))


def _moe_residual(x_ref, refs, rows=slice(None)):
    ya_refs, yb_refs = refs[:TOP_K], refs[TOP_K:2 * TOP_K]
    gate_ref, g2_ref = refs[2 * TOP_K:]
    gate = gate_ref[rows, :]
    acc = None
    for k in range(TOP_K):
        packed = jnp.concatenate([ya_refs[k][rows, :], yb_refs[k][rows, :]], axis=1)
        term = gate[:, k:k + 1] * _unpack_rows(packed)
        acc = term if acc is None else acc + term
    return x_ref[0, rows, :] + g2_ref[0] * acc


N_MOE_REFS = 2 * TOP_K + 2


def _moe_residual_operands(pending, bsz, s, d, tm, tile):
    yg, gate_rows, mod = pending
    nt = s // tm
    ntok = (bsz * s) // tm

    def flat(*g):
        b, i = tile(*g)
        return b * nt + i

    def yspec(k):
        return pl.BlockSpec((tm, d // 4), lambda *g: (k * ntok + flat(*g), 0))

    specs = [yspec(k) for k in range(TOP_K)] * 2 + [
        pl.BlockSpec((tm, TOP_K), lambda *g: (flat(*g), 0)),
        pl.BlockSpec((1, 1, d), lambda *g: (tile(*g)[0], 0, 5))]
    return specs, [yg[0]] * TOP_K + [yg[1]] * TOP_K + [gate_rows, mod]


def _inproj_kernel(x_ref, *refs, col_chunk, fused):
    if fused:
        g_ref, sh_ref, sc_ref, w_ref, o_ref, xo_ref = refs[N_MOE_REFS:]
    else:
        g_ref, sh_ref, sc_ref, w_ref, o_ref = refs
    nout = w_ref.shape[1]
    tm = x_ref.shape[1]
    sub = min(INPROJ_SUB_ROWS, tm)
    for r in range(0, tm, sub):
        rows = slice(r, r + sub)
        if fused:
            x = _moe_residual(x_ref, refs[:N_MOE_REFS], rows)
            xo_ref[0, rows, :] = x
        else:
            x = x_ref[0, rows, :]
        h = _rms(x) * g_ref[...] * (1.0 + sc_ref[0]) + sh_ref[0]
        hb = h.astype(BF16)
        for j in range(nout // col_chunk):
            cs = slice(j * col_chunk, (j + 1) * col_chunk)
            o_ref[0, rows, cs] = _dot(hb, w_ref[:, cs]).astype(BF16)


def _inproj(x, pending, gain, mod, w):
    bsz, s, d = x.shape
    nout = w.shape[1]
    tm = min(ROW_TILE, s)
    xspec = pl.BlockSpec((1, tm, d), lambda b, i: (b, i, 0))
    fused = pending is not None
    moe_specs, moe_args = (_moe_residual_operands(pending, bsz, s, d, tm, lambda b, i: (b, i))
                           if fused else ([], []))
    proj_spec = pl.BlockSpec((1, tm, nout), lambda b, i: (b, i, 0))
    proj_shape = jax.ShapeDtypeStruct((bsz, s, nout), BF16)
    out = pl.pallas_call(
        functools.partial(_inproj_kernel, col_chunk=1024, fused=fused),
        grid=(bsz, s // tm),
        in_specs=[xspec] + moe_specs + [
            pl.BlockSpec((1, d), lambda b, i: (0, 0)),
            pl.BlockSpec((1, 1, d), lambda b, i: (b, 0, 0)),
            pl.BlockSpec((1, 1, d), lambda b, i: (b, 0, 1)),
            pl.BlockSpec((d, nout), lambda b, i: (0, 0), pipeline_mode=pl.Buffered(1)),
        ],
        out_specs=[proj_spec, xspec] if fused else proj_spec,
        out_shape=[proj_shape, jax.ShapeDtypeStruct((bsz, s, d), F32)] if fused else proj_shape,
        compiler_params=pltpu.CompilerParams(
            dimension_semantics=("parallel", "parallel"), vmem_limit_bytes=V7X_VMEM_LIMIT),
        name="inproj",
    )(x, *moe_args, gain, mod, mod, w)
    return (out[1], out[0]) if fused else (x, out)


def _hgrn_gates(q, f, lb, one_m_lb):
    e = jnp.exp(-jnp.abs(f))
    inv = 1.0 / (1.0 + e)
    pos = f >= 0.0
    t = e * inv
    sig = jnp.where(pos, inv, t)
    sig_neg = jnp.where(pos, t, inv)
    has_lb = lb > 0.0
    logf = jnp.log(jnp.where(has_lb, lb + one_m_lb * sig, inv)) + jnp.where(has_lb, 0.0, jnp.minimum(f, 0.0))
    return q * _sigmoid(q), one_m_lb * sig_neg, logf


def _hgrn_kernel(q_ref, f_ref, i_ref, g_ref, lb_ref, gain_ref, o_ref, kk_s, v_s, b_s,
                 qt_s, kt_s, qd_s, kd_s, dec_s):
    t = HG_T
    dk = HG_DK
    w = q_ref.shape[2]
    n_chunks = q_ref.shape[1] // t
    lb = lb_ref[...]
    one_m_lb = 1.0 - lb
    gain = gain_ref[...]
    heads = [slice(h * dk, (h + 1) * dk) for h in range(HG_PAIR)]
    row = lax.broadcasted_iota(I32, (t, HG_PAIR * t), 0)
    col = lax.broadcasted_iota(I32, (t, HG_PAIR * t), 1)
    causal = row >= (col % t)
    tril = causal[:, :t].astype(BF16)

    def finish(o, g):
        return (_rms(o) * gain * (g * _sigmoid(g))).astype(BF16)

    def block_diag(parts):
        rows = []
        for h, p in enumerate(parts):
            z = jnp.zeros_like(p)
            rows.append(jnp.concatenate([p if j == h else z for j in range(HG_PAIR)], axis=1))
        return jnp.concatenate(rows, axis=0)

    def prepare(c, bmax):
        rows = pl.ds(pl.multiple_of(c * t, t), t)
        q = q_ref[0, rows, :].astype(F32)
        f = f_ref[0, rows, :].astype(F32)
        qs, kk, logf = _hgrn_gates(q, f, lb, one_m_lb)
        hi = logf.astype(BF16)
        lo = (logf - hi.astype(F32)).astype(BF16)
        bb = _dot(tril, jnp.concatenate([hi, lo], axis=-1))
        b = bb[:, :w] + bb[:, w:]
        b_last = b[t - 1:t, :]
        mid = 0.5 * b_last
        e_mid = jnp.exp(mid)
        qt = qs * jnp.exp(b - mid)
        kt = kk * jnp.exp(mid - b)
        qt_s[rows, :] = qt.astype(BF16)
        kt_s[rows, :] = kt.astype(BF16)
        qd_s[rows, :] = (qt * e_mid).astype(BF16)
        kd_s[rows, :] = (kt * e_mid).astype(BF16)
        dec_s[pl.ds(pl.multiple_of(c * 8, 8), 8), :] = jnp.broadcast_to(e_mid * e_mid, (8, w))
        return jnp.maximum(bmax, jnp.abs(b_last))

    bmax = lax.fori_loop(0, n_chunks, prepare, jnp.zeros_like(lb), unroll=8)

    def chunk(c, sts):
        rows = pl.ds(pl.multiple_of(c * t, t), t)
        v = i_ref[0, rows, :]
        g = g_ref[0, rows, :].astype(F32)
        ktb = kt_s[rows, :]
        kd = kd_s[rows, :]
        decay = dec_s[pl.ds(pl.multiple_of(c * 8, 8), 1), :]
        att = _dot_nt(qt_s[rows, :], block_diag([ktb[:, hs] for hs in heads]))
        att = jnp.where(causal, att, 0.0).astype(BF16)
        o = _dot(att, block_diag([v[:, hs] for hs in heads]))
        st_bd = block_diag([st.astype(BF16) for st in sts])
        o = o + _dot_nt(qd_s[rows, :], st_bd)
        sts = tuple(st * decay[:, hs] + _dot_tn(v[:, hs], kd[:, hs]) for st, hs in zip(sts, heads))
        out = [finish(o[:, hs], g[:, hs]) for hs in heads]
        o_ref[0, rows, :] = jnp.concatenate(out, axis=1)
        return sts

    st0 = jnp.zeros((dk, dk), F32)
    lax.fori_loop(0, n_chunks, chunk, (st0,) * HG_PAIR, unroll=16)
    safe = 0.5 * jnp.max(bmax) <= HG_MAX_HALF_RANGE

    @pl.when(jnp.logical_not(safe))
    def _():
        n = HG_SUB
        sub_row = lax.broadcasted_iota(I32, (n, 1), 0)
        tril_n = (lax.broadcasted_iota(I32, (n, n), 0) >= lax.broadcasted_iota(I32, (n, n), 1)).astype(F32)

        for hs in heads:
            def block(i, st, hs=hs):
                rows = pl.ds(pl.multiple_of(i * n, n), n)
                q = q_ref[0, rows, hs].astype(F32)
                f = f_ref[0, rows, hs].astype(F32)
                v = i_ref[0, rows, hs]
                g = g_ref[0, rows, hs].astype(F32)
                qs, kk, logf = _hgrn_gates(q, f, lb[:, hs], one_m_lb[:, hs])
                b = jnp.dot(tril_n, logf, precision=HIGHEST, preferred_element_type=F32)
                kk_s[...] = kk
                v_s[...] = v.astype(F32)
                b_s[...] = b
                o = _dot_nt((qs * jnp.exp(b)).astype(BF16), st.astype(BF16))

                def pair(s, acc):
                    dec = jnp.exp(jnp.minimum(b - b_s[pl.ds(s, 1), :], 0.0))
                    wgt = jnp.sum(qs * kk_s[pl.ds(s, 1), :] * dec, axis=-1, keepdims=True)
                    return acc + jnp.where(sub_row >= s, wgt, 0.0) * v_s[pl.ds(s, 1), :]

                o = lax.fori_loop(0, n, pair, o)
                b_last = b[n - 1:n, :]
                st = st * jnp.exp(b_last) + _dot_tn(v, (kk * jnp.exp(b_last - b)).astype(BF16))
                o_ref[0, rows, hs] = finish(o, g)
                return st

            lax.fori_loop(0, q_ref.shape[1] // n, block, st0)


def _hgrn(proj, lb, gain):
    bsz, s, w4 = proj.shape
    w = HG_PAIR * HG_DK
    npair = w4 // (4 * w)

    def spec(j):
        return pl.BlockSpec((1, s, w), lambda b, p: (b, 0, p + j * npair))

    return pl.pallas_call(
        _hgrn_kernel,
        grid=(bsz, npair),
        in_specs=[spec(0), spec(1), spec(2), spec(3),
                  pl.BlockSpec((1, w), lambda b, p: (0, p)),
                  pl.BlockSpec((1, HG_DK), lambda b, p: (0, 0))],
        out_specs=pl.BlockSpec((1, s, w), lambda b, p: (b, 0, p)),
        out_shape=jax.ShapeDtypeStruct((bsz, s, npair * w), BF16),
        scratch_shapes=[pltpu.VMEM((HG_SUB, HG_DK), F32)] * 3 + [pltpu.VMEM((s, w), BF16)] * 4
        + [pltpu.VMEM((8 * (s // HG_T), w), F32)],
        compiler_params=pltpu.CompilerParams(dimension_semantics=("parallel", "parallel")),
        name="hgrn",
    )(proj, proj, proj, proj, lb, gain)


def _rope_kernel(pos_ref, inv_ref, cos_ref, sin_ref):
    ang = pos_ref[0].astype(F32) * inv_ref[...]
    cos_ref[0] = jnp.cos(ang)
    sin_ref[0] = jnp.sin(ang)


def _rope_tables(positions):
    bsz, s = positions.shape
    half = RET_DK // 2
    inv_freq = (1.0 / (ROPE_BASE ** jnp.linspace(0.0, 1.0, half, dtype=F32))).reshape(1, half)
    out = jax.ShapeDtypeStruct((bsz, s, half), F32)
    return pl.pallas_call(
        _rope_kernel,
        grid=(bsz,),
        in_specs=[pl.BlockSpec((1, s, 1), lambda b: (b, 0, 0)),
                  pl.BlockSpec((1, half), lambda b: (0, 0))],
        out_specs=[pl.BlockSpec((1, s, half), lambda b: (b, 0, 0))] * 2,
        out_shape=[out, out],
        name="rope_tables",
    )(positions.reshape(bsz, s, 1), inv_freq)


def _ret_kernel(q_ref, k_ref, v_ref, g_ref, cos_ref, sin_ref, lg_ref, o_ref, r_ref, d_ref):
    t = RET_T
    dk = RET_DK
    half = dk // 2
    n_steps = q_ref.shape[1] // t
    lg = lg_ref[0]
    lg_k = lg[:, :dk]
    n = lax.broadcasted_iota(I32, (t, t), 0)
    m = lax.broadcasted_iota(I32, (t, t), 1)
    dist = jnp.abs(n - m).astype(F32)
    visible = (m // RET_CHUNK) <= (n // RET_CHUNK)
    k_scale = dk ** -0.5
    d_ref[...] = jnp.where(visible, k_scale * jnp.exp(dist * lg[:, :t]), 0.0)
    idx = lax.broadcasted_iota(I32, (t, dk), 0).astype(F32)
    q_decay = jnp.exp((idx + 1.0) * lg_k)
    k_decay = k_scale * jnp.exp((t - 1.0 - idx) * lg_k)
    step_decay = jnp.exp(float(t) * lg)
    r_ref[...] = jnp.zeros_like(r_ref)

    def rotate(x, cos, sin):
        x1, x2 = x[:, :half], x[:, half:]
        return jnp.concatenate([x1 * cos - x2 * sin, x2 * cos + x1 * sin], axis=-1)

    def step(c, carry):
        r0 = pl.multiple_of(c * t, t)
        rows = pl.ds(r0, t)
        cos = cos_ref[0, rows, :]
        sin = sin_ref[0, rows, :]
        q = rotate(q_ref[0, rows, :].astype(F32), cos, sin)
        k = rotate(k_ref[0, rows, :].astype(F32), cos, sin)
        v = v_ref[0, rows, :]
        g = g_ref[0, rows, :].astype(F32)
        r = r_ref[...]
        sc = _dot_nt(q.astype(BF16), k.astype(BF16)) * d_ref[...]
        o = _dot(sc.astype(BF16), v) + _dot((q * q_decay).astype(BF16), r.astype(BF16))
        r_ref[...] = r * step_decay + _dot_tn((k * k_decay).astype(BF16), v)
        out = _rms(o) * (g * _sigmoid(g))
        o_ref[0, rows, :] = out.astype(BF16)
        return carry

    lax.fori_loop(0, n_steps, step, 0, unroll=2)


def _retention(proj, cos, sin):
    bsz, s, w = proj.shape
    dk, dv = RET_DK, RET_DV
    nh = w // (2 * dk + 2 * dv)
    hidx = jnp.arange(nh, dtype=F32)
    log_gamma = jnp.log(1.0 - 2.0 ** (-5.0 - hidx))
    lg = jnp.broadcast_to(log_gamma[:, None, None], (nh, 1, dv))
    vbase = 2 * nh * dk // dv
    return pl.pallas_call(
        _ret_kernel,
        grid=(bsz, nh),
        in_specs=[
            pl.BlockSpec((1, s, dk), lambda b, h: (b, 0, h)),
            pl.BlockSpec((1, s, dk), lambda b, h: (b, 0, nh + h)),
            pl.BlockSpec((1, s, dv), lambda b, h: (b, 0, vbase + h)),
            pl.BlockSpec((1, s, dv), lambda b, h: (b, 0, vbase + nh + h)),
            pl.BlockSpec((1, s, dk // 2), lambda b, h: (b, 0, 0)),
            pl.BlockSpec((1, s, dk // 2), lambda b, h: (b, 0, 0)),
            pl.BlockSpec((1, 1, dv), lambda b, h: (h, 0, 0)),
        ],
        out_specs=pl.BlockSpec((1, s, dv), lambda b, h: (b, 0, h)),
        out_shape=jax.ShapeDtypeStruct((bsz, s, nh * dv), BF16),
        scratch_shapes=[pltpu.VMEM((dk, dv), F32), pltpu.VMEM((RET_T, RET_T), F32)],
        compiler_params=pltpu.CompilerParams(
            dimension_semantics=("parallel", "parallel"), vmem_limit_bytes=V7X_VMEM_LIMIT),
        name="retention",
    )(proj, proj, proj, proj, cos, sin, lg)


def _outproj_kernel(o_ref, w_ref, x_ref, g1_ref, sh_ref, sc_ref, gain_ref, rw_ref, rb_ref, u_ref,
                    xo_ref, ha_ref, hb_ref, eidx_ref, gate_ref, rank_ref, cnt_ref, base_ref):
    first = jnp.logical_and(pl.program_id(0) == 0, pl.program_id(1) == 0)

    @pl.when(first)
    def _():
        base_ref[...] = jnp.zeros_like(base_ref)

    ne = rw_ref.shape[0]
    rw = rw_ref[...]
    rw_hi = rw.astype(BF16)
    rw_lo = (rw - rw_hi.astype(F32)).astype(BF16)
    rw_both = jnp.concatenate([rw_hi, rw_lo], axis=0)
    tm = x_ref.shape[1]
    sub = min(OUTPROJ_SUB_ROWS, tm)
    h_groups = []
    for r in range(0, tm, sub):
        rows = slice(r, r + sub)
        y = _dot(o_ref[0, rows, :], w_ref[...])
        xn = x_ref[0, rows, :] + g1_ref[0] * y
        xo_ref[0, rows, :] = xn
        h = _rms(xn) * gain_ref[...] * (1.0 + sc_ref[0]) + sh_ref[0]
        packed = _pack_rows(h)
        slab = packed.shape[1] // ROW_PARTS
        ha_ref[0, rows, :] = packed[:, :slab]
        hb_ref[0, rows, :] = packed[:, slab:]
        h_groups.append(h.astype(BF16))

    part = _dot_nt(rw_both, jnp.concatenate(h_groups, axis=0))
    work = part[:ne] + part[ne:] + rb_ref[...]
    eiota = lax.broadcasted_iota(I32, (ne, tm), 0)
    onehots, tops = [], []
    for k in range(TOP_K):
        mx = jnp.max(work, axis=0, keepdims=True)
        idx = jnp.min(jnp.where(work == mx, eiota, ne), axis=0, keepdims=True)
        oh = eiota == idx
        work = jnp.where(oh, -jnp.inf, work)
        eidx_ref[k:k + 1, :] = idx
        onehots.append(oh)
        tops.append(mx)
    ex = [jnp.exp(m - tops[0]) for m in tops]
    denom = ex[0] + ex[1] + ex[2] + ex[3]
    for k in range(TOP_K):
        gate_ref[k:k + 1, :] = ex[k] / denom

    mask = jnp.zeros((ne, tm), F32)
    for oh in onehots:
        mask = mask + oh.astype(F32)
    incl = _dot(mask.astype(BF16), u_ref[...])
    excl = incl - mask + base_ref[...]
    for k in range(TOP_K):
        rk = jnp.sum(jnp.where(onehots[k], excl, 0.0), axis=0, keepdims=True)
        rank_ref[k:k + 1, :] = rk.astype(I32)
    total = base_ref[...] + incl[:, tm - 1:tm]
    base_ref[...] = total
    cnt_ref[...] = total.astype(I32)


def _outproj_route(o, w_out, x, mod, gain2, router_wt, router_b):
    bsz, s, d = x.shape
    hv = o.shape[2]
    tm = min(ROW_TILE, s)
    n = bsz * s
    nt = s // tm
    ne = router_wt.shape[0]
    upper = (jnp.arange(tm)[:, None] <= jnp.arange(tm)[None, :]).astype(BF16)

    def modspec(j):
        return pl.BlockSpec((1, 1, d), lambda b, i: (b, 0, j))

    tokspec = pl.BlockSpec((TOP_K, tm), lambda b, i: (0, b * nt + i))
    return pl.pallas_call(
        _outproj_kernel,
        grid=(bsz, nt),
        in_specs=[
            pl.BlockSpec((1, tm, hv), lambda b, i: (b, i, 0)),
            pl.BlockSpec((hv, d), lambda b, i: (0, 0)),
            pl.BlockSpec((1, tm, d), lambda b, i: (b, i, 0)),
            modspec(2), modspec(3), modspec(4),
            pl.BlockSpec((1, d), lambda b, i: (0, 0)),
            pl.BlockSpec((ne, d), lambda b, i: (0, 0)),
            pl.BlockSpec((ne, 1), lambda b, i: (0, 0)),
            pl.BlockSpec((tm, tm), lambda b, i: (0, 0)),
        ],
        out_specs=[
            pl.BlockSpec((1, tm, d), lambda b, i: (b, i, 0)),
            pl.BlockSpec((1, tm, d // 4), lambda b, i: (b, i, 0)),
            pl.BlockSpec((1, tm, d // 4), lambda b, i: (b, i, 0)),
            tokspec, tokspec, tokspec,
            pl.BlockSpec((ne, 1), lambda b, i: (0, 0)),
        ],
        out_shape=[
            jax.ShapeDtypeStruct((bsz, s, d), F32),
            jax.ShapeDtypeStruct((bsz, s, d // 4), I32),
            jax.ShapeDtypeStruct((bsz, s, d // 4), I32),
            jax.ShapeDtypeStruct((TOP_K, n), I32),
            jax.ShapeDtypeStruct((TOP_K, n), F32),
            jax.ShapeDtypeStruct((TOP_K, n), I32),
            jax.ShapeDtypeStruct((ne, 1), I32),
        ],
        scratch_shapes=[pltpu.VMEM((ne, 1), F32)],
        compiler_params=pltpu.CompilerParams(
            dimension_semantics=("arbitrary", "arbitrary"), vmem_limit_bytes=V7X_VMEM_LIMIT),
        name="outproj_route",
    )(o, w_out, x, mod, mod, mod, gain2, router_wt, router_b, upper)


def _moe_kernel(be_ref, nb_ref, xa_ref, xb_ref, w1_ref, b1_ref, w2_ref, b2_ref, ya_ref, yb_ref,
                w1_s, w2_s):
    i = pl.program_id(0)
    used = i < nb_ref[0]
    new_expert = jnp.logical_or(i == 0, be_ref[i] != be_ref[jnp.maximum(i - 1, 0)])

    @pl.when(jnp.logical_and(used, new_expert))
    def _():
        for w_ref, w_s in ((w1_ref, w1_s), (w2_ref, w2_s)):
            for r in range(0, w_s.shape[0], WEIGHT_CAST_ROWS):
                rows = slice(r, r + WEIGHT_CAST_ROWS)
                w_s[rows, :] = w_ref[0, 0, rows, :].astype(BF16)

    @pl.when(used)
    def _():
        f = w2_s.shape[0]
        for r in range(0, xa_ref.shape[0], MOE_SUB_ROWS):
            rows = slice(r, r + MOE_SUB_ROWS)
            x = _unpack_rows(jnp.concatenate([xa_ref[rows, :], xb_ref[rows, :]], axis=1)).astype(BF16)
            u = _dot(x, w1_s[...]) + b1_ref[0, 0]
            glu = jnp.minimum(u[:, :f], SWIGLU_LIMIT)
            lin = jnp.clip(u[:, f:], -SWIGLU_LIMIT, SWIGLU_LIMIT)
            a = glu * _sigmoid(SWIGLU_ALPHA * glu) * (lin + 1.0)
            y = _dot(a.astype(BF16), w2_s[...]) + b2_ref[0, 0]
            packed = _pack_rows(y)
            slab = packed.shape[1] // ROW_PARTS
            ya_ref[rows, :] = packed[:, :slab]
            yb_ref[rows, :] = packed[:, slab:]

    @pl.when(jnp.logical_not(used))
    def _():
        ya_ref[...] = jnp.zeros_like(ya_ref)
        yb_ref[...] = jnp.zeros_like(yb_ref)


def _moe_blocks(block_e, n_used, xs, layer, w1, b1, w2, b2):
    n_rows, dh = xs[0].shape
    nl, ne, d, f2 = w1.shape
    f = f2 // 2
    nblk = n_rows // MOE_ROWS
    grid_spec = pltpu.PrefetchScalarGridSpec(
        num_scalar_prefetch=2,
        grid=(nblk,),
        in_specs=[
            pl.BlockSpec((MOE_ROWS, dh), lambda i, be, nb: (i, 0)),
            pl.BlockSpec((MOE_ROWS, dh), lambda i, be, nb: (i, 0)),
            pl.BlockSpec((1, 1, d, f2), lambda i, be, nb: (layer, be[i], 0, 0)),
            pl.BlockSpec((1, 1, 1, f2), lambda i, be, nb: (layer, be[i], 0, 0)),
            pl.BlockSpec((1, 1, f, d), lambda i, be, nb: (layer, be[i], 0, 0)),
            pl.BlockSpec((1, 1, 1, d), lambda i, be, nb: (layer, be[i], 0, 0)),
        ],
        out_specs=[pl.BlockSpec((MOE_ROWS, dh), lambda i, be, nb: (i, 0))] * 2,
        scratch_shapes=[pltpu.VMEM((d, f2), BF16), pltpu.VMEM((f, d), BF16)],
    )
    return pl.pallas_call(
        _moe_kernel,
        grid_spec=grid_spec,
        out_shape=[jax.ShapeDtypeStruct((n_rows, dh), I32)] * 2,
        compiler_params=pltpu.CompilerParams(
            dimension_semantics=("arbitrary",), vmem_limit_bytes=V7X_VMEM_LIMIT),
        name="moe_experts",
    )(block_e, n_used, xs[0], xs[1], w1, b1.reshape(nl, ne, 1, f2), w2, b2.reshape(nl, ne, 1, d))


def _sc_mesh():
    return plsc.VectorSubcoreMesh(core_axis_name="c", subcore_axis_name="s")


def _sc_scatter_rows(srcs, dests, n_rows):
    n, w = srcs[0].shape
    ns, nk = len(srcs), len(dests)
    out = jax.ShapeDtypeStruct((n_rows, w), srcs[0].dtype)

    @functools.partial(pl.kernel, out_type=[out] * ns, mesh=_sc_mesh(),
                       scratch_types=[pltpu.SemaphoreType.DMA((nk,))])
    def scatter_kernel(*refs):
        x_hbm, idx_hbm, o_hbm, sems = refs[:ns], refs[ns:ns + nk], refs[ns + nk:-1], refs[-1]
        for x, o in zip(x_hbm, o_hbm):
            def body(x_vmem, *idx_vmem, o=o):
                copies = [pltpu.async_copy(x_vmem, o.at[iv.at[0]], sems.at[k])
                          for k, iv in enumerate(idx_vmem)]
                for cp in copies:
                    cp.wait()

            pltpu.emit_pipeline(
                body,
                grid=(n // SC_WINDOW,),
                in_specs=[pl.BlockSpec((SC_WINDOW, w), lambda i: (i, 0))]
                + [pl.BlockSpec((1, SC_WINDOW), lambda i: (0, i))] * nk,
                out_specs=[],
                core_axis_name=("c", "s"),
                dimension_semantics=(pltpu.PARALLEL,),
            )(x, *idx_hbm)

    return scatter_kernel(*srcs, *dests)


def _sc_gather_rows(tables, idx):
    m = idx.shape[1]
    w = tables[0].shape[1]
    nt = len(tables)
    out = jax.ShapeDtypeStruct((m, w), tables[0].dtype)

    @functools.partial(pl.kernel, out_type=[out] * nt, mesh=_sc_mesh(), scratch_types=[])
    def gather_kernel(*refs):
        t_hbm, i_hbm, o_hbm = refs[:nt], refs[nt], refs[nt + 1:]
        for t, o in zip(t_hbm, o_hbm):
            def body(i_vmem, o_vmem, t=t):
                pltpu.sync_copy(t.at[i_vmem.at[0]], o_vmem)

            pltpu.emit_pipeline(
                body,
                grid=(m // SC_WINDOW,),
                in_specs=[pl.BlockSpec((1, SC_WINDOW), lambda i: (0, i))],
                out_specs=[pl.BlockSpec((SC_WINDOW, w), lambda i: (i, 0))],
                core_axis_name=("c", "s"),
                dimension_semantics=(pltpu.PARALLEL,),
            )(i_hbm, o)

    return gather_kernel(*tables, idx)


def _dest_kernel(ps_ref, eidx_ref, rank_ref, o_ref):
    eidx = eidx_ref[...]
    start = jnp.zeros_like(eidx)
    for e in range(N_EXPERTS):
        start = jnp.where(eidx == e, ps_ref[e], start)
    o_ref[...] = start + rank_ref[...]


def _dest_rows(pad_start, eidx, rank):
    k, n = eidx.shape
    tn = min(8192, n)
    blk = pl.BlockSpec((k, tn), lambda i, ps: (0, i))
    return pl.pallas_call(
        _dest_kernel,
        grid_spec=pltpu.PrefetchScalarGridSpec(
            num_scalar_prefetch=1, grid=(n // tn,), in_specs=[blk, blk], out_specs=blk),
        out_shape=jax.ShapeDtypeStruct((k, n), I32),
        name="dest_rows",
    )(pad_start, eidx, rank)


def _final_kernel(x_ref, *refs):
    g_ref, sh_ref, sc_ref, o_ref = refs[N_MOE_REFS:]
    x = _moe_residual(x_ref, refs[:N_MOE_REFS])
    o_ref[0] = _rms(x) * g_ref[...] * (1.0 + sc_ref[0]) + sh_ref[0]


def _final(x, pending, gain, fmod):
    bsz, s, d = x.shape
    tm = min(ROW_TILE, s)
    blk = pl.BlockSpec((1, tm, d), lambda b, i: (b, i, 0))
    moe_specs, moe_args = _moe_residual_operands(pending, bsz, s, d, tm, lambda b, i: (b, i))
    return pl.pallas_call(
        _final_kernel,
        grid=(bsz, s // tm),
        in_specs=[blk] + moe_specs + [
            pl.BlockSpec((1, d), lambda b, i: (0, 0)),
            pl.BlockSpec((1, 1, d), lambda b, i: (b, 0, 0)),
            pl.BlockSpec((1, 1, d), lambda b, i: (b, 0, 1))],
        out_specs=blk,
        out_shape=jax.ShapeDtypeStruct((bsz, s, d), F32),
        compiler_params=pltpu.CompilerParams(dimension_semantics=("parallel", "parallel")),
        name="final_norm",
    )(x, *moe_args, gain, fmod, fmod)


def _moe_layer(h2, eidx, gate, rank, counts, layer, w1, b1, w2, b2):
    bsz, s, dh = h2[0].shape
    n = bsz * s
    ne = w1.shape[1]
    nblk = -(-(n * TOP_K) // MOE_ROWS) + ne
    n_rows = nblk * MOE_ROWS
    counts = counts[:, 0]
    padded = (counts + MOE_ROWS - 1) // MOE_ROWS * MOE_ROWS
    pad_end = jnp.cumsum(padded)
    pad_start = pad_end - padded
    block_start = jnp.arange(nblk, dtype=I32)[:, None] * MOE_ROWS
    block_e = jnp.minimum(jnp.sum(pad_end[None, :] <= block_start, axis=1), ne - 1).astype(I32)
    n_used = (pad_end[-1:] // MOE_ROWS).astype(I32)
    dest = _dest_rows(pad_start.astype(I32), eidx, rank)
    xs = _sc_scatter_rows([h.reshape(n, dh) for h in h2], [dest[k:k + 1] for k in range(TOP_K)], n_rows)
    ys = _moe_blocks(block_e, n_used, xs, layer, w1, b1, w2, b2)
    yg = _sc_gather_rows(ys, dest.reshape(1, TOP_K * n))
    return yg, gate.T


def _hgrn_lower_bounds(lb_logits):
    p = jax.nn.softmax(lb_logits.astype(F32), axis=0)
    cum = jnp.cumsum(p, axis=0)
    return cum - cum[0:1]


def kernel(x, c, positions, ada_w, ada_b, norm1_g, norm2_g, hgrn_w_in, hgrn_w_out, hgrn_o_gain, hgrn_lb_logits, ret_w_in, ret_w_out, router_w, router_b, moe_w1, moe_b1, moe_w2, moe_b2, final_g, final_ada_w, final_ada_b):
    depth = ada_w.shape[0]
    mods = _ada(c, ada_w, ada_b)
    fmod = _ada(c, final_ada_w[None], final_ada_b[None])[0][:, None, :]
    lbs = _hgrn_lower_bounds(hgrn_lb_logits)
    cos, sin = _rope_tables(positions)
    pending = None
    for layer in range(depth):
        mod = mods[layer][:, None, :]
        j = layer // N_MIXERS
        if layer % N_MIXERS == 0:
            x, proj = _inproj(x, pending, norm1_g[layer][None], mod, hgrn_w_in[j].astype(BF16))
            o = _hgrn(proj, lbs[j][None], hgrn_o_gain[j][None])
            w_out = hgrn_w_out[j]
        else:
            x, proj = _inproj(x, pending, norm1_g[layer][None], mod, ret_w_in[j].astype(BF16))
            o = _retention(proj, cos, sin)
            w_out = ret_w_out[j]
        x, ha, hb, eidx, gate, rank, counts = _outproj_route(
            o, w_out.astype(BF16), x, mod, norm2_g[layer][None],
            router_w[layer].T, router_b[layer][:, None])
        yg, gate_rows = _moe_layer((ha, hb), eidx, gate, rank, counts, layer, moe_w1, moe_b1, moe_w2, moe_b2)
        pending = (yg, gate_rows, mod)
    return _final(x, pending, final_g[None], fmod)
```

```python
import functools

import jax
import jax.numpy as jnp
from jax import lax
from jax.experimental import pallas as pl
from jax.experimental.pallas import tpu as pltpu
from jax.experimental.pallas import tpu_sc as plsc

F32 = jnp.float32
BF16 = jnp.bfloat16
I32 = jnp.int32
HIGHEST = lax.Precision.HIGHEST

EPS = 1e-6
N_MIXERS = 2
HG_DK = 128
HG_T = 64
HG_PAIR = 2
HG_SUB = 16
HG_MAX_HALF_RANGE = 80.0
RET_DK = 256
RET_DV = 512
RET_CHUNK = 64
RET_T = 256
ROPE_BASE = 10000.0
TOP_K = 4
SWIGLU_ALPHA = 1.702
SWIGLU_LIMIT = 7.0
MOE_ROWS = 1024
MOE_SUB_ROWS = 512
WEIGHT_CAST_ROWS = 128
ROW_TILE = 512
INPROJ_SUB_ROWS = 256
OUTPROJ_ROW_TILE = 1024
OUTPROJ_SUB_ROWS = 256
SC_WINDOW = 128
ROW_PARTS = 2
V7X_VMEM_BYTES = 64 * 1024 * 1024
V7X_VMEM_LIMIT = V7X_VMEM_BYTES * 7 // 8


def _dot(a, b):
    return jnp.dot(a, b, preferred_element_type=F32)


def _dot_nt(a, b):
    return lax.dot_general(a, b, (((1,), (1,)), ((), ())), preferred_element_type=F32)


def _dot_tn(a, b):
    return lax.dot_general(a, b, (((0,), (0,)), ((), ())), preferred_element_type=F32)


def _rms(x):
    return x * lax.rsqrt(jnp.mean(x * x, axis=-1, keepdims=True) + EPS)


def _sigmoid(x):
    return 1.0 / (1.0 + jnp.exp(-x))


def _pack_rows(h):
    half = h.shape[1] // 2
    a = lax.bitcast_convert_type(h[:, :half].astype(BF16).astype(F32), jnp.uint32)
    b = lax.bitcast_convert_type(h[:, half:].astype(BF16).astype(F32), jnp.uint32)
    return lax.bitcast_convert_type(a | (b >> 16), I32)


def _unpack_rows(w):
    u = lax.bitcast_convert_type(w, jnp.uint32)
    a = lax.bitcast_convert_type(u & jnp.uint32(0xFFFF0000), F32)
    b = lax.bitcast_convert_type(u << 16, F32)
    return jnp.concatenate([a, b], axis=1)


def _ada_kernel(c_ref, w_ref, b_ref, o_ref):
    c = c_ref[...]
    cond = c * _sigmoid(c)
    o_ref[0] = jnp.dot(cond, w_ref[0], precision=HIGHEST, preferred_element_type=F32) + b_ref[0]


def _ada(c, w, b):
    nl, d, kd = w.shape
    bsz = c.shape[0]
    return pl.pallas_call(
        _ada_kernel,
        grid=(nl, kd // d),
        in_specs=[
            pl.BlockSpec((bsz, d), lambda l, j: (0, 0)),
            pl.BlockSpec((1, d, d), lambda l, j: (l, 0, j)),
            pl.BlockSpec((1, 1, d), lambda l, j: (l, 0, j)),
        ],
        out_specs=pl.BlockSpec((1, bsz, d), lambda l, j: (l, 0, j)),
        out_shape=jax.ShapeDtypeStruct((nl, bsz, kd), F32),
        name="ada_mod",
    )(c, w, b.reshape(nl, 1, kd))


def _moe_residual(x_ref, refs, rows=slice(None)):
    ya_refs, yb_refs = refs[:TOP_K], refs[TOP_K:2 * TOP_K]
    gate_ref, g2_ref = refs[2 * TOP_K:]
    gate = gate_ref[rows, :]
    acc = None
    for k in range(TOP_K):
        packed = jnp.concatenate([ya_refs[k][rows, :], yb_refs[k][rows, :]], axis=1)
        term = gate[:, k:k + 1] * _unpack_rows(packed)
        acc = term if acc is None else acc + term
    return x_ref[0, rows, :] + g2_ref[0] * acc


N_MOE_REFS = 2 * TOP_K + 2


def _moe_residual_operands(pending, bsz, s, d, tm, tile):
    yg, gate_rows, mod = pending
    nt = s // tm
    ntok = (bsz * s) // tm

    def flat(*g):
        b, i = tile(*g)
        return b * nt + i

    def yspec(k):
        return pl.BlockSpec((tm, d // 4), lambda *g: (k * ntok + flat(*g), 0))

    specs = [yspec(k) for k in range(TOP_K)] * 2 + [
        pl.BlockSpec((tm, TOP_K), lambda *g: (flat(*g), 0)),
        pl.BlockSpec((1, 1, d), lambda *g: (tile(*g)[0], 0, 5))]
    return specs, [yg[0]] * TOP_K + [yg[1]] * TOP_K + [gate_rows, mod]


def _inproj_kernel(x_ref, *refs, col_chunk, fused):
    if fused:
        g_ref, sh_ref, sc_ref, w_ref, o_ref, xo_ref = refs[N_MOE_REFS:]
    else:
        g_ref, sh_ref, sc_ref, w_ref, o_ref = refs
    nout = w_ref.shape[1]
    tm = x_ref.shape[1]
    sub = min(INPROJ_SUB_ROWS, tm)
    for r in range(0, tm, sub):
        rows = slice(r, r + sub)
        if fused:
            x = _moe_residual(x_ref, refs[:N_MOE_REFS], rows)
            xo_ref[0, rows, :] = x
        else:
            x = x_ref[0, rows, :]
        h = _rms(x) * g_ref[...] * (1.0 + sc_ref[0]) + sh_ref[0]
        hb = h.astype(BF16)
        for j in range(nout // col_chunk):
            cs = slice(j * col_chunk, (j + 1) * col_chunk)
            o_ref[0, rows, cs] = _dot(hb, w_ref[:, cs]).astype(BF16)


def _inproj(x, pending, gain, mod, w):
    bsz, s, d = x.shape
    nout = w.shape[1]
    tm = min(ROW_TILE, s)
    xspec = pl.BlockSpec((1, tm, d), lambda b, i: (b, i, 0))
    fused = pending is not None
    moe_specs, moe_args = (_moe_residual_operands(pending, bsz, s, d, tm, lambda b, i: (b, i))
                           if fused else ([], []))
    proj_spec = pl.BlockSpec((1, tm, nout), lambda b, i: (b, i, 0))
    proj_shape = jax.ShapeDtypeStruct((bsz, s, nout), BF16)
    out = pl.pallas_call(
        functools.partial(_inproj_kernel, col_chunk=1024, fused=fused),
        grid=(bsz, s // tm),
        in_specs=[xspec] + moe_specs + [
            pl.BlockSpec((1, d), lambda b, i: (0, 0)),
            pl.BlockSpec((1, 1, d), lambda b, i: (b, 0, 0)),
            pl.BlockSpec((1, 1, d), lambda b, i: (b, 0, 1)),
            pl.BlockSpec((d, nout), lambda b, i: (0, 0), pipeline_mode=pl.Buffered(1)),
        ],
        out_specs=[proj_spec, xspec] if fused else proj_spec,
        out_shape=[proj_shape, jax.ShapeDtypeStruct((bsz, s, d), F32)] if fused else proj_shape,
        compiler_params=pltpu.CompilerParams(
            dimension_semantics=("parallel", "parallel"), vmem_limit_bytes=V7X_VMEM_LIMIT),
        name="inproj",
    )(x, *moe_args, gain, mod, mod, w)
    return (out[1], out[0]) if fused else (x, out)


def _hgrn_gates(q, f, lb, one_m_lb):
    e = jnp.exp(-jnp.abs(f))
    inv = 1.0 / (1.0 + e)
    pos = f >= 0.0
    t = e * inv
    sig = jnp.where(pos, inv, t)
    sig_neg = jnp.where(pos, t, inv)
    has_lb = lb > 0.0
    logf = jnp.log(jnp.where(has_lb, lb + one_m_lb * sig, inv)) + jnp.where(has_lb, 0.0, jnp.minimum(f, 0.0))
    return q * _sigmoid(q), one_m_lb * sig_neg, logf


def _hgrn_kernel(q_ref, f_ref, i_ref, g_ref, lb_ref, gain_ref, o_ref, kk_s, v_s, b_s,
                 qt_s, kt_s, qd_s, kd_s, dec_s):
    t = HG_T
    dk = HG_DK
    w = q_ref.shape[2]
    n_chunks = q_ref.shape[1] // t
    lb = lb_ref[...]
    one_m_lb = 1.0 - lb
    gain = gain_ref[...]
    heads = [slice(h * dk, (h + 1) * dk) for h in range(HG_PAIR)]
    row = lax.broadcasted_iota(I32, (t, HG_PAIR * t), 0)
    col = lax.broadcasted_iota(I32, (t, HG_PAIR * t), 1)
    causal = row >= (col % t)
    tril = causal[:, :t].astype(BF16)

    def finish(o, g):
        return (_rms(o) * gain * (g * _sigmoid(g))).astype(BF16)

    def block_diag(parts):
        rows = []
        for h, p in enumerate(parts):
            z = jnp.zeros_like(p)
            rows.append(jnp.concatenate([p if j == h else z for j in range(HG_PAIR)], axis=1))
        return jnp.concatenate(rows, axis=0)

    def prepare(c, bmax):
        rows = pl.ds(pl.multiple_of(c * t, t), t)
        q = q_ref[0, rows, :].astype(F32)
        f = f_ref[0, rows, :].astype(F32)
        qs, kk, logf = _hgrn_gates(q, f, lb, one_m_lb)
        hi = logf.astype(BF16)
        lo = (logf - hi.astype(F32)).astype(BF16)
        bb = _dot(tril, jnp.concatenate([hi, lo], axis=-1))
        b = bb[:, :w] + bb[:, w:]
        b_last = b[t - 1:t, :]
        mid = 0.5 * b_last
        e_mid = jnp.exp(mid)
        qt = qs * jnp.exp(b - mid)
        kt = kk * jnp.exp(mid - b)
        qt_s[rows, :] = qt.astype(BF16)
        kt_s[rows, :] = kt.astype(BF16)
        qd_s[rows, :] = (qt * e_mid).astype(BF16)
        kd_s[rows, :] = (kt * e_mid).astype(BF16)
        dec_s[pl.ds(pl.multiple_of(c * 8, 8), 8), :] = jnp.broadcast_to(e_mid * e_mid, (8, w))
        return jnp.maximum(bmax, jnp.abs(b_last))

    bmax = lax.fori_loop(0, n_chunks, prepare, jnp.zeros_like(lb), unroll=8)

    def chunk(c, sts):
        rows = pl.ds(pl.multiple_of(c * t, t), t)
        v = i_ref[0, rows, :]
        g = g_ref[0, rows, :].astype(F32)
        ktb = kt_s[rows, :]
        kd = kd_s[rows, :]
        decay = dec_s[pl.ds(pl.multiple_of(c * 8, 8), 1), :]
        att = _dot_nt(qt_s[rows, :], block_diag([ktb[:, hs] for hs in heads]))
        att = jnp.where(causal, att, 0.0).astype(BF16)
        o = _dot(att, block_diag([v[:, hs] for hs in heads]))
        st_bd = block_diag([st.astype(BF16) for st in sts])
        o = o + _dot_nt(qd_s[rows, :], st_bd)
        sts = tuple(st * decay[:, hs] + _dot_tn(v[:, hs], kd[:, hs]) for st, hs in zip(sts, heads))
        out = [finish(o[:, hs], g[:, hs]) for hs in heads]
        o_ref[0, rows, :] = jnp.concatenate(out, axis=1)
        return sts

    st0 = jnp.zeros((dk, dk), F32)
    lax.fori_loop(0, n_chunks, chunk, (st0,) * HG_PAIR, unroll=16)
    safe = 0.5 * jnp.max(bmax) <= HG_MAX_HALF_RANGE

    @pl.when(jnp.logical_not(safe))
    def _():
        n = HG_SUB
        sub_row = lax.broadcasted_iota(I32, (n, 1), 0)
        tril_n = (lax.broadcasted_iota(I32, (n, n), 0) >= lax.broadcasted_iota(I32, (n, n), 1)).astype(F32)

        for hs in heads:
            def block(i, st, hs=hs):
                rows = pl.ds(pl.multiple_of(i * n, n), n)
                q = q_ref[0, rows, hs].astype(F32)
                f = f_ref[0, rows, hs].astype(F32)
                v = i_ref[0, rows, hs]
                g = g_ref[0, rows, hs].astype(F32)
                qs, kk, logf = _hgrn_gates(q, f, lb[:, hs], one_m_lb[:, hs])
                b = jnp.dot(tril_n, logf, precision=HIGHEST, preferred_element_type=F32)
                kk_s[...] = kk
                v_s[...] = v.astype(F32)
                b_s[...] = b
                o = _dot_nt((qs * jnp.exp(b)).astype(BF16), st.astype(BF16))

                def pair(s, acc):
                    dec = jnp.exp(jnp.minimum(b - b_s[pl.ds(s, 1), :], 0.0))
                    wgt = jnp.sum(qs * kk_s[pl.ds(s, 1), :] * dec, axis=-1, keepdims=True)
                    return acc + jnp.where(sub_row >= s, wgt, 0.0) * v_s[pl.ds(s, 1), :]

                o = lax.fori_loop(0, n, pair, o)
                b_last = b[n - 1:n, :]
                st = st * jnp.exp(b_last) + _dot_tn(v, (kk * jnp.exp(b_last - b)).astype(BF16))
                o_ref[0, rows, hs] = finish(o, g)
                return st

            lax.fori_loop(0, q_ref.shape[1] // n, block, st0)


def _hgrn(proj, lb, gain):
    bsz, s, w4 = proj.shape
    w = HG_PAIR * HG_DK
    npair = w4 // (4 * w)

    def spec(j):
        return pl.BlockSpec((1, s, w), lambda b, p: (b, 0, p + j * npair))

    return pl.pallas_call(
        _hgrn_kernel,
        grid=(bsz, npair),
        in_specs=[spec(0), spec(1), spec(2), spec(3),
                  pl.BlockSpec((1, w), lambda b, p: (0, p)),
                  pl.BlockSpec((1, HG_DK), lambda b, p: (0, 0))],
        out_specs=pl.BlockSpec((1, s, w), lambda b, p: (b, 0, p)),
        out_shape=jax.ShapeDtypeStruct((bsz, s, npair * w), BF16),
        scratch_shapes=[pltpu.VMEM((HG_SUB, HG_DK), F32)] * 3 + [pltpu.VMEM((s, w), BF16)] * 4
        + [pltpu.VMEM((8 * (s // HG_T), w), F32)],
        compiler_params=pltpu.CompilerParams(dimension_semantics=("parallel", "parallel")),
        name="hgrn",
    )(proj, proj, proj, proj, lb, gain)


def _rope_kernel(pos_ref, inv_ref, cos_ref, sin_ref):
    ang = pos_ref[0].astype(F32) * inv_ref[...]
    cos_ref[0] = jnp.cos(ang)
    sin_ref[0] = jnp.sin(ang)


def _rope_tables(positions):
    bsz, s = positions.shape
    half = RET_DK // 2
    inv_freq = (1.0 / (ROPE_BASE ** jnp.linspace(0.0, 1.0, half, dtype=F32))).reshape(1, half)
    out = jax.ShapeDtypeStruct((bsz, s, half), F32)
    return pl.pallas_call(
        _rope_kernel,
        grid=(bsz,),
        in_specs=[pl.BlockSpec((1, s, 1), lambda b: (b, 0, 0)),
                  pl.BlockSpec((1, half), lambda b: (0, 0))],
        out_specs=[pl.BlockSpec((1, s, half), lambda b: (b, 0, 0))] * 2,
        out_shape=[out, out],
        name="rope_tables",
    )(positions.reshape(bsz, s, 1), inv_freq)


def _ret_kernel(q_ref, k_ref, v_ref, g_ref, cos_ref, sin_ref, lg_ref, o_ref, r_ref, d_ref):
    t = RET_T
    dk = RET_DK
    half = dk // 2
    n_steps = q_ref.shape[1] // t
    lg = lg_ref[0]
    lg_k = lg[:, :dk]
    n = lax.broadcasted_iota(I32, (t, t), 0)
    m = lax.broadcasted_iota(I32, (t, t), 1)
    dist = jnp.abs(n - m).astype(F32)
    visible = (m // RET_CHUNK) <= (n // RET_CHUNK)
    k_scale = dk ** -0.5
    d_ref[...] = jnp.where(visible, k_scale * jnp.exp(dist * lg[:, :t]), 0.0)
    idx = lax.broadcasted_iota(I32, (t, dk), 0).astype(F32)
    q_decay = jnp.exp((idx + 1.0) * lg_k)
    k_decay = k_scale * jnp.exp((t - 1.0 - idx) * lg_k)
    step_decay = jnp.exp(float(t) * lg)
    r_ref[...] = jnp.zeros_like(r_ref)

    def rotate(x, cos, sin):
        x1, x2 = x[:, :half], x[:, half:]
        return jnp.concatenate([x1 * cos - x2 * sin, x2 * cos + x1 * sin], axis=-1)

    def step(c, carry):
        r0 = pl.multiple_of(c * t, t)
        rows = pl.ds(r0, t)
        cos = cos_ref[0, rows, :]
        sin = sin_ref[0, rows, :]
        q = rotate(q_ref[0, rows, :].astype(F32), cos, sin)
        k = rotate(k_ref[0, rows, :].astype(F32), cos, sin)
        v = v_ref[0, rows, :]
        g = g_ref[0, rows, :].astype(F32)
        r = r_ref[...]
        sc = _dot_nt(q.astype(BF16), k.astype(BF16)) * d_ref[...]
        o = _dot(sc.astype(BF16), v) + _dot((q * q_decay).astype(BF16), r.astype(BF16))
        r_ref[...] = r * step_decay + _dot_tn((k * k_decay).astype(BF16), v)
        out = _rms(o) * (g * _sigmoid(g))
        o_ref[0, rows, :] = out.astype(BF16)
        return carry

    lax.fori_loop(0, n_steps, step, 0, unroll=2)


def _retention(proj, cos, sin):
    bsz, s, w = proj.shape
    dk, dv = RET_DK, RET_DV
    nh = w // (2 * dk + 2 * dv)
    hidx = jnp.arange(nh, dtype=F32)
    log_gamma = jnp.log(1.0 - 2.0 ** (-5.0 - hidx))
    lg = jnp.broadcast_to(log_gamma[:, None, None], (nh, 1, dv))
    vbase = 2 * nh * dk // dv
    return pl.pallas_call(
        _ret_kernel,
        grid=(bsz, nh),
        in_specs=[
            pl.BlockSpec((1, s, dk), lambda b, h: (b, 0, h)),
            pl.BlockSpec((1, s, dk), lambda b, h: (b, 0, nh + h)),
            pl.BlockSpec((1, s, dv), lambda b, h: (b, 0, vbase + h)),
            pl.BlockSpec((1, s, dv), lambda b, h: (b, 0, vbase + nh + h)),
            pl.BlockSpec((1, s, dk // 2), lambda b, h: (b, 0, 0)),
            pl.BlockSpec((1, s, dk // 2), lambda b, h: (b, 0, 0)),
            pl.BlockSpec((1, 1, dv), lambda b, h: (h, 0, 0)),
        ],
        out_specs=pl.BlockSpec((1, s, dv), lambda b, h: (b, 0, h)),
        out_shape=jax.ShapeDtypeStruct((bsz, s, nh * dv), BF16),
        scratch_shapes=[pltpu.VMEM((dk, dv), F32), pltpu.VMEM((RET_T, RET_T), F32)],
        compiler_params=pltpu.CompilerParams(
            dimension_semantics=("parallel", "parallel"), vmem_limit_bytes=V7X_VMEM_LIMIT),
        name="retention",
    )(proj, proj, proj, proj, cos, sin, lg)


def _outproj_kernel(o_ref, w_ref, x_ref, g1_ref, sh_ref, sc_ref, gain_ref, rw_ref, rb_ref, u_ref,
                    xo_ref, ha_ref, hb_ref, eidx_ref, gate_ref, rank_ref, cnt_ref, base_ref):
    first = jnp.logical_and(pl.program_id(0) == 0, pl.program_id(1) == 0)

    @pl.when(first)
    def _():
        base_ref[...] = jnp.zeros_like(base_ref)

    ne = rw_ref.shape[0]
    rw = rw_ref[...]
    rw_hi = rw.astype(BF16)
    rw_lo = (rw - rw_hi.astype(F32)).astype(BF16)
    rw_both = jnp.concatenate([rw_hi, rw_lo], axis=0)
    tm = x_ref.shape[1]
    sub = min(OUTPROJ_SUB_ROWS, tm)
    h_groups = []
    for r in range(0, tm, sub):
        rows = slice(r, r + sub)
        y = _dot(o_ref[0, rows, :], w_ref[...])
        xn = x_ref[0, rows, :] + g1_ref[0] * y
        xo_ref[0, rows, :] = xn
        h = _rms(xn) * gain_ref[...] * (1.0 + sc_ref[0]) + sh_ref[0]
        packed = _pack_rows(h)
        slab = packed.shape[1] // ROW_PARTS
        ha_ref[0, rows, :] = packed[:, :slab]
        hb_ref[0, rows, :] = packed[:, slab:]
        h_groups.append(h.astype(BF16))

    part = _dot_nt(rw_both, jnp.concatenate(h_groups, axis=0))
    work = part[:ne] + part[ne:] + rb_ref[...]
    eiota = lax.broadcasted_iota(I32, (ne, tm), 0)
    onehots, tops = [], []
    for k in range(TOP_K):
        mx = jnp.max(work, axis=0, keepdims=True)
        idx = jnp.min(jnp.where(work == mx, eiota, ne), axis=0, keepdims=True)
        oh = eiota == idx
        work = jnp.where(oh, -jnp.inf, work)
        eidx_ref[k:k + 1, :] = idx
        onehots.append(oh)
        tops.append(mx)
    ex = [jnp.exp(m - tops[0]) for m in tops]
    denom = ex[0] + ex[1] + ex[2] + ex[3]
    for k in range(TOP_K):
        gate_ref[k:k + 1, :] = ex[k] / denom

    mask = jnp.zeros((ne, tm), F32)
    for oh in onehots:
        mask = mask + oh.astype(F32)
    incl = _dot(mask.astype(BF16), u_ref[...])
    excl = incl - mask + base_ref[...]
    for k in range(TOP_K):
        rk = jnp.sum(jnp.where(onehots[k], excl, 0.0), axis=0, keepdims=True)
        rank_ref[k:k + 1, :] = rk.astype(I32)
    total = base_ref[...] + incl[:, tm - 1:tm]
    base_ref[...] = total
    cnt_ref[...] = total.astype(I32)


def _outproj_route(o, w_out, x, mod, gain2, router_wt, router_b):
    bsz, s, d = x.shape
    hv = o.shape[2]
    tm = min(OUTPROJ_ROW_TILE, s)
    n = bsz * s
    nt = s // tm
    ne = router_wt.shape[0]
    upper = (jnp.arange(tm)[:, None] <= jnp.arange(tm)[None, :]).astype(BF16)

    def modspec(j):
        return pl.BlockSpec((1, 1, d), lambda b, i: (b, 0, j))

    tokspec = pl.BlockSpec((TOP_K, tm), lambda b, i: (0, b * nt + i))
    return pl.pallas_call(
        _outproj_kernel,
        grid=(bsz, nt),
        in_specs=[
            pl.BlockSpec((1, tm, hv), lambda b, i: (b, i, 0)),
            pl.BlockSpec((hv, d), lambda b, i: (0, 0)),
            pl.BlockSpec((1, tm, d), lambda b, i: (b, i, 0)),
            modspec(2), modspec(3), modspec(4),
            pl.BlockSpec((1, d), lambda b, i: (0, 0)),
            pl.BlockSpec((ne, d), lambda b, i: (0, 0)),
            pl.BlockSpec((ne, 1), lambda b, i: (0, 0)),
            pl.BlockSpec((tm, tm), lambda b, i: (0, 0)),
        ],
        out_specs=[
            pl.BlockSpec((1, tm, d), lambda b, i: (b, i, 0)),
            pl.BlockSpec((1, tm, d // 4), lambda b, i: (b, i, 0)),
            pl.BlockSpec((1, tm, d // 4), lambda b, i: (b, i, 0)),
            tokspec, tokspec, tokspec,
            pl.BlockSpec((ne, 1), lambda b, i: (0, 0)),
        ],
        out_shape=[
            jax.ShapeDtypeStruct((bsz, s, d), F32),
            jax.ShapeDtypeStruct((bsz, s, d // 4), I32),
            jax.ShapeDtypeStruct((bsz, s, d // 4), I32),
            jax.ShapeDtypeStruct((TOP_K, n), I32),
            jax.ShapeDtypeStruct((TOP_K, n), F32),
            jax.ShapeDtypeStruct((TOP_K, n), I32),
            jax.ShapeDtypeStruct((ne, 1), I32),
        ],
        scratch_shapes=[pltpu.VMEM((ne, 1), F32)],
        compiler_params=pltpu.CompilerParams(
            dimension_semantics=("arbitrary", "arbitrary"), vmem_limit_bytes=V7X_VMEM_LIMIT),
        name="outproj_route",
    )(o, w_out, x, mod, mod, mod, gain2, router_wt, router_b, upper)


def _moe_kernel(be_ref, nb_ref, xa_ref, xb_ref, w1_ref, b1_ref, w2_ref, b2_ref, ya_ref, yb_ref,
                w1_s, w2_s):
    i = pl.program_id(0)
    used = i < nb_ref[0]
    new_expert = jnp.logical_or(i == 0, be_ref[i] != be_ref[jnp.maximum(i - 1, 0)])

    @pl.when(jnp.logical_and(used, new_expert))
    def _():
        for w_ref, w_s in ((w1_ref, w1_s), (w2_ref, w2_s)):
            for r in range(0, w_s.shape[0], WEIGHT_CAST_ROWS):
                rows = slice(r, r + WEIGHT_CAST_ROWS)
                w_s[rows, :] = w_ref[0, 0, rows, :].astype(BF16)

    @pl.when(used)
    def _():
        f = w2_s.shape[0]
        for r in range(0, xa_ref.shape[0], MOE_SUB_ROWS):
            rows = slice(r, r + MOE_SUB_ROWS)
            x = _unpack_rows(jnp.concatenate([xa_ref[rows, :], xb_ref[rows, :]], axis=1)).astype(BF16)
            u = _dot(x, w1_s[...]) + b1_ref[0, 0]
            glu = jnp.minimum(u[:, :f], SWIGLU_LIMIT)
            lin = jnp.clip(u[:, f:], -SWIGLU_LIMIT, SWIGLU_LIMIT)
            a = glu * _sigmoid(SWIGLU_ALPHA * glu) * (lin + 1.0)
            y = _dot(a.astype(BF16), w2_s[...]) + b2_ref[0, 0]
            packed = _pack_rows(y)
            slab = packed.shape[1] // ROW_PARTS
            ya_ref[rows, :] = packed[:, :slab]
            yb_ref[rows, :] = packed[:, slab:]

    @pl.when(jnp.logical_not(used))
    def _():
        ya_ref[...] = jnp.zeros_like(ya_ref)
        yb_ref[...] = jnp.zeros_like(yb_ref)


def _moe_blocks(block_e, n_used, xs, layer, w1, b1, w2, b2):
    n_rows, dh = xs[0].shape
    nl, ne, d, f2 = w1.shape
    f = f2 // 2
    nblk = n_rows // MOE_ROWS
    grid_spec = pltpu.PrefetchScalarGridSpec(
        num_scalar_prefetch=2,
        grid=(nblk,),
        in_specs=[
            pl.BlockSpec((MOE_ROWS, dh), lambda i, be, nb: (i, 0)),
            pl.BlockSpec((MOE_ROWS, dh), lambda i, be, nb: (i, 0)),
            pl.BlockSpec((1, 1, d, f2), lambda i, be, nb: (layer, be[i], 0, 0)),
            pl.BlockSpec((1, 1, 1, f2), lambda i, be, nb: (layer, be[i], 0, 0)),
            pl.BlockSpec((1, 1, f, d), lambda i, be, nb: (layer, be[i], 0, 0)),
            pl.BlockSpec((1, 1, 1, d), lambda i, be, nb: (layer, be[i], 0, 0)),
        ],
        out_specs=[pl.BlockSpec((MOE_ROWS, dh), lambda i, be, nb: (i, 0))] * 2,
        scratch_shapes=[pltpu.VMEM((d, f2), BF16), pltpu.VMEM((f, d), BF16)],
    )
    return pl.pallas_call(
        _moe_kernel,
        grid_spec=grid_spec,
        out_shape=[jax.ShapeDtypeStruct((n_rows, dh), I32)] * 2,
        compiler_params=pltpu.CompilerParams(
            dimension_semantics=("arbitrary",), vmem_limit_bytes=V7X_VMEM_LIMIT),
        name="moe_experts",
    )(block_e, n_used, xs[0], xs[1], w1, b1.reshape(nl, ne, 1, f2), w2, b2.reshape(nl, ne, 1, d))


def _sc_mesh():
    return plsc.VectorSubcoreMesh(core_axis_name="c", subcore_axis_name="s")


def _sc_scatter_rows(srcs, dests, n_rows):
    n, w = srcs[0].shape
    ns, nk = len(srcs), len(dests)
    out = jax.ShapeDtypeStruct((n_rows, w), srcs[0].dtype)

    @functools.partial(pl.kernel, out_type=[out] * ns, mesh=_sc_mesh(),
                       scratch_types=[pltpu.SemaphoreType.DMA((nk,))])
    def scatter_kernel(*refs):
        x_hbm, idx_hbm, o_hbm, sems = refs[:ns], refs[ns:ns + nk], refs[ns + nk:-1], refs[-1]
        for x, o in zip(x_hbm, o_hbm):
            def body(x_vmem, *idx_vmem, o=o):
                copies = [pltpu.async_copy(x_vmem, o.at[iv.at[0]], sems.at[k])
                          for k, iv in enumerate(idx_vmem)]
                for cp in copies:
                    cp.wait()

            pltpu.emit_pipeline(
                body,
                grid=(n // SC_WINDOW,),
                in_specs=[pl.BlockSpec((SC_WINDOW, w), lambda i: (i, 0))]
                + [pl.BlockSpec((1, SC_WINDOW), lambda i: (0, i))] * nk,
                out_specs=[],
                core_axis_name=("c", "s"),
                dimension_semantics=(pltpu.PARALLEL,),
            )(x, *idx_hbm)

    return scatter_kernel(*srcs, *dests)


def _sc_gather_rows(tables, idx):
    m = idx.shape[1]
    w = tables[0].shape[1]
    nt = len(tables)
    out = jax.ShapeDtypeStruct((m, w), tables[0].dtype)

    @functools.partial(pl.kernel, out_type=[out] * nt, mesh=_sc_mesh(), scratch_types=[])
    def gather_kernel(*refs):
        t_hbm, i_hbm, o_hbm = refs[:nt], refs[nt], refs[nt + 1:]
        for t, o in zip(t_hbm, o_hbm):
            def body(i_vmem, o_vmem, t=t):
                pltpu.sync_copy(t.at[i_vmem.at[0]], o_vmem)

            pltpu.emit_pipeline(
                body,
                grid=(m // SC_WINDOW,),
                in_specs=[pl.BlockSpec((1, SC_WINDOW), lambda i: (0, i))],
                out_specs=[pl.BlockSpec((SC_WINDOW, w), lambda i: (i, 0))],
                core_axis_name=("c", "s"),
                dimension_semantics=(pltpu.PARALLEL,),
            )(i_hbm, o)

    return gather_kernel(*tables, idx)


def _dest_kernel(ps_ref, eidx_ref, rank_ref, o_ref):
    eidx = eidx_ref[...]
    start = jnp.zeros_like(eidx)
    for e in range(ps_ref.shape[0]):
        start = jnp.where(eidx == e, ps_ref[e], start)
    o_ref[...] = start + rank_ref[...]


def _dest_rows(pad_start, eidx, rank):
    k, n = eidx.shape
    tn = min(8192, n)
    blk = pl.BlockSpec((k, tn), lambda i, ps: (0, i))
    return pl.pallas_call(
        _dest_kernel,
        grid_spec=pltpu.PrefetchScalarGridSpec(
            num_scalar_prefetch=1, grid=(n // tn,), in_specs=[blk, blk], out_specs=blk),
        out_shape=jax.ShapeDtypeStruct((k, n), I32),
        name="dest_rows",
    )(pad_start, eidx, rank)


def _final_kernel(x_ref, *refs):
    g_ref, sh_ref, sc_ref, o_ref = refs[N_MOE_REFS:]
    x = _moe_residual(x_ref, refs[:N_MOE_REFS])
    o_ref[0] = _rms(x) * g_ref[...] * (1.0 + sc_ref[0]) + sh_ref[0]


def _final(x, pending, gain, fmod):
    bsz, s, d = x.shape
    tm = min(ROW_TILE, s)
    blk = pl.BlockSpec((1, tm, d), lambda b, i: (b, i, 0))
    moe_specs, moe_args = _moe_residual_operands(pending, bsz, s, d, tm, lambda b, i: (b, i))
    return pl.pallas_call(
        _final_kernel,
        grid=(bsz, s // tm),
        in_specs=[blk] + moe_specs + [
            pl.BlockSpec((1, d), lambda b, i: (0, 0)),
            pl.BlockSpec((1, 1, d), lambda b, i: (b, 0, 0)),
            pl.BlockSpec((1, 1, d), lambda b, i: (b, 0, 1))],
        out_specs=blk,
        out_shape=jax.ShapeDtypeStruct((bsz, s, d), F32),
        compiler_params=pltpu.CompilerParams(dimension_semantics=("parallel", "parallel")),
        name="final_norm",
    )(x, *moe_args, gain, fmod, fmod)


def _moe_layer(h2, eidx, gate, rank, counts, layer, w1, b1, w2, b2):
    bsz, s, dh = h2[0].shape
    n = bsz * s
    ne = w1.shape[1]
    nblk = -(-(n * TOP_K) // MOE_ROWS) + ne
    n_rows = nblk * MOE_ROWS
    counts = counts[:, 0]
    padded = (counts + MOE_ROWS - 1) // MOE_ROWS * MOE_ROWS
    pad_end = jnp.cumsum(padded)
    pad_start = pad_end - padded
    block_start = jnp.arange(nblk, dtype=I32)[:, None] * MOE_ROWS
    block_e = jnp.minimum(jnp.sum(pad_end[None, :] <= block_start, axis=1), ne - 1).astype(I32)
    n_used = (pad_end[-1:] // MOE_ROWS).astype(I32)
    dest = _dest_rows(pad_start.astype(I32), eidx, rank)
    xs = _sc_scatter_rows([h.reshape(n, dh) for h in h2], [dest[k:k + 1] for k in range(TOP_K)], n_rows)
    ys = _moe_blocks(block_e, n_used, xs, layer, w1, b1, w2, b2)
    yg = _sc_gather_rows(ys, dest.reshape(1, TOP_K * n))
    return yg, gate.T


def _hgrn_lower_bounds(lb_logits):
    p = jax.nn.softmax(lb_logits.astype(F32), axis=0)
    cum = jnp.cumsum(p, axis=0)
    return cum - cum[0:1]


def kernel(x, c, positions, ada_w, ada_b, norm1_g, norm2_g, hgrn_w_in, hgrn_w_out, hgrn_o_gain, hgrn_lb_logits, ret_w_in, ret_w_out, router_w, router_b, moe_w1, moe_b1, moe_w2, moe_b2, final_g, final_ada_w, final_ada_b):
    depth = ada_w.shape[0]
    bsz, s, d = x.shape
    tm = min(ROW_TILE, s)
    assert s % tm == 0 and s % min(OUTPROJ_ROW_TILE, s) == 0 and s % RET_T == 0 and s % HG_T == 0
    assert (bsz * s) % SC_WINDOW == 0
    assert hgrn_w_in.shape[2] % (4 * HG_PAIR * HG_DK) == 0 and d % (2 * ROW_PARTS * 128) == 0
    assert router_w.shape[2] >= TOP_K == 4
    mods = _ada(c, ada_w, ada_b)
    fmod = _ada(c, final_ada_w[None], final_ada_b[None])[0][:, None, :]
    lbs = _hgrn_lower_bounds(hgrn_lb_logits)
    cos, sin = _rope_tables(positions)
    pending = None
    for layer in range(depth):
        mod = mods[layer][:, None, :]
        j = layer // N_MIXERS
        if layer % N_MIXERS == 0:
            x, proj = _inproj(x, pending, norm1_g[layer][None], mod, hgrn_w_in[j].astype(BF16))
            o = _hgrn(proj, lbs[j][None], hgrn_o_gain[j][None])
            w_out = hgrn_w_out[j]
        else:
            x, proj = _inproj(x, pending, norm1_g[layer][None], mod, ret_w_in[j].astype(BF16))
            o = _retention(proj, cos, sin)
            w_out = ret_w_out[j]
        x, ha, hb, eidx, gate, rank, counts = _outproj_route(
            o, w_out.astype(BF16), x, mod, norm2_g[layer][None],
            router_w[layer].T, router_b[layer][:, None])
        yg, gate_rows = _moe_layer((ha, hb), eidx, gate, rank, counts, layer, moe_w1, moe_b1, moe_w2, moe_b2)
        pending = (yg, gate_rows, mod)
    return _final(x, pending, final_g[None], fmod)
```

```python
import functools

import jax
import jax.numpy as jnp
from jax import lax
from jax.experimental import pallas as pl
from jax.experimental.pallas import tpu as pltpu
from jax.experimental.pallas import tpu_sc as plsc

F32 = jnp.float32
BF16 = jnp.bfloat16
I32 = jnp.int32
HIGHEST = lax.Precision.HIGHEST

EPS = 1e-6
N_MIXERS = 2
HG_DK = 128
HG_T = 64
HG_PAIR = 2
HG_SUB = 16
HG_MAX_HALF_RANGE = 80.0
RET_DK = 256
RET_DV = 512
RET_CHUNK = 64
RET_T = 256
ROPE_BASE = 10000.0
TOP_K = 4
SWIGLU_ALPHA = 1.702
SWIGLU_LIMIT = 7.0
MOE_ROWS = 1024
MOE_SUB_ROWS = 512
WEIGHT_CAST_ROWS = 128
ROW_TILE = 512
INPROJ_SUB_ROWS = 256
OUTPROJ_ROW_TILE = 1024
OUTPROJ_SUB_ROWS = 256
SC_WINDOW = 128
ROW_PARTS = 2
V7X_VMEM_BYTES = 64 * 1024 * 1024
V7X_VMEM_LIMIT = V7X_VMEM_BYTES * 7 // 8


def _dot(a, b):
    return jnp.dot(a, b, preferred_element_type=F32)


def _dot_nt(a, b):
    return lax.dot_general(a, b, (((1,), (1,)), ((), ())), preferred_element_type=F32)


def _dot_tn(a, b):
    return lax.dot_general(a, b, (((0,), (0,)), ((), ())), preferred_element_type=F32)


def _rms(x):
    return x * lax.rsqrt(jnp.mean(x * x, axis=-1, keepdims=True) + EPS)


def _sigmoid(x):
    return 1.0 / (1.0 + jnp.exp(-x))


def _pack_rows(h):
    half = h.shape[1] // 2
    a = lax.bitcast_convert_type(h[:, :half].astype(BF16).astype(F32), jnp.uint32)
    b = lax.bitcast_convert_type(h[:, half:].astype(BF16).astype(F32), jnp.uint32)
    return lax.bitcast_convert_type(a | (b >> 16), I32)


def _unpack_rows(w):
    u = lax.bitcast_convert_type(w, jnp.uint32)
    a = lax.bitcast_convert_type(u & jnp.uint32(0xFFFF0000), F32)
    b = lax.bitcast_convert_type(u << 16, F32)
    return jnp.concatenate([a, b], axis=1)


def _ada_kernel(c_ref, w_ref, b_ref, o_ref):
    c = c_ref[...]
    cond = c * _sigmoid(c)
    o_ref[0] = jnp.dot(cond, w_ref[0], precision=HIGHEST, preferred_element_type=F32) + b_ref[0]


def _ada(c, w, b):
    nl, d, kd = w.shape
    bsz = c.shape[0]
    return pl.pallas_call(
        _ada_kernel,
        grid=(nl, kd // d),
        in_specs=[
            pl.BlockSpec((bsz, d), lambda l, j: (0, 0)),
            pl.BlockSpec((1, d, d), lambda l, j: (l, 0, j)),
            pl.BlockSpec((1, 1, d), lambda l, j: (l, 0, j)),
        ],
        out_specs=pl.BlockSpec((1, bsz, d), lambda l, j: (l, 0, j)),
        out_shape=jax.ShapeDtypeStruct((nl, bsz, kd), F32),
        name="ada_mod",
    )(c, w, b.reshape(nl, 1, kd))


def _moe_residual(x_ref, refs, rows=slice(None)):
    ya_refs, yb_refs = refs[:TOP_K], refs[TOP_K:2 * TOP_K]
    gate_ref, g2_ref = refs[2 * TOP_K:]
    gate = gate_ref[rows, :]
    acc = None
    for k in range(TOP_K):
        packed = jnp.concatenate([ya_refs[k][rows, :], yb_refs[k][rows, :]], axis=1)
        term = gate[:, k:k + 1] * _unpack_rows(packed)
        acc = term if acc is None else acc + term
    return x_ref[0, rows, :] + g2_ref[0] * acc


N_MOE_REFS = 2 * TOP_K + 2


def _moe_residual_operands(pending, bsz, s, d, tm, tile):
    yg, gate_rows, mod = pending
    nt = s // tm
    ntok = (bsz * s) // tm

    def flat(*g):
        b, i = tile(*g)
        return b * nt + i

    def yspec(k):
        return pl.BlockSpec((tm, d // 4), lambda *g: (k * ntok + flat(*g), 0))

    specs = [yspec(k) for k in range(TOP_K)] * 2 + [
        pl.BlockSpec((tm, TOP_K), lambda *g: (flat(*g), 0)),
        pl.BlockSpec((1, 1, d), lambda *g: (tile(*g)[0], 0, 5))]
    return specs, [yg[0]] * TOP_K + [yg[1]] * TOP_K + [gate_rows, mod]


def _inproj_kernel(x_ref, *refs, col_chunk, fused):
    if fused:
        g_ref, sh_ref, sc_ref, w_ref, o_ref, xo_ref = refs[N_MOE_REFS:]
    else:
        g_ref, sh_ref, sc_ref, w_ref, o_ref = refs
    nout = w_ref.shape[1]
    tm = x_ref.shape[1]
    sub = min(INPROJ_SUB_ROWS, tm)
    for r in range(0, tm, sub):
        rows = slice(r, r + sub)
        if fused:
            x = _moe_residual(x_ref, refs[:N_MOE_REFS], rows)
            xo_ref[0, rows, :] = x
        else:
            x = x_ref[0, rows, :]
        h = _rms(x) * g_ref[...] * (1.0 + sc_ref[0]) + sh_ref[0]
        hb = h.astype(BF16)
        for j in range(nout // col_chunk):
            cs = slice(j * col_chunk, (j + 1) * col_chunk)
            o_ref[0, rows, cs] = _dot(hb, w_ref[:, cs]).astype(BF16)


def _inproj(x, pending, gain, mod, w):
    bsz, s, d = x.shape
    nout = w.shape[1]
    tm = min(ROW_TILE, s)
    xspec = pl.BlockSpec((1, tm, d), lambda b, i: (b, i, 0))
    fused = pending is not None
    moe_specs, moe_args = (_moe_residual_operands(pending, bsz, s, d, tm, lambda b, i: (b, i))
                           if fused else ([], []))
    proj_spec = pl.BlockSpec((1, tm, nout), lambda b, i: (b, i, 0))
    proj_shape = jax.ShapeDtypeStruct((bsz, s, nout), BF16)
    out = pl.pallas_call(
        functools.partial(_inproj_kernel, col_chunk=1024, fused=fused),
        grid=(bsz, s // tm),
        in_specs=[xspec] + moe_specs + [
            pl.BlockSpec((1, d), lambda b, i: (0, 0)),
            pl.BlockSpec((1, 1, d), lambda b, i: (b, 0, 0)),
            pl.BlockSpec((1, 1, d), lambda b, i: (b, 0, 1)),
            pl.BlockSpec((d, nout), lambda b, i: (0, 0), pipeline_mode=pl.Buffered(1)),
        ],
        out_specs=[proj_spec, xspec] if fused else proj_spec,
        out_shape=[proj_shape, jax.ShapeDtypeStruct((bsz, s, d), F32)] if fused else proj_shape,
        compiler_params=pltpu.CompilerParams(
            dimension_semantics=("parallel", "parallel"), vmem_limit_bytes=V7X_VMEM_LIMIT),
        name="inproj",
    )(x, *moe_args, gain, mod, mod, w)
    return (out[1], out[0]) if fused else (x, out)


def _hgrn_gates(q, f, lb, one_m_lb):
    e = jnp.exp(-jnp.abs(f))
    inv = 1.0 / (1.0 + e)
    pos = f >= 0.0
    t = e * inv
    sig = jnp.where(pos, inv, t)
    sig_neg = jnp.where(pos, t, inv)
    has_lb = lb > 0.0
    logf = jnp.log(jnp.where(has_lb, lb + one_m_lb * sig, inv)) + jnp.where(has_lb, 0.0, jnp.minimum(f, 0.0))
    return q * _sigmoid(q), one_m_lb * sig_neg, logf


def _hgrn_kernel(q_ref, f_ref, i_ref, g_ref, lb_ref, gain_ref, o_ref, kk_s, v_s, b_s,
                 qt_s, qd_s, kd_s, ktt_s, dec_s):
    t = HG_T
    dk = HG_DK
    w = q_ref.shape[2]
    n_chunks = q_ref.shape[1] // t
    lb = lb_ref[...]
    one_m_lb = 1.0 - lb
    gain = gain_ref[...]
    heads = [slice(h * dk, (h + 1) * dk) for h in range(HG_PAIR)]
    row = lax.broadcasted_iota(I32, (t, HG_PAIR * t), 0)
    col = lax.broadcasted_iota(I32, (t, HG_PAIR * t), 1)
    causal = row >= (col % t)
    tril = causal[:, :t].astype(BF16)

    def finish(o, g):
        return (_rms(o) * gain * (g * _sigmoid(g))).astype(BF16)

    def block_diag(parts):
        rows = []
        for h, p in enumerate(parts):
            z = jnp.zeros_like(p)
            rows.append(jnp.concatenate([p if j == h else z for j in range(HG_PAIR)], axis=1))
        return jnp.concatenate(rows, axis=0)

    def prepare(c, bmax):
        rows = pl.ds(pl.multiple_of(c * t, t), t)
        q = q_ref[0, rows, :].astype(F32)
        f = f_ref[0, rows, :].astype(F32)
        qs, kk, logf = _hgrn_gates(q, f, lb, one_m_lb)
        hi = logf.astype(BF16)
        lo = (logf - hi.astype(F32)).astype(BF16)
        bb = _dot(tril, jnp.concatenate([hi, lo], axis=-1))
        b = bb[:, :w] + bb[:, w:]
        b_last = b[t - 1:t, :]
        mid = 0.5 * b_last
        e_mid = jnp.exp(mid)
        qt = qs * jnp.exp(b - mid)
        kt = kk * jnp.exp(mid - b)
        qt_s[rows, :] = qt.astype(BF16)
        qd_s[rows, :] = (qt * e_mid).astype(BF16)
        kd_s[rows, :] = (kt * e_mid).astype(BF16)
        wrows = pl.ds(pl.multiple_of(c * w, w), w)
        ktt_s[wrows, :] = block_diag([kt[:, hs] for hs in heads]).T.astype(BF16)
        dec_s[pl.ds(pl.multiple_of(c * 8, 8), 8), :] = jnp.broadcast_to(e_mid * e_mid, (8, w))
        return jnp.maximum(bmax, jnp.abs(b_last))

    bmax = lax.fori_loop(0, n_chunks, prepare, jnp.zeros_like(lb), unroll=8)

    def chunk(c, sts):
        rows = pl.ds(pl.multiple_of(c * t, t), t)
        wrows = pl.ds(pl.multiple_of(c * w, w), w)
        v = i_ref[0, rows, :]
        g = g_ref[0, rows, :].astype(F32)
        kd = kd_s[rows, :]
        dec_row = dec_s[pl.ds(pl.multiple_of(c * 8, 8), 1), :]
        decay = jnp.broadcast_to(dec_row, (dk, w)).T
        att = _dot(qt_s[rows, :], ktt_s[wrows, :])
        att = jnp.where(causal, att, 0.0).astype(BF16)
        o = _dot(att, block_diag([v[:, hs] for hs in heads]))
        st_bd = block_diag([st.astype(BF16) for st in sts])
        o = o + _dot(qd_s[rows, :], st_bd)
        sts = tuple(st * decay[hs, :] + _dot_tn(kd[:, hs], v[:, hs]) for st, hs in zip(sts, heads))
        out = [finish(o[:, hs], g[:, hs]) for hs in heads]
        o_ref[0, rows, :] = jnp.concatenate(out, axis=1)
        return sts

    st0 = jnp.zeros((dk, dk), F32)
    lax.fori_loop(0, n_chunks, chunk, (st0,) * HG_PAIR, unroll=16)
    safe = 0.5 * jnp.max(bmax) <= HG_MAX_HALF_RANGE

    @pl.when(jnp.logical_not(safe))
    def _():
        n = HG_SUB
        sub_row = lax.broadcasted_iota(I32, (n, 1), 0)
        tril_n = (lax.broadcasted_iota(I32, (n, n), 0) >= lax.broadcasted_iota(I32, (n, n), 1)).astype(F32)

        for hs in heads:
            def block(i, st, hs=hs):
                rows = pl.ds(pl.multiple_of(i * n, n), n)
                q = q_ref[0, rows, hs].astype(F32)
                f = f_ref[0, rows, hs].astype(F32)
                v = i_ref[0, rows, hs]
                g = g_ref[0, rows, hs].astype(F32)
                qs, kk, logf = _hgrn_gates(q, f, lb[:, hs], one_m_lb[:, hs])
                b = jnp.dot(tril_n, logf, precision=HIGHEST, preferred_element_type=F32)
                kk_s[...] = kk
                v_s[...] = v.astype(F32)
                b_s[...] = b
                o = _dot_nt((qs * jnp.exp(b)).astype(BF16), st.astype(BF16))

                def pair(s, acc):
                    dec = jnp.exp(jnp.minimum(b - b_s[pl.ds(s, 1), :], 0.0))
                    wgt = jnp.sum(qs * kk_s[pl.ds(s, 1), :] * dec, axis=-1, keepdims=True)
                    return acc + jnp.where(sub_row >= s, wgt, 0.0) * v_s[pl.ds(s, 1), :]

                o = lax.fori_loop(0, n, pair, o)
                b_last = b[n - 1:n, :]
                st = st * jnp.exp(b_last) + _dot_tn(v, (kk * jnp.exp(b_last - b)).astype(BF16))
                o_ref[0, rows, hs] = finish(o, g)
                return st

            lax.fori_loop(0, q_ref.shape[1] // n, block, st0)


def _hgrn(proj, lb, gain):
    bsz, s, w4 = proj.shape
    w = HG_PAIR * HG_DK
    npair = w4 // (4 * w)

    def spec(j):
        return pl.BlockSpec((1, s, w), lambda b, p: (b, 0, p + j * npair))

    return pl.pallas_call(
        _hgrn_kernel,
        grid=(bsz, npair),
        in_specs=[spec(0), spec(1), spec(2), spec(3),
                  pl.BlockSpec((1, w), lambda b, p: (0, p)),
                  pl.BlockSpec((1, HG_DK), lambda b, p: (0, 0))],
        out_specs=pl.BlockSpec((1, s, w), lambda b, p: (b, 0, p)),
        out_shape=jax.ShapeDtypeStruct((bsz, s, npair * w), BF16),
        scratch_shapes=[pltpu.VMEM((HG_SUB, HG_DK), F32)] * 3 + [pltpu.VMEM((s, w), BF16)] * 3
        + [pltpu.VMEM((s // HG_T * w, HG_PAIR * HG_T), BF16), pltpu.VMEM((s // HG_T * 8, w), F32)],
        compiler_params=pltpu.CompilerParams(dimension_semantics=("parallel", "parallel")),
        name="hgrn",
    )(proj, proj, proj, proj, lb, gain)


def _rope_kernel(pos_ref, inv_ref, cos_ref, sin_ref):
    ang = pos_ref[0].astype(F32) * inv_ref[...]
    cos_ref[0] = jnp.cos(ang)
    sin_ref[0] = jnp.sin(ang)


def _rope_tables(positions):
    bsz, s = positions.shape
    half = RET_DK // 2
    inv_freq = (1.0 / (ROPE_BASE ** jnp.linspace(0.0, 1.0, half, dtype=F32))).reshape(1, half)
    out = jax.ShapeDtypeStruct((bsz, s, half), F32)
    return pl.pallas_call(
        _rope_kernel,
        grid=(bsz,),
        in_specs=[pl.BlockSpec((1, s, 1), lambda b: (b, 0, 0)),
                  pl.BlockSpec((1, half), lambda b: (0, 0))],
        out_specs=[pl.BlockSpec((1, s, half), lambda b: (b, 0, 0))] * 2,
        out_shape=[out, out],
        name="rope_tables",
    )(positions.reshape(bsz, s, 1), inv_freq)


def _ret_kernel(q_ref, k_ref, v_ref, g_ref, cos_ref, sin_ref, lg_ref, o_ref, r_ref, d_ref):
    t = RET_T
    dk = RET_DK
    half = dk // 2
    n_steps = q_ref.shape[1] // t
    lg = lg_ref[0]
    lg_k = lg[:, :dk]
    n = lax.broadcasted_iota(I32, (t, t), 0)
    m = lax.broadcasted_iota(I32, (t, t), 1)
    dist = jnp.abs(n - m).astype(F32)
    visible = (m // RET_CHUNK) <= (n // RET_CHUNK)
    k_scale = dk ** -0.5
    d_ref[...] = jnp.where(visible, k_scale * jnp.exp(dist * lg[:, :t]), 0.0)
    idx = lax.broadcasted_iota(I32, (t, dk), 0).astype(F32)
    q_decay = jnp.exp((idx + 1.0) * lg_k)
    k_decay = k_scale * jnp.exp((t - 1.0 - idx) * lg_k)
    step_decay = jnp.exp(float(t) * lg)
    r_ref[...] = jnp.zeros_like(r_ref)

    def rotate(x, cos, sin):
        x1, x2 = x[:, :half], x[:, half:]
        return jnp.concatenate([x1 * cos - x2 * sin, x2 * cos + x1 * sin], axis=-1)

    def step(c, carry):
        r0 = pl.multiple_of(c * t, t)
        rows = pl.ds(r0, t)
        cos = cos_ref[0, rows, :]
        sin = sin_ref[0, rows, :]
        q = rotate(q_ref[0, rows, :].astype(F32), cos, sin)
        k = rotate(k_ref[0, rows, :].astype(F32), cos, sin)
        v = v_ref[0, rows, :]
        g = g_ref[0, rows, :].astype(F32)
        r = r_ref[...]
        sc = _dot_nt(q.astype(BF16), k.astype(BF16)) * d_ref[...]
        o = _dot(sc.astype(BF16), v) + _dot((q * q_decay).astype(BF16), r.astype(BF16))
        r_ref[...] = r * step_decay + _dot_tn((k * k_decay).astype(BF16), v)
        out = _rms(o) * (g * _sigmoid(g))
        o_ref[0, rows, :] = out.astype(BF16)
        return carry

    lax.fori_loop(0, n_steps, step, 0, unroll=2)


def _retention(proj, cos, sin):
    bsz, s, w = proj.shape
    dk, dv = RET_DK, RET_DV
    nh = w // (2 * dk + 2 * dv)
    hidx = jnp.arange(nh, dtype=F32)
    log_gamma = jnp.log(1.0 - 2.0 ** (-5.0 - hidx))
    lg = jnp.broadcast_to(log_gamma[:, None, None], (nh, 1, dv))
    vbase = 2 * nh * dk // dv
    return pl.pallas_call(
        _ret_kernel,
        grid=(bsz, nh),
        in_specs=[
            pl.BlockSpec((1, s, dk), lambda b, h: (b, 0, h)),
            pl.BlockSpec((1, s, dk), lambda b, h: (b, 0, nh + h)),
            pl.BlockSpec((1, s, dv), lambda b, h: (b, 0, vbase + h)),
            pl.BlockSpec((1, s, dv), lambda b, h: (b, 0, vbase + nh + h)),
            pl.BlockSpec((1, s, dk // 2), lambda b, h: (b, 0, 0)),
            pl.BlockSpec((1, s, dk // 2), lambda b, h: (b, 0, 0)),
            pl.BlockSpec((1, 1, dv), lambda b, h: (h, 0, 0)),
        ],
        out_specs=pl.BlockSpec((1, s, dv), lambda b, h: (b, 0, h)),
        out_shape=jax.ShapeDtypeStruct((bsz, s, nh * dv), BF16),
        scratch_shapes=[pltpu.VMEM((dk, dv), F32), pltpu.VMEM((RET_T, RET_T), F32)],
        compiler_params=pltpu.CompilerParams(
            dimension_semantics=("parallel", "parallel"), vmem_limit_bytes=V7X_VMEM_LIMIT),
        name="retention",
    )(proj, proj, proj, proj, cos, sin, lg)


def _outproj_kernel(o_ref, w_ref, x_ref, g1_ref, sh_ref, sc_ref, gain_ref, rw_ref, rb_ref, u_ref,
                    xo_ref, ha_ref, hb_ref, eidx_ref, gate_ref, rank_ref, cnt_ref, base_ref):
    first = jnp.logical_and(pl.program_id(0) == 0, pl.program_id(1) == 0)

    @pl.when(first)
    def _():
        base_ref[...] = jnp.zeros_like(base_ref)

    ne = rw_ref.shape[0]
    rw = rw_ref[...]
    rw_hi = rw.astype(BF16)
    rw_lo = (rw - rw_hi.astype(F32)).astype(BF16)
    rw_both = jnp.concatenate([rw_hi, rw_lo], axis=0)
    tm = x_ref.shape[1]
    sub = min(OUTPROJ_SUB_ROWS, tm)
    h_groups = []
    for r in range(0, tm, sub):
        rows = slice(r, r + sub)
        y = _dot(o_ref[0, rows, :], w_ref[...])
        xn = x_ref[0, rows, :] + g1_ref[0] * y
        xo_ref[0, rows, :] = xn
        h = _rms(xn) * gain_ref[...] * (1.0 + sc_ref[0]) + sh_ref[0]
        packed = _pack_rows(h)
        slab = packed.shape[1] // ROW_PARTS
        ha_ref[0, rows, :] = packed[:, :slab]
        hb_ref[0, rows, :] = packed[:, slab:]
        h_groups.append(h.astype(BF16))

    part = _dot_nt(rw_both, jnp.concatenate(h_groups, axis=0))
    work = part[:ne] + part[ne:] + rb_ref[...]
    eiota = lax.broadcasted_iota(I32, (ne, tm), 0)
    onehots, tops = [], []
    for k in range(TOP_K):
        mx = jnp.max(work, axis=0, keepdims=True)
        idx = jnp.min(jnp.where(work == mx, eiota, ne), axis=0, keepdims=True)
        oh = eiota == idx
        work = jnp.where(oh, -jnp.inf, work)
        eidx_ref[k:k + 1, :] = idx
        onehots.append(oh)
        tops.append(mx)
    ex = [jnp.exp(m - tops[0]) for m in tops]
    denom = ex[0] + ex[1] + ex[2] + ex[3]
    for k in range(TOP_K):
        gate_ref[k:k + 1, :] = ex[k] / denom

    mask = jnp.zeros((ne, tm), F32)
    for oh in onehots:
        mask = mask + oh.astype(F32)
    incl = _dot(mask.astype(BF16), u_ref[...])
    excl = incl - mask + base_ref[...]
    for k in range(TOP_K):
        rk = jnp.sum(jnp.where(onehots[k], excl, 0.0), axis=0, keepdims=True)
        rank_ref[k:k + 1, :] = rk.astype(I32)
    total = base_ref[...] + incl[:, tm - 1:tm]
    base_ref[...] = total
    cnt_ref[...] = total.astype(I32)


def _outproj_route(o, w_out, x, mod, gain2, router_wt, router_b):
    bsz, s, d = x.shape
    hv = o.shape[2]
    tm = min(OUTPROJ_ROW_TILE, s)
    n = bsz * s
    nt = s // tm
    ne = router_wt.shape[0]
    upper = (jnp.arange(tm)[:, None] <= jnp.arange(tm)[None, :]).astype(BF16)

    def modspec(j):
        return pl.BlockSpec((1, 1, d), lambda b, i: (b, 0, j))

    tokspec = pl.BlockSpec((TOP_K, tm), lambda b, i: (0, b * nt + i))
    return pl.pallas_call(
        _outproj_kernel,
        grid=(bsz, nt),
        in_specs=[
            pl.BlockSpec((1, tm, hv), lambda b, i: (b, i, 0)),
            pl.BlockSpec((hv, d), lambda b, i: (0, 0)),
            pl.BlockSpec((1, tm, d), lambda b, i: (b, i, 0)),
            modspec(2), modspec(3), modspec(4),
            pl.BlockSpec((1, d), lambda b, i: (0, 0)),
            pl.BlockSpec((ne, d), lambda b, i: (0, 0)),
            pl.BlockSpec((ne, 1), lambda b, i: (0, 0)),
            pl.BlockSpec((tm, tm), lambda b, i: (0, 0)),
        ],
        out_specs=[
            pl.BlockSpec((1, tm, d), lambda b, i: (b, i, 0)),
            pl.BlockSpec((1, tm, d // 4), lambda b, i: (b, i, 0)),
            pl.BlockSpec((1, tm, d // 4), lambda b, i: (b, i, 0)),
            tokspec, tokspec, tokspec,
            pl.BlockSpec((ne, 1), lambda b, i: (0, 0)),
        ],
        out_shape=[
            jax.ShapeDtypeStruct((bsz, s, d), F32),
            jax.ShapeDtypeStruct((bsz, s, d // 4), I32),
            jax.ShapeDtypeStruct((bsz, s, d // 4), I32),
            jax.ShapeDtypeStruct((TOP_K, n), I32),
            jax.ShapeDtypeStruct((TOP_K, n), F32),
            jax.ShapeDtypeStruct((TOP_K, n), I32),
            jax.ShapeDtypeStruct((ne, 1), I32),
        ],
        scratch_shapes=[pltpu.VMEM((ne, 1), F32)],
        compiler_params=pltpu.CompilerParams(
            dimension_semantics=("arbitrary", "arbitrary"), vmem_limit_bytes=V7X_VMEM_LIMIT),
        name="outproj_route",
    )(o, w_out, x, mod, mod, mod, gain2, router_wt, router_b, upper)


def _moe_kernel(be_ref, nb_ref, xa_ref, xb_ref, w1_ref, b1_ref, w2_ref, b2_ref, ya_ref, yb_ref,
                w1_s, w2_s):
    i = pl.program_id(0)
    used = i < nb_ref[0]
    new_expert = jnp.logical_or(i == 0, be_ref[i] != be_ref[jnp.maximum(i - 1, 0)])

    @pl.when(jnp.logical_and(used, new_expert))
    def _():
        for w_ref, w_s in ((w1_ref, w1_s), (w2_ref, w2_s)):
            for r in range(0, w_s.shape[0], WEIGHT_CAST_ROWS):
                rows = slice(r, r + WEIGHT_CAST_ROWS)
                w_s[rows, :] = w_ref[0, 0, rows, :].astype(BF16)

    @pl.when(used)
    def _():
        f = w2_s.shape[0]
        for r in range(0, xa_ref.shape[0], MOE_SUB_ROWS):
            rows = slice(r, r + MOE_SUB_ROWS)
            x = _unpack_rows(jnp.concatenate([xa_ref[rows, :], xb_ref[rows, :]], axis=1)).astype(BF16)
            u = _dot(x, w1_s[...]) + b1_ref[0, 0]
            glu = jnp.minimum(u[:, :f], SWIGLU_LIMIT)
            lin = jnp.clip(u[:, f:], -SWIGLU_LIMIT, SWIGLU_LIMIT)
            a = glu * _sigmoid(SWIGLU_ALPHA * glu) * (lin + 1.0)
            y = _dot(a.astype(BF16), w2_s[...]) + b2_ref[0, 0]
            packed = _pack_rows(y)
            slab = packed.shape[1] // ROW_PARTS
            ya_ref[rows, :] = packed[:, :slab]
            yb_ref[rows, :] = packed[:, slab:]

    @pl.when(jnp.logical_not(used))
    def _():
        ya_ref[...] = jnp.zeros_like(ya_ref)
        yb_ref[...] = jnp.zeros_like(yb_ref)


def _moe_blocks(block_e, n_used, xs, layer, w1, b1, w2, b2):
    n_rows, dh = xs[0].shape
    nl, ne, d, f2 = w1.shape
    f = f2 // 2
    nblk = n_rows // MOE_ROWS
    grid_spec = pltpu.PrefetchScalarGridSpec(
        num_scalar_prefetch=2,
        grid=(nblk,),
        in_specs=[
            pl.BlockSpec((MOE_ROWS, dh), lambda i, be, nb: (i, 0)),
            pl.BlockSpec((MOE_ROWS, dh), lambda i, be, nb: (i, 0)),
            pl.BlockSpec((1, 1, d, f2), lambda i, be, nb: (layer, be[i], 0, 0)),
            pl.BlockSpec((1, 1, 1, f2), lambda i, be, nb: (layer, be[i], 0, 0)),
            pl.BlockSpec((1, 1, f, d), lambda i, be, nb: (layer, be[i], 0, 0)),
            pl.BlockSpec((1, 1, 1, d), lambda i, be, nb: (layer, be[i], 0, 0)),
        ],
        out_specs=[pl.BlockSpec((MOE_ROWS, dh), lambda i, be, nb: (i, 0))] * 2,
        scratch_shapes=[pltpu.VMEM((d, f2), BF16), pltpu.VMEM((f, d), BF16)],
    )
    return pl.pallas_call(
        _moe_kernel,
        grid_spec=grid_spec,
        out_shape=[jax.ShapeDtypeStruct((n_rows, dh), I32)] * 2,
        compiler_params=pltpu.CompilerParams(
            dimension_semantics=("arbitrary",), vmem_limit_bytes=V7X_VMEM_LIMIT),
        name="moe_experts",
    )(block_e, n_used, xs[0], xs[1], w1, b1.reshape(nl, ne, 1, f2), w2, b2.reshape(nl, ne, 1, d))


def _sc_mesh():
    return plsc.VectorSubcoreMesh(core_axis_name="c", subcore_axis_name="s")


def _sc_scatter_rows(srcs, dests, n_rows):
    n, w = srcs[0].shape
    ns, nk = len(srcs), len(dests)
    out = jax.ShapeDtypeStruct((n_rows, w), srcs[0].dtype)

    @functools.partial(pl.kernel, out_type=[out] * ns, mesh=_sc_mesh(),
                       scratch_types=[pltpu.SemaphoreType.DMA((nk,))])
    def scatter_kernel(*refs):
        x_hbm, idx_hbm, o_hbm, sems = refs[:ns], refs[ns:ns + nk], refs[ns + nk:-1], refs[-1]
        for x, o in zip(x_hbm, o_hbm):
            def body(x_vmem, *idx_vmem, o=o):
                copies = [pltpu.async_copy(x_vmem, o.at[iv.at[0]], sems.at[k])
                          for k, iv in enumerate(idx_vmem)]
                for cp in copies:
                    cp.wait()

            pltpu.emit_pipeline(
                body,
                grid=(n // SC_WINDOW,),
                in_specs=[pl.BlockSpec((SC_WINDOW, w), lambda i: (i, 0))]
                + [pl.BlockSpec((1, SC_WINDOW), lambda i: (0, i))] * nk,
                out_specs=[],
                core_axis_name=("c", "s"),
                dimension_semantics=(pltpu.PARALLEL,),
            )(x, *idx_hbm)

    return scatter_kernel(*srcs, *dests)


def _sc_gather_rows(tables, idx):
    m = idx.shape[1]
    w = tables[0].shape[1]
    nt = len(tables)
    out = jax.ShapeDtypeStruct((m, w), tables[0].dtype)

    @functools.partial(pl.kernel, out_type=[out] * nt, mesh=_sc_mesh(), scratch_types=[])
    def gather_kernel(*refs):
        t_hbm, i_hbm, o_hbm = refs[:nt], refs[nt], refs[nt + 1:]
        for t, o in zip(t_hbm, o_hbm):
            def body(i_vmem, o_vmem, t=t):
                pltpu.sync_copy(t.at[i_vmem.at[0]], o_vmem)

            pltpu.emit_pipeline(
                body,
                grid=(m // SC_WINDOW,),
                in_specs=[pl.BlockSpec((1, SC_WINDOW), lambda i: (0, i))],
                out_specs=[pl.BlockSpec((SC_WINDOW, w), lambda i: (i, 0))],
                core_axis_name=("c", "s"),
                dimension_semantics=(pltpu.PARALLEL,),
            )(i_hbm, o)

    return gather_kernel(*tables, idx)


def _dest_kernel(ps_ref, eidx_ref, rank_ref, o_ref):
    eidx = eidx_ref[...]
    start = jnp.zeros_like(eidx)
    for e in range(ps_ref.shape[0]):
        start = jnp.where(eidx == e, ps_ref[e], start)
    o_ref[...] = start + rank_ref[...]


def _dest_rows(pad_start, eidx, rank):
    k, n = eidx.shape
    tn = min(8192, n)
    blk = pl.BlockSpec((k, tn), lambda i, ps: (0, i))
    return pl.pallas_call(
        _dest_kernel,
        grid_spec=pltpu.PrefetchScalarGridSpec(
            num_scalar_prefetch=1, grid=(n // tn,), in_specs=[blk, blk], out_specs=blk),
        out_shape=jax.ShapeDtypeStruct((k, n), I32),
        name="dest_rows",
    )(pad_start, eidx, rank)


def _final_kernel(x_ref, *refs):
    g_ref, sh_ref, sc_ref, o_ref = refs[N_MOE_REFS:]
    x = _moe_residual(x_ref, refs[:N_MOE_REFS])
    o_ref[0] = _rms(x) * g_ref[...] * (1.0 + sc_ref[0]) + sh_ref[0]


def _final(x, pending, gain, fmod):
    bsz, s, d = x.shape
    tm = min(ROW_TILE, s)
    blk = pl.BlockSpec((1, tm, d), lambda b, i: (b, i, 0))
    moe_specs, moe_args = _moe_residual_operands(pending, bsz, s, d, tm, lambda b, i: (b, i))
    return pl.pallas_call(
        _final_kernel,
        grid=(bsz, s // tm),
        in_specs=[blk] + moe_specs + [
            pl.BlockSpec((1, d), lambda b, i: (0, 0)),
            pl.BlockSpec((1, 1, d), lambda b, i: (b, 0, 0)),
            pl.BlockSpec((1, 1, d), lambda b, i: (b, 0, 1))],
        out_specs=blk,
        out_shape=jax.ShapeDtypeStruct((bsz, s, d), F32),
        compiler_params=pltpu.CompilerParams(dimension_semantics=("parallel", "parallel")),
        name="final_norm",
    )(x, *moe_args, gain, fmod, fmod)


def _moe_layer(h2, eidx, gate, rank, counts, layer, w1, b1, w2, b2):
    bsz, s, dh = h2[0].shape
    n = bsz * s
    ne = w1.shape[1]
    nblk = -(-(n * TOP_K) // MOE_ROWS) + ne
    n_rows = nblk * MOE_ROWS
    counts = counts[:, 0]
    padded = (counts + MOE_ROWS - 1) // MOE_ROWS * MOE_ROWS
    pad_end = jnp.cumsum(padded)
    pad_start = pad_end - padded
    block_start = jnp.arange(nblk, dtype=I32)[:, None] * MOE_ROWS
    block_e = jnp.minimum(jnp.sum(pad_end[None, :] <= block_start, axis=1), ne - 1).astype(I32)
    n_used = (pad_end[-1:] // MOE_ROWS).astype(I32)
    dest = _dest_rows(pad_start.astype(I32), eidx, rank)
    xs = _sc_scatter_rows([h.reshape(n, dh) for h in h2], [dest[k:k + 1] for k in range(TOP_K)], n_rows)
    ys = _moe_blocks(block_e, n_used, xs, layer, w1, b1, w2, b2)
    yg = _sc_gather_rows(ys, dest.reshape(1, TOP_K * n))
    return yg, gate.T


def _hgrn_lower_bounds(lb_logits):
    p = jax.nn.softmax(lb_logits.astype(F32), axis=0)
    cum = jnp.cumsum(p, axis=0)
    return cum - cum[0:1]


def kernel(x, c, positions, ada_w, ada_b, norm1_g, norm2_g, hgrn_w_in, hgrn_w_out, hgrn_o_gain, hgrn_lb_logits, ret_w_in, ret_w_out, router_w, router_b, moe_w1, moe_b1, moe_w2, moe_b2, final_g, final_ada_w, final_ada_b):
    depth = ada_w.shape[0]
    bsz, s, d = x.shape
    tm = min(ROW_TILE, s)
    assert s % tm == 0 and s % min(OUTPROJ_ROW_TILE, s) == 0 and s % RET_T == 0 and s % HG_T == 0
    assert (bsz * s) % SC_WINDOW == 0
    assert hgrn_w_in.shape[2] % (4 * HG_PAIR * HG_DK) == 0 and d % (2 * ROW_PARTS * 128) == 0
    assert router_w.shape[2] >= TOP_K == 4
    mods = _ada(c, ada_w, ada_b)
    fmod = _ada(c, final_ada_w[None], final_ada_b[None])[0][:, None, :]
    lbs = _hgrn_lower_bounds(hgrn_lb_logits)
    cos, sin = _rope_tables(positions)
    pending = None
    for layer in range(depth):
        mod = mods[layer][:, None, :]
        j = layer // N_MIXERS
        if layer % N_MIXERS == 0:
            x, proj = _inproj(x, pending, norm1_g[layer][None], mod, hgrn_w_in[j].astype(BF16))
            o = _hgrn(proj, lbs[j][None], hgrn_o_gain[j][None])
            w_out = hgrn_w_out[j]
        else:
            x, proj = _inproj(x, pending, norm1_g[layer][None], mod, ret_w_in[j].astype(BF16))
            o = _retention(proj, cos, sin)
            w_out = ret_w_out[j]
        x, ha, hb, eidx, gate, rank, counts = _outproj_route(
            o, w_out.astype(BF16), x, mod, norm2_g[layer][None],
            router_w[layer].T, router_b[layer][:, None])
        yg, gate_rows = _moe_layer((ha, hb), eidx, gate, rank, counts, layer, moe_w1, moe_b1, moe_w2, moe_b2)
        pending = (yg, gate_rows, mod)
    return _final(x, pending, final_g[None], fmod)
```

```python
import functools

import jax
import jax.numpy as jnp
from jax import lax
from jax.experimental import pallas as pl
from jax.experimental.pallas import tpu as pltpu
from jax.experimental.pallas import tpu_sc as plsc

F32 = jnp.float32
BF16 = jnp.bfloat16
I32 = jnp.int32
HIGHEST = lax.Precision.HIGHEST

EPS = 1e-6
N_MIXERS = 2
HG_DK = 128
HG_T = 64
HG_PAIR = 2
HG_SUB = 16
HG_MAX_HALF_RANGE = 80.0
RET_DK = 256
RET_DV = 512
RET_CHUNK = 64
RET_T = 256
ROPE_BASE = 10000.0
TOP_K = 4
SWIGLU_ALPHA = 1.702
SWIGLU_LIMIT = 7.0
MOE_ROWS = 1024
MOE_SUB_ROWS = 512
WEIGHT_CAST_ROWS = 128
ROW_TILE = 512
INPROJ_SUB_ROWS = 256
OUTPROJ_ROW_TILE = 1024
OUTPROJ_SUB_ROWS = 256
SC_WINDOW = 128
ROW_PARTS = 2
V7X_VMEM_BYTES = 64 * 1024 * 1024
V7X_VMEM_LIMIT = V7X_VMEM_BYTES * 7 // 8


def _dot(a, b):
    return jnp.dot(a, b, preferred_element_type=F32)


def _dot_nt(a, b):
    return lax.dot_general(a, b, (((1,), (1,)), ((), ())), preferred_element_type=F32)


def _dot_tn(a, b):
    return lax.dot_general(a, b, (((0,), (0,)), ((), ())), preferred_element_type=F32)


def _rms(x):
    return x * lax.rsqrt(jnp.mean(x * x, axis=-1, keepdims=True) + EPS)


def _norm_mod(x, gain, scale, shift):
    return _rms(x) * (gain * (1.0 + scale)) + shift


def _sigmoid(x):
    return 0.5 * jnp.tanh(0.5 * x) + 0.5


def _pack_rows(h):
    half = h.shape[1] // 2
    a = lax.bitcast_convert_type(h[:, :half].astype(BF16).astype(F32), jnp.uint32)
    b = lax.bitcast_convert_type(h[:, half:].astype(BF16).astype(F32), jnp.uint32)
    return lax.bitcast_convert_type(a | (b >> 16), I32)


def _unpack_rows(w):
    u = lax.bitcast_convert_type(w, jnp.uint32)
    a = lax.bitcast_convert_type(u & jnp.uint32(0xFFFF0000), F32)
    b = lax.bitcast_convert_type(u << 16, F32)
    return jnp.concatenate([a, b], axis=1)


def _ada_kernel(c_ref, w_ref, b_ref, o_ref):
    c = c_ref[...]
    cond = c * _sigmoid(c)
    o_ref[0] = jnp.dot(cond, w_ref[0], precision=HIGHEST, preferred_element_type=F32) + b_ref[0]


def _ada(c, w, b):
    nl, d, kd = w.shape
    bsz = c.shape[0]
    return pl.pallas_call(
        _ada_kernel,
        grid=(nl, kd // d),
        in_specs=[
            pl.BlockSpec((bsz, d), lambda l, j: (0, 0)),
            pl.BlockSpec((1, d, d), lambda l, j: (l, 0, j)),
            pl.BlockSpec((1, 1, d), lambda l, j: (l, 0, j)),
        ],
        out_specs=pl.BlockSpec((1, bsz, d), lambda l, j: (l, 0, j)),
        out_shape=jax.ShapeDtypeStruct((nl, bsz, kd), F32),
        name="ada_mod",
    )(c, w, b.reshape(nl, 1, kd))


def _moe_residual(x_ref, refs, rows=slice(None)):
    ya_refs, yb_refs = refs[:TOP_K], refs[TOP_K:2 * TOP_K]
    gate_ref, g2_ref = refs[2 * TOP_K:]
    gate = gate_ref[rows, :]
    acc = None
    for k in range(TOP_K):
        packed = jnp.concatenate([ya_refs[k][rows, :], yb_refs[k][rows, :]], axis=1)
        term = gate[:, k:k + 1] * _unpack_rows(packed)
        acc = term if acc is None else acc + term
    return x_ref[0, rows, :] + g2_ref[0] * acc


N_MOE_REFS = 2 * TOP_K + 2


def _moe_residual_operands(pending, bsz, s, d, tm, tile):
    yg, gate_rows, mod = pending
    nt = s // tm
    ntok = (bsz * s) // tm

    def flat(*g):
        b, i = tile(*g)
        return b * nt + i

    def yspec(k):
        return pl.BlockSpec((tm, d // 4), lambda *g: (k * ntok + flat(*g), 0))

    specs = [yspec(k) for k in range(TOP_K)] * 2 + [
        pl.BlockSpec((tm, TOP_K), lambda *g: (flat(*g), 0)),
        pl.BlockSpec((1, 1, d), lambda *g: (tile(*g)[0], 0, 5))]
    return specs, [yg[0]] * TOP_K + [yg[1]] * TOP_K + [gate_rows, mod]


def _inproj_kernel(x_ref, *refs, col_chunk, fused):
    if fused:
        g_ref, sh_ref, sc_ref, w_ref, o_ref, xo_ref = refs[N_MOE_REFS:]
    else:
        g_ref, sh_ref, sc_ref, w_ref, o_ref = refs
    nout = w_ref.shape[1]
    tm = x_ref.shape[1]
    sub = min(INPROJ_SUB_ROWS, tm)
    for r in range(0, tm, sub):
        rows = slice(r, r + sub)
        if fused:
            x = _moe_residual(x_ref, refs[:N_MOE_REFS], rows)
            xo_ref[0, rows, :] = x
        else:
            x = x_ref[0, rows, :]
        h = _norm_mod(x, g_ref[...], sc_ref[0], sh_ref[0])
        hb = h.astype(BF16)
        for j in range(nout // col_chunk):
            cs = slice(j * col_chunk, (j + 1) * col_chunk)
            o_ref[0, rows, cs] = _dot(hb, w_ref[:, cs]).astype(BF16)


def _inproj(x, pending, gain, mod, w):
    bsz, s, d = x.shape
    nout = w.shape[1]
    tm = min(ROW_TILE, s)
    xspec = pl.BlockSpec((1, tm, d), lambda b, i: (b, i, 0))
    fused = pending is not None
    moe_specs, moe_args = (_moe_residual_operands(pending, bsz, s, d, tm, lambda b, i: (b, i))
                           if fused else ([], []))
    proj_spec = pl.BlockSpec((1, tm, nout), lambda b, i: (b, i, 0))
    proj_shape = jax.ShapeDtypeStruct((bsz, s, nout), BF16)
    out = pl.pallas_call(
        functools.partial(_inproj_kernel, col_chunk=1024, fused=fused),
        grid=(bsz, s // tm),
        in_specs=[xspec] + moe_specs + [
            pl.BlockSpec((1, d), lambda b, i: (0, 0)),
            pl.BlockSpec((1, 1, d), lambda b, i: (b, 0, 0)),
            pl.BlockSpec((1, 1, d), lambda b, i: (b, 0, 1)),
            pl.BlockSpec((d, nout), lambda b, i: (0, 0), pipeline_mode=pl.Buffered(1)),
        ],
        out_specs=[proj_spec, xspec] if fused else proj_spec,
        out_shape=[proj_shape, jax.ShapeDtypeStruct((bsz, s, d), F32)] if fused else proj_shape,
        compiler_params=pltpu.CompilerParams(
            dimension_semantics=("parallel", "parallel"), vmem_limit_bytes=V7X_VMEM_LIMIT),
        name="inproj",
    )(x, *moe_args, gain, mod, mod, w)
    return (out[1], out[0]) if fused else (x, out)


def _hgrn_gates(q, f, lb, one_m_lb):
    e = jnp.exp(-jnp.abs(f))
    inv = 1.0 / (1.0 + e)
    pos = f >= 0.0
    t = e * inv
    sig = jnp.where(pos, inv, t)
    sig_neg = jnp.where(pos, t, inv)
    has_lb = lb > 0.0
    logf = jnp.log(jnp.where(has_lb, lb + one_m_lb * sig, inv)) + jnp.where(has_lb, 0.0, jnp.minimum(f, 0.0))
    return q * _sigmoid(q), one_m_lb * sig_neg, logf


def _hgrn_kernel(q_ref, f_ref, i_ref, g_ref, lb_ref, gain_ref, o_ref, kk_s, v_s, b_s,
                 qt_s, qd_s, kd_s, ktt_s, dec_s):
    t = HG_T
    dk = HG_DK
    w = q_ref.shape[2]
    n_chunks = q_ref.shape[1] // t
    lb = lb_ref[...]
    one_m_lb = 1.0 - lb
    gain = gain_ref[...]
    heads = [slice(h * dk, (h + 1) * dk) for h in range(HG_PAIR)]
    row = lax.broadcasted_iota(I32, (t, HG_PAIR * t), 0)
    col = lax.broadcasted_iota(I32, (t, HG_PAIR * t), 1)
    causal = row >= (col % t)
    tril = causal[:, :t].astype(BF16)

    def finish(o, g):
        return (_rms(o) * gain * (g * _sigmoid(g))).astype(BF16)

    def block_diag(parts):
        rows = []
        for h, p in enumerate(parts):
            z = jnp.zeros_like(p)
            rows.append(jnp.concatenate([p if j == h else z for j in range(HG_PAIR)], axis=1))
        return jnp.concatenate(rows, axis=0)

    def prepare(c, bmax):
        rows = pl.ds(pl.multiple_of(c * t, t), t)
        q = q_ref[0, rows, :].astype(F32)
        f = f_ref[0, rows, :].astype(F32)
        qs, kk, logf = _hgrn_gates(q, f, lb, one_m_lb)
        hi = logf.astype(BF16)
        lo = (logf - hi.astype(F32)).astype(BF16)
        bb = _dot(tril, jnp.concatenate([hi, lo], axis=-1))
        b = bb[:, :w] + bb[:, w:]
        b_last = b[t - 1:t, :]
        mid = 0.5 * b_last
        e_mid = jnp.exp(mid)
        qt = qs * jnp.exp(b - mid)
        kt = kk * jnp.exp(mid - b)
        qt_s[rows, :] = qt.astype(BF16)
        qd_s[rows, :] = (qt * e_mid).astype(BF16)
        kd_s[rows, :] = (kt * e_mid).astype(BF16)
        wrows = pl.ds(pl.multiple_of(c * w, w), w)
        ktt_s[wrows, :] = block_diag([kt[:, hs] for hs in heads]).T.astype(BF16)
        dec_s[pl.ds(pl.multiple_of(c * 8, 8), 8), :] = jnp.broadcast_to(e_mid * e_mid, (8, w))
        return jnp.maximum(bmax, jnp.abs(b_last))

    bmax = lax.fori_loop(0, n_chunks, prepare, jnp.zeros_like(lb), unroll=8)

    def chunk(c, sts):
        rows = pl.ds(pl.multiple_of(c * t, t), t)
        wrows = pl.ds(pl.multiple_of(c * w, w), w)
        v = i_ref[0, rows, :]
        g = g_ref[0, rows, :].astype(F32)
        kd = kd_s[rows, :]
        dec_row = dec_s[pl.ds(pl.multiple_of(c * 8, 8), 1), :]
        decay = jnp.broadcast_to(dec_row, (dk, w)).T
        att = _dot(qt_s[rows, :], ktt_s[wrows, :])
        att = jnp.where(causal, att, 0.0).astype(BF16)
        o = _dot(att, block_diag([v[:, hs] for hs in heads]))
        st_bd = block_diag([st.astype(BF16) for st in sts])
        o = o + _dot(qd_s[rows, :], st_bd)
        sts = tuple(st * decay[hs, :] + _dot_tn(kd[:, hs], v[:, hs]) for st, hs in zip(sts, heads))
        out = [finish(o[:, hs], g[:, hs]) for hs in heads]
        o_ref[0, rows, :] = jnp.concatenate(out, axis=1)
        return sts

    st0 = jnp.zeros((dk, dk), F32)
    lax.fori_loop(0, n_chunks, chunk, (st0,) * HG_PAIR, unroll=16)
    safe = 0.5 * jnp.max(bmax) <= HG_MAX_HALF_RANGE

    @pl.when(jnp.logical_not(safe))
    def _():
        n = HG_SUB
        sub_row = lax.broadcasted_iota(I32, (n, 1), 0)
        tril_n = (lax.broadcasted_iota(I32, (n, n), 0) >= lax.broadcasted_iota(I32, (n, n), 1)).astype(F32)

        for hs in heads:
            def block(i, st, hs=hs):
                rows = pl.ds(pl.multiple_of(i * n, n), n)
                q = q_ref[0, rows, hs].astype(F32)
                f = f_ref[0, rows, hs].astype(F32)
                v = i_ref[0, rows, hs]
                g = g_ref[0, rows, hs].astype(F32)
                qs, kk, logf = _hgrn_gates(q, f, lb[:, hs], one_m_lb[:, hs])
                b = jnp.dot(tril_n, logf, precision=HIGHEST, preferred_element_type=F32)
                kk_s[...] = kk
                v_s[...] = v.astype(F32)
                b_s[...] = b
                o = _dot_nt((qs * jnp.exp(b)).astype(BF16), st.astype(BF16))

                def pair(s, acc):
                    dec = jnp.exp(jnp.minimum(b - b_s[pl.ds(s, 1), :], 0.0))
                    wgt = jnp.sum(qs * kk_s[pl.ds(s, 1), :] * dec, axis=-1, keepdims=True)
                    return acc + jnp.where(sub_row >= s, wgt, 0.0) * v_s[pl.ds(s, 1), :]

                o = lax.fori_loop(0, n, pair, o)
                b_last = b[n - 1:n, :]
                st = st * jnp.exp(b_last) + _dot_tn(v, (kk * jnp.exp(b_last - b)).astype(BF16))
                o_ref[0, rows, hs] = finish(o, g)
                return st

            lax.fori_loop(0, q_ref.shape[1] // n, block, st0)


def _hgrn(proj, lb, gain):
    bsz, s, w4 = proj.shape
    w = HG_PAIR * HG_DK
    npair = w4 // (4 * w)

    def spec(j):
        return pl.BlockSpec((1, s, w), lambda b, p: (b, 0, p + j * npair))

    return pl.pallas_call(
        _hgrn_kernel,
        grid=(bsz, npair),
        in_specs=[spec(0), spec(1), spec(2), spec(3),
                  pl.BlockSpec((1, w), lambda b, p: (0, p)),
                  pl.BlockSpec((1, HG_DK), lambda b, p: (0, 0))],
        out_specs=pl.BlockSpec((1, s, w), lambda b, p: (b, 0, p)),
        out_shape=jax.ShapeDtypeStruct((bsz, s, npair * w), BF16),
        scratch_shapes=[pltpu.VMEM((HG_SUB, HG_DK), F32)] * 3 + [pltpu.VMEM((s, w), BF16)] * 3
        + [pltpu.VMEM((s // HG_T * w, HG_PAIR * HG_T), BF16), pltpu.VMEM((s // HG_T * 8, w), F32)],
        compiler_params=pltpu.CompilerParams(dimension_semantics=("parallel", "parallel")),
        name="hgrn",
    )(proj, proj, proj, proj, lb, gain)


def _rope_kernel(pos_ref, inv_ref, cos_ref, sin_ref):
    ang = pos_ref[0].astype(F32) * inv_ref[...]
    cos_ref[0] = jnp.cos(ang)
    sin_ref[0] = jnp.sin(ang)


def _rope_tables(positions):
    bsz, s = positions.shape
    half = RET_DK // 2
    inv_freq = (1.0 / (ROPE_BASE ** jnp.linspace(0.0, 1.0, half, dtype=F32))).reshape(1, half)
    out = jax.ShapeDtypeStruct((bsz, s, half), F32)
    return pl.pallas_call(
        _rope_kernel,
        grid=(bsz,),
        in_specs=[pl.BlockSpec((1, s, 1), lambda b: (b, 0, 0)),
                  pl.BlockSpec((1, half), lambda b: (0, 0))],
        out_specs=[pl.BlockSpec((1, s, half), lambda b: (b, 0, 0))] * 2,
        out_shape=[out, out],
        name="rope_tables",
    )(positions.reshape(bsz, s, 1), inv_freq)


def _ret_kernel(q_ref, k_ref, v_ref, g_ref, cos_ref, sin_ref, lg_ref, o_ref, r_ref, d_ref):
    t = RET_T
    dk = RET_DK
    half = dk // 2
    n_steps = q_ref.shape[1] // t
    lg = lg_ref[0]
    lg_k = lg[:, :dk]
    n = lax.broadcasted_iota(I32, (t, t), 0)
    m = lax.broadcasted_iota(I32, (t, t), 1)
    dist = jnp.abs(n - m).astype(F32)
    visible = (m // RET_CHUNK) <= (n // RET_CHUNK)
    k_scale = dk ** -0.5
    d_ref[...] = jnp.where(visible, k_scale * jnp.exp(dist * lg[:, :t]), 0.0)
    idx = lax.broadcasted_iota(I32, (t, dk), 0).astype(F32)
    q_decay = jnp.exp((idx + 1.0) * lg_k)
    k_decay = k_scale * jnp.exp((t - 1.0 - idx) * lg_k)
    step_decay = jnp.exp(float(t) * lg)
    r_ref[...] = jnp.zeros_like(r_ref)

    def rotate(x, cos, sin):
        x1, x2 = x[:, :half], x[:, half:]
        return jnp.concatenate([x1 * cos - x2 * sin, x2 * cos + x1 * sin], axis=-1)

    def step(c, carry):
        r0 = pl.multiple_of(c * t, t)
        rows = pl.ds(r0, t)
        cos = cos_ref[0, rows, :]
        sin = sin_ref[0, rows, :]
        q = rotate(q_ref[0, rows, :].astype(F32), cos, sin)
        k = rotate(k_ref[0, rows, :].astype(F32), cos, sin)
        v = v_ref[0, rows, :]
        g = g_ref[0, rows, :].astype(F32)
        r = r_ref[...]
        sc = _dot_nt(q.astype(BF16), k.astype(BF16)) * d_ref[...]
        o = _dot(sc.astype(BF16), v) + _dot((q * q_decay).astype(BF16), r.astype(BF16))
        r_ref[...] = r * step_decay + _dot_tn((k * k_decay).astype(BF16), v)
        out = _rms(o) * (g * _sigmoid(g))
        o_ref[0, rows, :] = out.astype(BF16)
        return carry

    lax.fori_loop(0, n_steps, step, 0, unroll=2)


def _retention(proj, cos, sin):
    bsz, s, w = proj.shape
    dk, dv = RET_DK, RET_DV
    nh = w // (2 * dk + 2 * dv)
    hidx = jnp.arange(nh, dtype=F32)
    log_gamma = jnp.log(1.0 - 2.0 ** (-5.0 - hidx))
    lg = jnp.broadcast_to(log_gamma[:, None, None], (nh, 1, dv))
    vbase = 2 * nh * dk // dv
    return pl.pallas_call(
        _ret_kernel,
        grid=(bsz, nh),
        in_specs=[
            pl.BlockSpec((1, s, dk), lambda b, h: (b, 0, h)),
            pl.BlockSpec((1, s, dk), lambda b, h: (b, 0, nh + h)),
            pl.BlockSpec((1, s, dv), lambda b, h: (b, 0, vbase + h)),
            pl.BlockSpec((1, s, dv), lambda b, h: (b, 0, vbase + nh + h)),
            pl.BlockSpec((1, s, dk // 2), lambda b, h: (b, 0, 0)),
            pl.BlockSpec((1, s, dk // 2), lambda b, h: (b, 0, 0)),
            pl.BlockSpec((1, 1, dv), lambda b, h: (h, 0, 0)),
        ],
        out_specs=pl.BlockSpec((1, s, dv), lambda b, h: (b, 0, h)),
        out_shape=jax.ShapeDtypeStruct((bsz, s, nh * dv), BF16),
        scratch_shapes=[pltpu.VMEM((dk, dv), F32), pltpu.VMEM((RET_T, RET_T), F32)],
        compiler_params=pltpu.CompilerParams(
            dimension_semantics=("parallel", "parallel"), vmem_limit_bytes=V7X_VMEM_LIMIT),
        name="retention",
    )(proj, proj, proj, proj, cos, sin, lg)


def _outproj_kernel(o_ref, w_ref, x_ref, g1_ref, sh_ref, sc_ref, gain_ref, rw_ref, rb_ref, u_ref,
                    xo_ref, ha_ref, hb_ref, eidx_ref, gate_ref, rank_ref, cnt_ref, base_ref):
    first = jnp.logical_and(pl.program_id(0) == 0, pl.program_id(1) == 0)

    @pl.when(first)
    def _():
        base_ref[...] = jnp.zeros_like(base_ref)

    ne = rw_ref.shape[0]
    rw = rw_ref[...]
    rw_hi = rw.astype(BF16)
    rw_lo = (rw - rw_hi.astype(F32)).astype(BF16)
    rw_both = jnp.concatenate([rw_hi, rw_lo], axis=0)
    tm = x_ref.shape[1]
    sub = min(OUTPROJ_SUB_ROWS, tm)
    h_groups = []
    for r in range(0, tm, sub):
        rows = slice(r, r + sub)
        y = _dot(o_ref[0, rows, :], w_ref[...])
        xn = x_ref[0, rows, :] + g1_ref[0] * y
        xo_ref[0, rows, :] = xn
        h = _norm_mod(xn, gain_ref[...], sc_ref[0], sh_ref[0])
        packed = _pack_rows(h)
        slab = packed.shape[1] // ROW_PARTS
        ha_ref[0, rows, :] = packed[:, :slab]
        hb_ref[0, rows, :] = packed[:, slab:]
        h_groups.append(h.astype(BF16))

    part = _dot_nt(rw_both, jnp.concatenate(h_groups, axis=0))
    work = part[:ne] + part[ne:] + rb_ref[...]
    eiota = lax.broadcasted_iota(I32, (ne, tm), 0)
    onehots, tops = [], []
    for k in range(TOP_K):
        mx = jnp.max(work, axis=0, keepdims=True)
        idx = jnp.min(jnp.where(work == mx, eiota, ne), axis=0, keepdims=True)
        oh = eiota == idx
        work = jnp.where(oh, -jnp.inf, work)
        eidx_ref[k:k + 1, :] = idx
        onehots.append(oh)
        tops.append(mx)
    ex = [jnp.exp(m - tops[0]) for m in tops]
    denom = ex[0] + ex[1] + ex[2] + ex[3]
    for k in range(TOP_K):
        gate_ref[k:k + 1, :] = ex[k] / denom

    mask = jnp.zeros((ne, tm), F32)
    for oh in onehots:
        mask = mask + oh.astype(F32)
    incl = _dot(mask.astype(BF16), u_ref[...])
    excl = incl - mask + base_ref[...]
    for k in range(TOP_K):
        rk = jnp.sum(jnp.where(onehots[k], excl, 0.0), axis=0, keepdims=True)
        rank_ref[k:k + 1, :] = rk.astype(I32)
    total = base_ref[...] + incl[:, tm - 1:tm]
    base_ref[...] = total
    cnt_ref[...] = total.astype(I32)


def _outproj_route(o, w_out, x, mod, gain2, router_wt, router_b):
    bsz, s, d = x.shape
    hv = o.shape[2]
    tm = min(OUTPROJ_ROW_TILE, s)
    n = bsz * s
    nt = s // tm
    ne = router_wt.shape[0]
    upper = (jnp.arange(tm)[:, None] <= jnp.arange(tm)[None, :]).astype(BF16)

    def modspec(j):
        return pl.BlockSpec((1, 1, d), lambda b, i: (b, 0, j))

    tokspec = pl.BlockSpec((TOP_K, tm), lambda b, i: (0, b * nt + i))
    return pl.pallas_call(
        _outproj_kernel,
        grid=(bsz, nt),
        in_specs=[
            pl.BlockSpec((1, tm, hv), lambda b, i: (b, i, 0)),
            pl.BlockSpec((hv, d), lambda b, i: (0, 0)),
            pl.BlockSpec((1, tm, d), lambda b, i: (b, i, 0)),
            modspec(2), modspec(3), modspec(4),
            pl.BlockSpec((1, d), lambda b, i: (0, 0)),
            pl.BlockSpec((ne, d), lambda b, i: (0, 0)),
            pl.BlockSpec((ne, 1), lambda b, i: (0, 0)),
            pl.BlockSpec((tm, tm), lambda b, i: (0, 0)),
        ],
        out_specs=[
            pl.BlockSpec((1, tm, d), lambda b, i: (b, i, 0)),
            pl.BlockSpec((1, tm, d // 4), lambda b, i: (b, i, 0)),
            pl.BlockSpec((1, tm, d // 4), lambda b, i: (b, i, 0)),
            tokspec, tokspec, tokspec,
            pl.BlockSpec((ne, 1), lambda b, i: (0, 0)),
        ],
        out_shape=[
            jax.ShapeDtypeStruct((bsz, s, d), F32),
            jax.ShapeDtypeStruct((bsz, s, d // 4), I32),
            jax.ShapeDtypeStruct((bsz, s, d // 4), I32),
            jax.ShapeDtypeStruct((TOP_K, n), I32),
            jax.ShapeDtypeStruct((TOP_K, n), F32),
            jax.ShapeDtypeStruct((TOP_K, n), I32),
            jax.ShapeDtypeStruct((ne, 1), I32),
        ],
        scratch_shapes=[pltpu.VMEM((ne, 1), F32)],
        compiler_params=pltpu.CompilerParams(
            dimension_semantics=("arbitrary", "arbitrary"), vmem_limit_bytes=V7X_VMEM_LIMIT),
        name="outproj_route",
    )(o, w_out, x, mod, mod, mod, gain2, router_wt, router_b, upper)


def _moe_kernel(be_ref, nb_ref, xa_ref, xb_ref, w1_ref, b1_ref, w2_ref, b2_ref, ya_ref, yb_ref,
                w1_s, w2_s):
    i = pl.program_id(0)
    used = i < nb_ref[0]
    new_expert = jnp.logical_or(i == 0, be_ref[i] != be_ref[jnp.maximum(i - 1, 0)])

    @pl.when(jnp.logical_and(used, new_expert))
    def _():
        for w_ref, w_s in ((w1_ref, w1_s), (w2_ref, w2_s)):
            for r in range(0, w_s.shape[0], WEIGHT_CAST_ROWS):
                rows = slice(r, r + WEIGHT_CAST_ROWS)
                w_s[rows, :] = w_ref[0, 0, rows, :].astype(BF16)

    @pl.when(used)
    def _():
        f = w2_s.shape[0]
        for r in range(0, xa_ref.shape[0], MOE_SUB_ROWS):
            rows = slice(r, r + MOE_SUB_ROWS)
            x = _unpack_rows(jnp.concatenate([xa_ref[rows, :], xb_ref[rows, :]], axis=1)).astype(BF16)
            u = _dot(x, w1_s[...]) + b1_ref[0, 0]
            glu = jnp.minimum(u[:, :f], SWIGLU_LIMIT)
            lin = jnp.clip(u[:, f:], -SWIGLU_LIMIT, SWIGLU_LIMIT)
            a = glu * _sigmoid(SWIGLU_ALPHA * glu) * (lin + 1.0)
            y = _dot(a.astype(BF16), w2_s[...]) + b2_ref[0, 0]
            packed = _pack_rows(y)
            slab = packed.shape[1] // ROW_PARTS
            ya_ref[rows, :] = packed[:, :slab]
            yb_ref[rows, :] = packed[:, slab:]

    @pl.when(jnp.logical_not(used))
    def _():
        ya_ref[...] = jnp.zeros_like(ya_ref)
        yb_ref[...] = jnp.zeros_like(yb_ref)


def _moe_blocks(block_e, n_used, xs, layer, w1, b1, w2, b2):
    n_rows, dh = xs[0].shape
    nl, ne, d, f2 = w1.shape
    f = f2 // 2
    nblk = n_rows // MOE_ROWS
    grid_spec = pltpu.PrefetchScalarGridSpec(
        num_scalar_prefetch=2,
        grid=(nblk,),
        in_specs=[
            pl.BlockSpec((MOE_ROWS, dh), lambda i, be, nb: (i, 0)),
            pl.BlockSpec((MOE_ROWS, dh), lambda i, be, nb: (i, 0)),
            pl.BlockSpec((1, 1, d, f2), lambda i, be, nb: (layer, be[i], 0, 0)),
            pl.BlockSpec((1, 1, 1, f2), lambda i, be, nb: (layer, be[i], 0, 0)),
            pl.BlockSpec((1, 1, f, d), lambda i, be, nb: (layer, be[i], 0, 0)),
            pl.BlockSpec((1, 1, 1, d), lambda i, be, nb: (layer, be[i], 0, 0)),
        ],
        out_specs=[pl.BlockSpec((MOE_ROWS, dh), lambda i, be, nb: (i, 0))] * 2,
        scratch_shapes=[pltpu.VMEM((d, f2), BF16), pltpu.VMEM((f, d), BF16)],
    )
    return pl.pallas_call(
        _moe_kernel,
        grid_spec=grid_spec,
        out_shape=[jax.ShapeDtypeStruct((n_rows, dh), I32)] * 2,
        compiler_params=pltpu.CompilerParams(
            dimension_semantics=("arbitrary",), vmem_limit_bytes=V7X_VMEM_LIMIT),
        name="moe_experts",
    )(block_e, n_used, xs[0], xs[1], w1, b1.reshape(nl, ne, 1, f2), w2, b2.reshape(nl, ne, 1, d))


def _sc_mesh():
    return plsc.VectorSubcoreMesh(core_axis_name="c", subcore_axis_name="s")


def _sc_scatter_rows(srcs, dests, n_rows):
    n, w = srcs[0].shape
    ns, nk = len(srcs), len(dests)
    out = jax.ShapeDtypeStruct((n_rows, w), srcs[0].dtype)

    @functools.partial(pl.kernel, out_type=[out] * ns, mesh=_sc_mesh(),
                       scratch_types=[pltpu.SemaphoreType.DMA((nk,))])
    def scatter_kernel(*refs):
        x_hbm, idx_hbm, o_hbm, sems = refs[:ns], refs[ns:ns + nk], refs[ns + nk:-1], refs[-1]
        for x, o in zip(x_hbm, o_hbm):
            def body(x_vmem, *idx_vmem, o=o):
                copies = [pltpu.async_copy(x_vmem, o.at[iv.at[0]], sems.at[k])
                          for k, iv in enumerate(idx_vmem)]
                for cp in copies:
                    cp.wait()

            pltpu.emit_pipeline(
                body,
                grid=(n // SC_WINDOW,),
                in_specs=[pl.BlockSpec((SC_WINDOW, w), lambda i: (i, 0))]
                + [pl.BlockSpec((1, SC_WINDOW), lambda i: (0, i))] * nk,
                out_specs=[],
                core_axis_name=("c", "s"),
                dimension_semantics=(pltpu.PARALLEL,),
            )(x, *idx_hbm)

    return scatter_kernel(*srcs, *dests)


def _sc_gather_rows(tables, idx):
    m = idx.shape[1]
    w = tables[0].shape[1]
    nt = len(tables)
    out = jax.ShapeDtypeStruct((m, w), tables[0].dtype)

    @functools.partial(pl.kernel, out_type=[out] * nt, mesh=_sc_mesh(), scratch_types=[])
    def gather_kernel(*refs):
        t_hbm, i_hbm, o_hbm = refs[:nt], refs[nt], refs[nt + 1:]
        for t, o in zip(t_hbm, o_hbm):
            def body(i_vmem, o_vmem, t=t):
                pltpu.sync_copy(t.at[i_vmem.at[0]], o_vmem)

            pltpu.emit_pipeline(
                body,
                grid=(m // SC_WINDOW,),
                in_specs=[pl.BlockSpec((1, SC_WINDOW), lambda i: (0, i))],
                out_specs=[pl.BlockSpec((SC_WINDOW, w), lambda i: (i, 0))],
                core_axis_name=("c", "s"),
                dimension_semantics=(pltpu.PARALLEL,),
            )(i_hbm, o)

    return gather_kernel(*tables, idx)


def _dest_kernel(ps_ref, eidx_ref, rank_ref, o_ref):
    eidx = eidx_ref[...]
    start = jnp.zeros_like(eidx)
    for e in range(ps_ref.shape[0]):
        start = jnp.where(eidx == e, ps_ref[e], start)
    o_ref[...] = start + rank_ref[...]


def _dest_rows(pad_start, eidx, rank):
    k, n = eidx.shape
    tn = min(8192, n)
    blk = pl.BlockSpec((k, tn), lambda i, ps: (0, i))
    return pl.pallas_call(
        _dest_kernel,
        grid_spec=pltpu.PrefetchScalarGridSpec(
            num_scalar_prefetch=1, grid=(n // tn,), in_specs=[blk, blk], out_specs=blk),
        out_shape=jax.ShapeDtypeStruct((k, n), I32),
        name="dest_rows",
    )(pad_start, eidx, rank)


def _final_kernel(x_ref, *refs):
    g_ref, sh_ref, sc_ref, o_ref = refs[N_MOE_REFS:]
    x = _moe_residual(x_ref, refs[:N_MOE_REFS])
    o_ref[0] = _norm_mod(x, g_ref[...], sc_ref[0], sh_ref[0])


def _final(x, pending, gain, fmod):
    bsz, s, d = x.shape
    tm = min(ROW_TILE, s)
    blk = pl.BlockSpec((1, tm, d), lambda b, i: (b, i, 0))
    moe_specs, moe_args = _moe_residual_operands(pending, bsz, s, d, tm, lambda b, i: (b, i))
    return pl.pallas_call(
        _final_kernel,
        grid=(bsz, s // tm),
        in_specs=[blk] + moe_specs + [
            pl.BlockSpec((1, d), lambda b, i: (0, 0)),
            pl.BlockSpec((1, 1, d), lambda b, i: (b, 0, 0)),
            pl.BlockSpec((1, 1, d), lambda b, i: (b, 0, 1))],
        out_specs=blk,
        out_shape=jax.ShapeDtypeStruct((bsz, s, d), F32),
        compiler_params=pltpu.CompilerParams(dimension_semantics=("parallel", "parallel")),
        name="final_norm",
    )(x, *moe_args, gain, fmod, fmod)


def _moe_layer(h2, eidx, gate, rank, counts, layer, w1, b1, w2, b2):
    bsz, s, dh = h2[0].shape
    n = bsz * s
    ne = w1.shape[1]
    nblk = -(-(n * TOP_K) // MOE_ROWS) + ne
    n_rows = nblk * MOE_ROWS
    counts = counts[:, 0]
    padded = (counts + MOE_ROWS - 1) // MOE_ROWS * MOE_ROWS
    pad_end = jnp.cumsum(padded)
    pad_start = pad_end - padded
    block_start = jnp.arange(nblk, dtype=I32)[:, None] * MOE_ROWS
    block_e = jnp.minimum(jnp.sum(pad_end[None, :] <= block_start, axis=1), ne - 1).astype(I32)
    n_used = (pad_end[-1:] // MOE_ROWS).astype(I32)
    dest = _dest_rows(pad_start.astype(I32), eidx, rank)
    xs = _sc_scatter_rows([h.reshape(n, dh) for h in h2], [dest[k:k + 1] for k in range(TOP_K)], n_rows)
    ys = _moe_blocks(block_e, n_used, xs, layer, w1, b1, w2, b2)
    yg = _sc_gather_rows(ys, dest.reshape(1, TOP_K * n))
    return yg, gate.T


def _hgrn_lower_bounds(lb_logits):
    p = jax.nn.softmax(lb_logits.astype(F32), axis=0)
    cum = jnp.cumsum(p, axis=0)
    return cum - cum[0:1]


def kernel(x, c, positions, ada_w, ada_b, norm1_g, norm2_g, hgrn_w_in, hgrn_w_out, hgrn_o_gain, hgrn_lb_logits, ret_w_in, ret_w_out, router_w, router_b, moe_w1, moe_b1, moe_w2, moe_b2, final_g, final_ada_w, final_ada_b):
    depth = ada_w.shape[0]
    bsz, s, d = x.shape
    tm = min(ROW_TILE, s)
    assert s % tm == 0 and s % min(OUTPROJ_ROW_TILE, s) == 0 and s % RET_T == 0 and s % HG_T == 0
    assert (bsz * s) % SC_WINDOW == 0
    assert hgrn_w_in.shape[2] % (4 * HG_PAIR * HG_DK) == 0 and d % (2 * ROW_PARTS * 128) == 0
    assert router_w.shape[2] >= TOP_K == 4
    mods = _ada(c, ada_w, ada_b)
    fmod = _ada(c, final_ada_w[None], final_ada_b[None])[0][:, None, :]
    lbs = _hgrn_lower_bounds(hgrn_lb_logits)
    cos, sin = _rope_tables(positions)
    pending = None
    for layer in range(depth):
        mod = mods[layer][:, None, :]
        j = layer // N_MIXERS
        if layer % N_MIXERS == 0:
            x, proj = _inproj(x, pending, norm1_g[layer][None], mod, hgrn_w_in[j].astype(BF16))
            o = _hgrn(proj, lbs[j][None], hgrn_o_gain[j][None])
            w_out = hgrn_w_out[j]
        else:
            x, proj = _inproj(x, pending, norm1_g[layer][None], mod, ret_w_in[j].astype(BF16))
            o = _retention(proj, cos, sin)
            w_out = ret_w_out[j]
        x, ha, hb, eidx, gate, rank, counts = _outproj_route(
            o, w_out.astype(BF16), x, mod, norm2_g[layer][None],
            router_w[layer].T, router_b[layer][:, None])
        yg, gate_rows = _moe_layer((ha, hb), eidx, gate, rank, counts, layer, moe_w1, moe_b1, moe_w2, moe_b2)
        pending = (yg, gate_rows, mod)
    return _final(x, pending, final_g[None], fmod)
```

```python
import functools

import jax
import jax.numpy as jnp
from jax import lax
from jax.experimental import pallas as pl
from jax.experimental.pallas import tpu as pltpu
from jax.experimental.pallas import tpu_sc as plsc

F32 = jnp.float32
BF16 = jnp.bfloat16
I32 = jnp.int32
HIGHEST = lax.Precision.HIGHEST

EPS = 1e-6
N_MIXERS = 2
HG_DK = 128
HG_T = 64
HG_PAIR = 2
HG_SUB = 16
HG_MAX_HALF_RANGE = 80.0
RET_DK = 256
RET_DV = 512
RET_CHUNK = 64
RET_T = 256
ROPE_BASE = 10000.0
TOP_K = 4
GATE_LANES = 8
SWIGLU_ALPHA = 1.702
SWIGLU_LIMIT = 7.0
MOE_ROWS = 1024
MOE_SUB_ROWS = 512
WEIGHT_CAST_ROWS = 128
ROW_TILE = 512
INPROJ_SUB_ROWS = 256
OUTPROJ_ROW_TILE = 1024
OUTPROJ_SUB_ROWS = 256
SC_WINDOW = 128
ROW_PARTS = 2
V7X_VMEM_BYTES = 64 * 1024 * 1024
V7X_VMEM_LIMIT = V7X_VMEM_BYTES * 7 // 8


def _dot(a, b):
    return jnp.dot(a, b, preferred_element_type=F32)


def _dot_nt(a, b):
    return lax.dot_general(a, b, (((1,), (1,)), ((), ())), preferred_element_type=F32)


def _dot_tn(a, b):
    return lax.dot_general(a, b, (((0,), (0,)), ((), ())), preferred_element_type=F32)


def _rms(x):
    return x * lax.rsqrt(jnp.mean(x * x, axis=-1, keepdims=True) + EPS)


def _norm_mod(x, gain, scale, shift):
    return _rms(x) * (gain * (1.0 + scale)) + shift


def _sigmoid(x):
    return 0.5 * jnp.tanh(0.5 * x) + 0.5


def _pack_rows(h):
    half = h.shape[1] // 2
    a = lax.bitcast_convert_type(h[:, :half].astype(BF16).astype(F32), jnp.uint32)
    b = lax.bitcast_convert_type(h[:, half:].astype(BF16).astype(F32), jnp.uint32)
    return lax.bitcast_convert_type(a | (b >> 16), I32)


def _unpack_rows(w):
    u = lax.bitcast_convert_type(w, jnp.uint32)
    a = lax.bitcast_convert_type(u & jnp.uint32(0xFFFF0000), F32)
    b = lax.bitcast_convert_type(u << 16, F32)
    return jnp.concatenate([a, b], axis=1)


def _ada_kernel(c_ref, w_ref, b_ref, o_ref):
    c = c_ref[...]
    cond = c * _sigmoid(c)
    o_ref[0] = jnp.dot(cond, w_ref[0], precision=HIGHEST, preferred_element_type=F32) + b_ref[0]


def _ada(c, w, b):
    nl, d, kd = w.shape
    bsz = c.shape[0]
    return pl.pallas_call(
        _ada_kernel,
        grid=(nl, kd // d),
        in_specs=[
            pl.BlockSpec((bsz, d), lambda l, j: (0, 0)),
            pl.BlockSpec((1, d, d), lambda l, j: (l, 0, j)),
            pl.BlockSpec((1, 1, d), lambda l, j: (l, 0, j)),
        ],
        out_specs=pl.BlockSpec((1, bsz, d), lambda l, j: (l, 0, j)),
        out_shape=jax.ShapeDtypeStruct((nl, bsz, kd), F32),
        name="ada_mod",
    )(c, w, b.reshape(nl, 1, kd))


def _moe_residual(x_ref, refs, rows=slice(None)):
    ya_refs, yb_refs = refs[:TOP_K], refs[TOP_K:2 * TOP_K]
    gate_ref, g2_ref = refs[2 * TOP_K:]
    gate = gate_ref[rows, :]
    acc = None
    for k in range(TOP_K):
        packed = jnp.concatenate([ya_refs[k][rows, :], yb_refs[k][rows, :]], axis=1)
        term = gate[:, k:k + 1] * _unpack_rows(packed)
        acc = term if acc is None else acc + term
    return x_ref[0, rows, :] + g2_ref[0] * acc


N_MOE_REFS = 2 * TOP_K + 2


def _moe_residual_operands(pending, bsz, s, d, tm, tile):
    yg, gate_rows, mod = pending
    nt = s // tm
    ntok = (bsz * s) // tm

    def flat(*g):
        b, i = tile(*g)
        return b * nt + i

    def yspec(k):
        return pl.BlockSpec((tm, d // 4), lambda *g: (k * ntok + flat(*g), 0))

    specs = [yspec(k) for k in range(TOP_K)] * 2 + [
        pl.BlockSpec((tm, GATE_LANES), lambda *g: (flat(*g), 0)),
        pl.BlockSpec((1, 1, d), lambda *g: (tile(*g)[0], 0, 5))]
    return specs, [yg[0]] * TOP_K + [yg[1]] * TOP_K + [gate_rows, mod]


def _inproj_kernel(x_ref, *refs, col_chunk, fused):
    if fused:
        g_ref, sh_ref, sc_ref, w_ref, o_ref, xo_ref = refs[N_MOE_REFS:]
    else:
        g_ref, sh_ref, sc_ref, w_ref, o_ref = refs
    nout = w_ref.shape[1]
    tm = x_ref.shape[1]
    sub = min(INPROJ_SUB_ROWS, tm)
    for r in range(0, tm, sub):
        rows = slice(r, r + sub)
        if fused:
            x = _moe_residual(x_ref, refs[:N_MOE_REFS], rows)
            xo_ref[0, rows, :] = x
        else:
            x = x_ref[0, rows, :]
        h = _norm_mod(x, g_ref[...], sc_ref[0], sh_ref[0])
        hb = h.astype(BF16)
        for j in range(nout // col_chunk):
            cs = slice(j * col_chunk, (j + 1) * col_chunk)
            o_ref[0, rows, cs] = _dot(hb, w_ref[:, cs]).astype(BF16)


def _inproj(x, pending, gain, mod, w):
    bsz, s, d = x.shape
    nout = w.shape[1]
    tm = min(ROW_TILE, s)
    xspec = pl.BlockSpec((1, tm, d), lambda b, i: (b, i, 0))
    fused = pending is not None
    moe_specs, moe_args = (_moe_residual_operands(pending, bsz, s, d, tm, lambda b, i: (b, i))
                           if fused else ([], []))
    proj_spec = pl.BlockSpec((1, tm, nout), lambda b, i: (b, i, 0))
    proj_shape = jax.ShapeDtypeStruct((bsz, s, nout), BF16)
    out = pl.pallas_call(
        functools.partial(_inproj_kernel, col_chunk=1024, fused=fused),
        grid=(bsz, s // tm),
        in_specs=[xspec] + moe_specs + [
            pl.BlockSpec((1, d), lambda b, i: (0, 0)),
            pl.BlockSpec((1, 1, d), lambda b, i: (b, 0, 0)),
            pl.BlockSpec((1, 1, d), lambda b, i: (b, 0, 1)),
            pl.BlockSpec((d, nout), lambda b, i: (0, 0), pipeline_mode=pl.Buffered(1)),
        ],
        out_specs=[proj_spec, xspec] if fused else proj_spec,
        out_shape=[proj_shape, jax.ShapeDtypeStruct((bsz, s, d), F32)] if fused else proj_shape,
        compiler_params=pltpu.CompilerParams(
            dimension_semantics=("parallel", "parallel"), vmem_limit_bytes=V7X_VMEM_LIMIT),
        name="inproj",
    )(x, *moe_args, gain, mod, mod, w)
    return (out[1], out[0]) if fused else (x, out)


def _hgrn_gates(q, f, lb, one_m_lb):
    e = jnp.exp(-jnp.abs(f))
    inv = 1.0 / (1.0 + e)
    pos = f >= 0.0
    t = e * inv
    sig = jnp.where(pos, inv, t)
    sig_neg = jnp.where(pos, t, inv)
    has_lb = lb > 0.0
    logf = jnp.log(jnp.where(has_lb, lb + one_m_lb * sig, inv)) + jnp.where(has_lb, 0.0, jnp.minimum(f, 0.0))
    return q * _sigmoid(q), one_m_lb * sig_neg, logf


def _hgrn_kernel(q_ref, f_ref, i_ref, g_ref, lb_ref, gain_ref, o_ref, kk_s, v_s, b_s,
                 qt_s, qd_s, kd_s, ktt_s, dec_s):
    t = HG_T
    dk = HG_DK
    w = q_ref.shape[2]
    n_chunks = q_ref.shape[1] // t
    lb = lb_ref[...]
    one_m_lb = 1.0 - lb
    gain = gain_ref[...]
    heads = [slice(h * dk, (h + 1) * dk) for h in range(HG_PAIR)]
    row = lax.broadcasted_iota(I32, (t, HG_PAIR * t), 0)
    col = lax.broadcasted_iota(I32, (t, HG_PAIR * t), 1)
    causal = row >= (col % t)
    tril = causal[:, :t].astype(BF16)

    def finish(o, g):
        return (_rms(o) * gain * (g * _sigmoid(g))).astype(BF16)

    def block_diag(parts):
        rows = []
        for h, p in enumerate(parts):
            z = jnp.zeros_like(p)
            rows.append(jnp.concatenate([p if j == h else z for j in range(HG_PAIR)], axis=1))
        return jnp.concatenate(rows, axis=0)

    def prepare(c, bmax):
        rows = pl.ds(pl.multiple_of(c * t, t), t)
        q = q_ref[0, rows, :].astype(F32)
        f = f_ref[0, rows, :].astype(F32)
        qs, kk, logf = _hgrn_gates(q, f, lb, one_m_lb)
        hi = logf.astype(BF16)
        lo = (logf - hi.astype(F32)).astype(BF16)
        bb = _dot(tril, jnp.concatenate([hi, lo], axis=-1))
        b = bb[:, :w] + bb[:, w:]
        b_last = b[t - 1:t, :]
        mid = 0.5 * b_last
        e_mid = jnp.exp(mid)
        qt = qs * jnp.exp(b - mid)
        kt = kk * jnp.exp(mid - b)
        qt_s[rows, :] = qt.astype(BF16)
        qd_s[rows, :] = (qt * e_mid).astype(BF16)
        kd_s[rows, :] = (kt * e_mid).astype(BF16)
        wrows = pl.ds(pl.multiple_of(c * w, w), w)
        ktt_s[wrows, :] = block_diag([kt[:, hs] for hs in heads]).T.astype(BF16)
        dec_s[pl.ds(pl.multiple_of(c * 8, 8), 8), :] = jnp.broadcast_to(e_mid * e_mid, (8, w))
        return jnp.maximum(bmax, jnp.abs(b_last))

    bmax = lax.fori_loop(0, n_chunks, prepare, jnp.zeros_like(lb), unroll=8)

    def chunk(c, sts):
        rows = pl.ds(pl.multiple_of(c * t, t), t)
        wrows = pl.ds(pl.multiple_of(c * w, w), w)
        v = i_ref[0, rows, :]
        g = g_ref[0, rows, :].astype(F32)
        kd = kd_s[rows, :]
        dec_row = dec_s[pl.ds(pl.multiple_of(c * 8, 8), 1), :]
        decay = jnp.broadcast_to(dec_row, (dk, w)).T
        att = _dot(qt_s[rows, :], ktt_s[wrows, :])
        att = jnp.where(causal, att, 0.0).astype(BF16)
        o = _dot(att, block_diag([v[:, hs] for hs in heads]))
        st_bd = block_diag([st.astype(BF16) for st in sts])
        o = o + _dot(qd_s[rows, :], st_bd)
        sts = tuple(st * decay[hs, :] + _dot_tn(kd[:, hs], v[:, hs]) for st, hs in zip(sts, heads))
        out = [finish(o[:, hs], g[:, hs]) for hs in heads]
        o_ref[0, rows, :] = jnp.concatenate(out, axis=1)
        return sts

    st0 = jnp.zeros((dk, dk), F32)
    lax.fori_loop(0, n_chunks, chunk, (st0,) * HG_PAIR, unroll=16)
    safe = 0.5 * jnp.max(bmax) <= HG_MAX_HALF_RANGE

    @pl.when(jnp.logical_not(safe))
    def _():
        n = HG_SUB
        sub_row = lax.broadcasted_iota(I32, (n, 1), 0)
        tril_n = (lax.broadcasted_iota(I32, (n, n), 0) >= lax.broadcasted_iota(I32, (n, n), 1)).astype(F32)

        for hs in heads:
            def block(i, st, hs=hs):
                rows = pl.ds(pl.multiple_of(i * n, n), n)
                q = q_ref[0, rows, hs].astype(F32)
                f = f_ref[0, rows, hs].astype(F32)
                v = i_ref[0, rows, hs]
                g = g_ref[0, rows, hs].astype(F32)
                qs, kk, logf = _hgrn_gates(q, f, lb[:, hs], one_m_lb[:, hs])
                b = jnp.dot(tril_n, logf, precision=HIGHEST, preferred_element_type=F32)
                kk_s[...] = kk
                v_s[...] = v.astype(F32)
                b_s[...] = b
                o = _dot_nt((qs * jnp.exp(b)).astype(BF16), st.astype(BF16))

                def pair(s, acc):
                    dec = jnp.exp(jnp.minimum(b - b_s[pl.ds(s, 1), :], 0.0))
                    wgt = jnp.sum(qs * kk_s[pl.ds(s, 1), :] * dec, axis=-1, keepdims=True)
                    return acc + jnp.where(sub_row >= s, wgt, 0.0) * v_s[pl.ds(s, 1), :]

                o = lax.fori_loop(0, n, pair, o)
                b_last = b[n - 1:n, :]
                st = st * jnp.exp(b_last) + _dot_tn(v, (kk * jnp.exp(b_last - b)).astype(BF16))
                o_ref[0, rows, hs] = finish(o, g)
                return st

            lax.fori_loop(0, q_ref.shape[1] // n, block, st0)


def _hgrn(proj, lb, gain):
    bsz, s, w4 = proj.shape
    w = HG_PAIR * HG_DK
    npair = w4 // (4 * w)

    def spec(j):
        return pl.BlockSpec((1, s, w), lambda b, p: (b, 0, p + j * npair))

    return pl.pallas_call(
        _hgrn_kernel,
        grid=(bsz, npair),
        in_specs=[spec(0), spec(1), spec(2), spec(3),
                  pl.BlockSpec((1, w), lambda b, p: (0, p)),
                  pl.BlockSpec((1, HG_DK), lambda b, p: (0, 0))],
        out_specs=pl.BlockSpec((1, s, w), lambda b, p: (b, 0, p)),
        out_shape=jax.ShapeDtypeStruct((bsz, s, npair * w), BF16),
        scratch_shapes=[pltpu.VMEM((HG_SUB, HG_DK), F32)] * 3 + [pltpu.VMEM((s, w), BF16)] * 3
        + [pltpu.VMEM((s // HG_T * w, HG_PAIR * HG_T), BF16), pltpu.VMEM((s // HG_T * 8, w), F32)],
        compiler_params=pltpu.CompilerParams(dimension_semantics=("parallel", "parallel")),
        name="hgrn",
    )(proj, proj, proj, proj, lb, gain)


def _rope_kernel(pos_ref, inv_ref, cos_ref, sin_ref):
    ang = pos_ref[0].astype(F32) * inv_ref[...]
    cos_ref[0] = jnp.cos(ang)
    sin_ref[0] = jnp.sin(ang)


def _rope_tables(positions):
    bsz, s = positions.shape
    half = RET_DK // 2
    inv_freq = (1.0 / (ROPE_BASE ** jnp.linspace(0.0, 1.0, half, dtype=F32))).reshape(1, half)
    out = jax.ShapeDtypeStruct((bsz, s, half), F32)
    return pl.pallas_call(
        _rope_kernel,
        grid=(bsz,),
        in_specs=[pl.BlockSpec((1, s, 1), lambda b: (b, 0, 0)),
                  pl.BlockSpec((1, half), lambda b: (0, 0))],
        out_specs=[pl.BlockSpec((1, s, half), lambda b: (b, 0, 0))] * 2,
        out_shape=[out, out],
        name="rope_tables",
    )(positions.reshape(bsz, s, 1), inv_freq)


def _ret_kernel(q_ref, k_ref, v_ref, g_ref, cos_ref, sin_ref, lg_ref, o_ref, r_ref, d_ref):
    t = RET_T
    dk = RET_DK
    half = dk // 2
    n_steps = q_ref.shape[1] // t
    lg = lg_ref[0]
    lg_k = lg[:, :dk]
    n = lax.broadcasted_iota(I32, (t, t), 0)
    m = lax.broadcasted_iota(I32, (t, t), 1)
    dist = jnp.abs(n - m).astype(F32)
    visible = (m // RET_CHUNK) <= (n // RET_CHUNK)
    k_scale = dk ** -0.5
    d_ref[...] = jnp.where(visible, k_scale * jnp.exp(dist * lg[:, :t]), 0.0)
    idx = lax.broadcasted_iota(I32, (t, dk), 0).astype(F32)
    q_decay = jnp.exp((idx + 1.0) * lg_k)
    k_decay = k_scale * jnp.exp((t - 1.0 - idx) * lg_k)
    step_decay = jnp.exp(float(t) * lg)
    r_ref[...] = jnp.zeros_like(r_ref)

    def rotate(x, cos, sin):
        x1, x2 = x[:, :half], x[:, half:]
        return jnp.concatenate([x1 * cos - x2 * sin, x2 * cos + x1 * sin], axis=-1)

    def step(c, carry):
        r0 = pl.multiple_of(c * t, t)
        rows = pl.ds(r0, t)
        cos = cos_ref[0, rows, :]
        sin = sin_ref[0, rows, :]
        q = rotate(q_ref[0, rows, :].astype(F32), cos, sin)
        k = rotate(k_ref[0, rows, :].astype(F32), cos, sin)
        v = v_ref[0, rows, :]
        g = g_ref[0, rows, :].astype(F32)
        r = r_ref[...]
        sc = _dot_nt(q.astype(BF16), k.astype(BF16)) * d_ref[...]
        o = _dot(sc.astype(BF16), v) + _dot((q * q_decay).astype(BF16), r.astype(BF16))
        r_ref[...] = r * step_decay + _dot_tn((k * k_decay).astype(BF16), v)
        out = _rms(o) * (g * _sigmoid(g))
        o_ref[0, rows, :] = out.astype(BF16)
        return carry

    lax.fori_loop(0, n_steps, step, 0, unroll=2)


def _retention(proj, cos, sin):
    bsz, s, w = proj.shape
    dk, dv = RET_DK, RET_DV
    nh = w // (2 * dk + 2 * dv)
    hidx = jnp.arange(nh, dtype=F32)
    log_gamma = jnp.log(1.0 - 2.0 ** (-5.0 - hidx))
    lg = jnp.broadcast_to(log_gamma[:, None, None], (nh, 1, dv))
    vbase = 2 * nh * dk // dv
    return pl.pallas_call(
        _ret_kernel,
        grid=(bsz, nh),
        in_specs=[
            pl.BlockSpec((1, s, dk), lambda b, h: (b, 0, h)),
            pl.BlockSpec((1, s, dk), lambda b, h: (b, 0, nh + h)),
            pl.BlockSpec((1, s, dv), lambda b, h: (b, 0, vbase + h)),
            pl.BlockSpec((1, s, dv), lambda b, h: (b, 0, vbase + nh + h)),
            pl.BlockSpec((1, s, dk // 2), lambda b, h: (b, 0, 0)),
            pl.BlockSpec((1, s, dk // 2), lambda b, h: (b, 0, 0)),
            pl.BlockSpec((1, 1, dv), lambda b, h: (h, 0, 0)),
        ],
        out_specs=pl.BlockSpec((1, s, dv), lambda b, h: (b, 0, h)),
        out_shape=jax.ShapeDtypeStruct((bsz, s, nh * dv), BF16),
        scratch_shapes=[pltpu.VMEM((dk, dv), F32), pltpu.VMEM((RET_T, RET_T), F32)],
        compiler_params=pltpu.CompilerParams(
            dimension_semantics=("parallel", "parallel"), vmem_limit_bytes=V7X_VMEM_LIMIT),
        name="retention",
    )(proj, proj, proj, proj, cos, sin, lg)


def _outproj_kernel(o_ref, w_ref, x_ref, g1_ref, sh_ref, sc_ref, gain_ref, rw_ref, rb_ref, u_ref,
                    xo_ref, ha_ref, hb_ref, eidx_ref, gate_ref, rank_ref, cnt_ref, base_ref):
    first = jnp.logical_and(pl.program_id(0) == 0, pl.program_id(1) == 0)

    @pl.when(first)
    def _():
        base_ref[...] = jnp.zeros_like(base_ref)

    ne = rw_ref.shape[0]
    rw = rw_ref[...]
    rw_hi = rw.astype(BF16)
    rw_lo = (rw - rw_hi.astype(F32)).astype(BF16)
    rw_both = jnp.concatenate([rw_hi, rw_lo], axis=0)
    tm = x_ref.shape[1]
    sub = min(OUTPROJ_SUB_ROWS, tm)
    h_groups = []
    for r in range(0, tm, sub):
        rows = slice(r, r + sub)
        y = _dot(o_ref[0, rows, :], w_ref[...])
        xn = x_ref[0, rows, :] + g1_ref[0] * y
        xo_ref[0, rows, :] = xn
        h = _norm_mod(xn, gain_ref[...], sc_ref[0], sh_ref[0])
        packed = _pack_rows(h)
        slab = packed.shape[1] // ROW_PARTS
        ha_ref[0, rows, :] = packed[:, :slab]
        hb_ref[0, rows, :] = packed[:, slab:]
        h_groups.append(h.astype(BF16))

    part = _dot_nt(rw_both, jnp.concatenate(h_groups, axis=0))
    work = part[:ne] + part[ne:] + rb_ref[...]
    eiota = lax.broadcasted_iota(I32, (ne, tm), 0)
    onehots, tops = [], []
    for k in range(TOP_K):
        mx = jnp.max(work, axis=0, keepdims=True)
        idx = jnp.min(jnp.where(work == mx, eiota, ne), axis=0, keepdims=True)
        oh = eiota == idx
        work = jnp.where(oh, -jnp.inf, work)
        eidx_ref[k:k + 1, :] = idx
        onehots.append(oh)
        tops.append(mx)
    ex = [jnp.exp(m - tops[0]) for m in tops]
    denom = ex[0] + ex[1] + ex[2] + ex[3]
    gates = [e / denom for e in ex] + [jnp.zeros_like(denom)] * (GATE_LANES - TOP_K)
    gate_ref[...] = jnp.concatenate(gates, axis=0).T

    mask = jnp.zeros((ne, tm), F32)
    for oh in onehots:
        mask = mask + oh.astype(F32)
    incl = _dot(mask.astype(BF16), u_ref[...])
    excl = incl - mask + base_ref[...]
    for k in range(TOP_K):
        rk = jnp.sum(jnp.where(onehots[k], excl, 0.0), axis=0, keepdims=True)
        rank_ref[k:k + 1, :] = rk.astype(I32)
    total = base_ref[...] + incl[:, tm - 1:tm]
    base_ref[...] = total
    cnt_ref[...] = total.astype(I32)


def _outproj_route(o, w_out, x, mod, gain2, router_wt, router_b):
    bsz, s, d = x.shape
    hv = o.shape[2]
    tm = min(OUTPROJ_ROW_TILE, s)
    n = bsz * s
    nt = s // tm
    ne = router_wt.shape[0]
    upper = (jnp.arange(tm)[:, None] <= jnp.arange(tm)[None, :]).astype(BF16)

    def modspec(j):
        return pl.BlockSpec((1, 1, d), lambda b, i: (b, 0, j))

    tokspec = pl.BlockSpec((TOP_K, tm), lambda b, i: (0, b * nt + i))
    return pl.pallas_call(
        _outproj_kernel,
        grid=(bsz, nt),
        in_specs=[
            pl.BlockSpec((1, tm, hv), lambda b, i: (b, i, 0)),
            pl.BlockSpec((hv, d), lambda b, i: (0, 0)),
            pl.BlockSpec((1, tm, d), lambda b, i: (b, i, 0)),
            modspec(2), modspec(3), modspec(4),
            pl.BlockSpec((1, d), lambda b, i: (0, 0)),
            pl.BlockSpec((ne, d), lambda b, i: (0, 0)),
            pl.BlockSpec((ne, 1), lambda b, i: (0, 0)),
            pl.BlockSpec((tm, tm), lambda b, i: (0, 0)),
        ],
        out_specs=[
            pl.BlockSpec((1, tm, d), lambda b, i: (b, i, 0)),
            pl.BlockSpec((1, tm, d // 4), lambda b, i: (b, i, 0)),
            pl.BlockSpec((1, tm, d // 4), lambda b, i: (b, i, 0)),
            tokspec,
            pl.BlockSpec((tm, GATE_LANES), lambda b, i: (b * nt + i, 0)),
            tokspec,
            pl.BlockSpec((ne, 1), lambda b, i: (0, 0)),
        ],
        out_shape=[
            jax.ShapeDtypeStruct((bsz, s, d), F32),
            jax.ShapeDtypeStruct((bsz, s, d // 4), I32),
            jax.ShapeDtypeStruct((bsz, s, d // 4), I32),
            jax.ShapeDtypeStruct((TOP_K, n), I32),
            jax.ShapeDtypeStruct((n, GATE_LANES), F32),
            jax.ShapeDtypeStruct((TOP_K, n), I32),
            jax.ShapeDtypeStruct((ne, 1), I32),
        ],
        scratch_shapes=[pltpu.VMEM((ne, 1), F32)],
        compiler_params=pltpu.CompilerParams(
            dimension_semantics=("arbitrary", "arbitrary"), vmem_limit_bytes=V7X_VMEM_LIMIT),
        name="outproj_route",
    )(o, w_out, x, mod, mod, mod, gain2, router_wt, router_b, upper)


def _moe_kernel(be_ref, nb_ref, xa_ref, xb_ref, w1_ref, b1_ref, w2_ref, b2_ref, ya_ref, yb_ref,
                w1_s, w2_s):
    i = pl.program_id(0)
    used = i < nb_ref[0]
    new_expert = jnp.logical_or(i == 0, be_ref[i] != be_ref[jnp.maximum(i - 1, 0)])

    @pl.when(jnp.logical_and(used, new_expert))
    def _():
        for w_ref, w_s in ((w1_ref, w1_s), (w2_ref, w2_s)):
            for r in range(0, w_s.shape[0], WEIGHT_CAST_ROWS):
                rows = slice(r, r + WEIGHT_CAST_ROWS)
                w_s[rows, :] = w_ref[0, 0, rows, :].astype(BF16)

    @pl.when(used)
    def _():
        f = w2_s.shape[0]
        for r in range(0, xa_ref.shape[0], MOE_SUB_ROWS):
            rows = slice(r, r + MOE_SUB_ROWS)
            x = _unpack_rows(jnp.concatenate([xa_ref[rows, :], xb_ref[rows, :]], axis=1)).astype(BF16)
            u = _dot(x, w1_s[...]) + b1_ref[0, 0]
            glu = jnp.minimum(u[:, :f], SWIGLU_LIMIT)
            lin = jnp.clip(u[:, f:], -SWIGLU_LIMIT, SWIGLU_LIMIT)
            a = glu * _sigmoid(SWIGLU_ALPHA * glu) * (lin + 1.0)
            y = _dot(a.astype(BF16), w2_s[...]) + b2_ref[0, 0]
            packed = _pack_rows(y)
            slab = packed.shape[1] // ROW_PARTS
            ya_ref[rows, :] = packed[:, :slab]
            yb_ref[rows, :] = packed[:, slab:]

    @pl.when(jnp.logical_not(used))
    def _():
        ya_ref[...] = jnp.zeros_like(ya_ref)
        yb_ref[...] = jnp.zeros_like(yb_ref)


def _moe_blocks(block_e, n_used, xs, layer, w1, b1, w2, b2):
    n_rows, dh = xs[0].shape
    nl, ne, d, f2 = w1.shape
    f = f2 // 2
    nblk = n_rows // MOE_ROWS
    grid_spec = pltpu.PrefetchScalarGridSpec(
        num_scalar_prefetch=2,
        grid=(nblk,),
        in_specs=[
            pl.BlockSpec((MOE_ROWS, dh), lambda i, be, nb: (i, 0)),
            pl.BlockSpec((MOE_ROWS, dh), lambda i, be, nb: (i, 0)),
            pl.BlockSpec((1, 1, d, f2), lambda i, be, nb: (layer, be[i], 0, 0)),
            pl.BlockSpec((1, 1, 1, f2), lambda i, be, nb: (layer, be[i], 0, 0)),
            pl.BlockSpec((1, 1, f, d), lambda i, be, nb: (layer, be[i], 0, 0)),
            pl.BlockSpec((1, 1, 1, d), lambda i, be, nb: (layer, be[i], 0, 0)),
        ],
        out_specs=[pl.BlockSpec((MOE_ROWS, dh), lambda i, be, nb: (i, 0))] * 2,
        scratch_shapes=[pltpu.VMEM((d, f2), BF16), pltpu.VMEM((f, d), BF16)],
    )
    return pl.pallas_call(
        _moe_kernel,
        grid_spec=grid_spec,
        out_shape=[jax.ShapeDtypeStruct((n_rows, dh), I32)] * 2,
        compiler_params=pltpu.CompilerParams(
            dimension_semantics=("arbitrary",), vmem_limit_bytes=V7X_VMEM_LIMIT),
        name="moe_experts",
    )(block_e, n_used, xs[0], xs[1], w1, b1.reshape(nl, ne, 1, f2), w2, b2.reshape(nl, ne, 1, d))


def _sc_mesh():
    return plsc.VectorSubcoreMesh(core_axis_name="c", subcore_axis_name="s")


def _sc_scatter_rows(srcs, dests, n_rows):
    n, w = srcs[0].shape
    ns, nk = len(srcs), len(dests)
    out = jax.ShapeDtypeStruct((n_rows, w), srcs[0].dtype)

    @functools.partial(pl.kernel, out_type=[out] * ns, mesh=_sc_mesh(),
                       scratch_types=[pltpu.SemaphoreType.DMA((nk,))])
    def scatter_kernel(*refs):
        x_hbm, idx_hbm, o_hbm, sems = refs[:ns], refs[ns:ns + nk], refs[ns + nk:-1], refs[-1]
        for x, o in zip(x_hbm, o_hbm):
            def body(x_vmem, *idx_vmem, o=o):
                copies = [pltpu.async_copy(x_vmem, o.at[iv.at[0]], sems.at[k])
                          for k, iv in enumerate(idx_vmem)]
                for cp in copies:
                    cp.wait()

            pltpu.emit_pipeline(
                body,
                grid=(n // SC_WINDOW,),
                in_specs=[pl.BlockSpec((SC_WINDOW, w), lambda i: (i, 0))]
                + [pl.BlockSpec((1, SC_WINDOW), lambda i: (0, i))] * nk,
                out_specs=[],
                core_axis_name=("c", "s"),
                dimension_semantics=(pltpu.PARALLEL,),
            )(x, *idx_hbm)

    return scatter_kernel(*srcs, *dests)


def _sc_gather_rows(tables, idx):
    m = idx.shape[1]
    w = tables[0].shape[1]
    nt = len(tables)
    out = jax.ShapeDtypeStruct((m, w), tables[0].dtype)

    @functools.partial(pl.kernel, out_type=[out] * nt, mesh=_sc_mesh(), scratch_types=[])
    def gather_kernel(*refs):
        t_hbm, i_hbm, o_hbm = refs[:nt], refs[nt], refs[nt + 1:]
        for t, o in zip(t_hbm, o_hbm):
            def body(i_vmem, o_vmem, t=t):
                pltpu.sync_copy(t.at[i_vmem.at[0]], o_vmem)

            pltpu.emit_pipeline(
                body,
                grid=(m // SC_WINDOW,),
                in_specs=[pl.BlockSpec((1, SC_WINDOW), lambda i: (0, i))],
                out_specs=[pl.BlockSpec((SC_WINDOW, w), lambda i: (i, 0))],
                core_axis_name=("c", "s"),
                dimension_semantics=(pltpu.PARALLEL,),
            )(i_hbm, o)

    return gather_kernel(*tables, idx)


def _dest_kernel(ps_ref, eidx_ref, rank_ref, o_ref):
    eidx = eidx_ref[...]
    start = jnp.zeros_like(eidx)
    for e in range(ps_ref.shape[0]):
        start = jnp.where(eidx == e, ps_ref[e], start)
    o_ref[...] = start + rank_ref[...]


def _dest_rows(pad_start, eidx, rank):
    k, n = eidx.shape
    tn = min(8192, n)
    blk = pl.BlockSpec((k, tn), lambda i, ps: (0, i))
    return pl.pallas_call(
        _dest_kernel,
        grid_spec=pltpu.PrefetchScalarGridSpec(
            num_scalar_prefetch=1, grid=(n // tn,), in_specs=[blk, blk], out_specs=blk),
        out_shape=jax.ShapeDtypeStruct((k, n), I32),
        name="dest_rows",
    )(pad_start, eidx, rank)


def _final_kernel(x_ref, *refs):
    g_ref, sh_ref, sc_ref, o_ref = refs[N_MOE_REFS:]
    x = _moe_residual(x_ref, refs[:N_MOE_REFS])
    o_ref[0] = _norm_mod(x, g_ref[...], sc_ref[0], sh_ref[0])


def _final(x, pending, gain, fmod):
    bsz, s, d = x.shape
    tm = min(ROW_TILE, s)
    blk = pl.BlockSpec((1, tm, d), lambda b, i: (b, i, 0))
    moe_specs, moe_args = _moe_residual_operands(pending, bsz, s, d, tm, lambda b, i: (b, i))
    return pl.pallas_call(
        _final_kernel,
        grid=(bsz, s // tm),
        in_specs=[blk] + moe_specs + [
            pl.BlockSpec((1, d), lambda b, i: (0, 0)),
            pl.BlockSpec((1, 1, d), lambda b, i: (b, 0, 0)),
            pl.BlockSpec((1, 1, d), lambda b, i: (b, 0, 1))],
        out_specs=blk,
        out_shape=jax.ShapeDtypeStruct((bsz, s, d), F32),
        compiler_params=pltpu.CompilerParams(dimension_semantics=("parallel", "parallel")),
        name="final_norm",
    )(x, *moe_args, gain, fmod, fmod)


def _moe_layer(h2, eidx, gate, rank, counts, layer, w1, b1, w2, b2):
    bsz, s, dh = h2[0].shape
    n = bsz * s
    ne = w1.shape[1]
    nblk = -(-(n * TOP_K) // MOE_ROWS) + ne
    n_rows = nblk * MOE_ROWS
    counts = counts[:, 0]
    padded = (counts + MOE_ROWS - 1) // MOE_ROWS * MOE_ROWS
    pad_end = jnp.cumsum(padded)
    pad_start = pad_end - padded
    block_start = jnp.arange(nblk, dtype=I32)[:, None] * MOE_ROWS
    block_e = jnp.minimum(jnp.sum(pad_end[None, :] <= block_start, axis=1), ne - 1).astype(I32)
    n_used = (pad_end[-1:] // MOE_ROWS).astype(I32)
    dest = _dest_rows(pad_start.astype(I32), eidx, rank)
    xs = _sc_scatter_rows([h.reshape(n, dh) for h in h2], [dest[k:k + 1] for k in range(TOP_K)], n_rows)
    ys = _moe_blocks(block_e, n_used, xs, layer, w1, b1, w2, b2)
    yg = _sc_gather_rows(ys, dest.reshape(1, TOP_K * n))
    return yg, gate


def _hgrn_lower_bounds(lb_logits):
    p = jax.nn.softmax(lb_logits.astype(F32), axis=0)
    cum = jnp.cumsum(p, axis=0)
    return cum - cum[0:1]


def kernel(x, c, positions, ada_w, ada_b, norm1_g, norm2_g, hgrn_w_in, hgrn_w_out, hgrn_o_gain, hgrn_lb_logits, ret_w_in, ret_w_out, router_w, router_b, moe_w1, moe_b1, moe_w2, moe_b2, final_g, final_ada_w, final_ada_b):
    depth = ada_w.shape[0]
    bsz, s, d = x.shape
    tm = min(ROW_TILE, s)
    assert s % tm == 0 and s % min(OUTPROJ_ROW_TILE, s) == 0 and s % RET_T == 0 and s % HG_T == 0
    assert (bsz * s) % SC_WINDOW == 0
    assert hgrn_w_in.shape[2] % (4 * HG_PAIR * HG_DK) == 0 and d % (2 * ROW_PARTS * 128) == 0
    assert router_w.shape[2] >= TOP_K == 4
    mods = _ada(c, ada_w, ada_b)
    fmod = _ada(c, final_ada_w[None], final_ada_b[None])[0][:, None, :]
    lbs = _hgrn_lower_bounds(hgrn_lb_logits)
    cos, sin = _rope_tables(positions)
    pending = None
    for layer in range(depth):
        mod = mods[layer][:, None, :]
        j = layer // N_MIXERS
        if layer % N_MIXERS == 0:
            x, proj = _inproj(x, pending, norm1_g[layer][None], mod, hgrn_w_in[j].astype(BF16))
            o = _hgrn(proj, lbs[j][None], hgrn_o_gain[j][None])
            w_out = hgrn_w_out[j]
        else:
            x, proj = _inproj(x, pending, norm1_g[layer][None], mod, ret_w_in[j].astype(BF16))
            o = _retention(proj, cos, sin)
            w_out = ret_w_out[j]
        x, ha, hb, eidx, gate, rank, counts = _outproj_route(
            o, w_out.astype(BF16), x, mod, norm2_g[layer][None],
            router_w[layer].T, router_b[layer][:, None])
        yg, gate_rows = _moe_layer((ha, hb), eidx, gate, rank, counts, layer, moe_w1, moe_b1, moe_w2, moe_b2)
        pending = (yg, gate_rows, mod)
    return _final(x, pending, final_g[None], fmod)
```
